```python
import math
import jax, jax.numpy as jnp
from jax import lax
import numpy as np

D_MODEL = 1024
BATCH = 4
SEQ = 4096
DEPTH = 2

HEAD_DIM = 64
N_Q_HEADS = 8
N_KV_HEADS = 2
Q_PER_KV = N_Q_HEADS // N_KV_HEADS
D_ATTN = N_Q_HEADS * HEAD_DIM
D_KV = N_KV_HEADS * HEAD_DIM
ROT_HALF = HEAD_DIM // 2
ROPE_THETA = 10000.0
GRID_W = 64
Q_BLOCK = 128
D_SSM = D_MODEL // 2
SSM_GROUP = 16
N_SSM_GROUPS = D_SSM // SSM_GROUP
SSM_STATE = 64
DT_MIN = 0.001
DT_MAX = 0.1
N_EXPERTS = 16
CAPACITY_FACTOR = 2
D_FF_EXPERT = 2 * D_MODEL
N_IN = D_SSM + D_ATTN + 2 * D_KV + 2 * D_MODEL
SPLITS = (D_SSM, D_SSM + D_ATTN, D_SSM + D_ATTN + D_KV, D_SSM + D_ATTN + 2 * D_KV,
          D_SSM + D_ATTN + 2 * D_KV + D_MODEL)
EPS = 1e-6

kernel_name = "hybrid_s5_gqa_ec_moe_encoder"


def rmsnorm(x, w):
    xf = x.astype(jnp.float32)
    y = xf * lax.rsqrt(jnp.mean(xf * xf, axis=-1, keepdims=True) + EPS)
    return (y * w.astype(jnp.float32)).astype(x.dtype)


def modulate(x, w, shift, scale):
    return rmsnorm(x, w) * (1 + scale[:, None, :]) + shift[:, None, :]


def axial_rope_tables(S):
    rows = S // GRID_W
    row = jnp.repeat(jnp.arange(rows), GRID_W).astype(jnp.float32)
    col = jnp.tile(jnp.arange(GRID_W), rows).astype(jnp.float32)
    inv = 1.0 / (ROPE_THETA ** (jnp.arange(0, ROT_HALF, 2, dtype=jnp.float32) / ROT_HALF))
    ang_r = row[:, None] * inv[None, :]
    ang_c = col[:, None] * inv[None, :]
    return (jnp.cos(ang_r), jnp.sin(ang_r), jnp.cos(ang_c), jnp.sin(ang_c))


def rope_1d(x, cos, sin):
    x1, x2 = jnp.split(x, 2, axis=-1)
    cos = cos[None, :, None, :]
    sin = sin[None, :, None, :]
    return jnp.concatenate([x1 * cos - x2 * sin, x1 * sin + x2 * cos], axis=-1)


def rope_axial(x, tabs):
    cr, sr, cc, sc = tabs
    xf = x.astype(jnp.float32)
    xr, xc = jnp.split(xf, 2, axis=-1)
    out = jnp.concatenate([rope_1d(xr, cr, sr), rope_1d(xc, cc, sc)], axis=-1)
    return out.astype(x.dtype)


def block_attention(q, k, v):
    B, S = q.shape[0], q.shape[1]
    nb = S // Q_BLOCK
    qb = q.reshape(B, nb, Q_BLOCK, N_KV_HEADS, Q_PER_KV, HEAD_DIM).transpose(1, 0, 2, 3, 4, 5)
    scale = HEAD_DIM ** -0.5

    def one_block(qi):
        s = jnp.einsum('bqhgd,bshd->bhgqs', qi, k).astype(jnp.float32) * scale
        p = jax.nn.softmax(s, axis=-1).astype(v.dtype)
        return jnp.einsum('bhgqs,bshd->bqhgd', p, v)

    o = lax.map(one_block, qb)
    return o.transpose(1, 0, 2, 3, 4, 5).reshape(B, S, D_ATTN)


def s5_direction(u_g, lam_re, lam_im, log_dt, b_re, b_im, c_re, c_im, reverse):
    lam_re = lam_re.astype(jnp.float32)
    lam_im = lam_im.astype(jnp.float32)
    dt = jnp.exp(log_dt.astype(jnp.float32))[:, None]
    mag = jnp.exp(lam_re * dt)
    lb_re = mag * jnp.cos(lam_im * dt)
    lb_im = mag * jnp.sin(lam_im * dt)
    nr = lb_re - 1.0
    ni = lb_im
    den = lam_re * lam_re + lam_im * lam_im
    f_re = ((nr * lam_re + ni * lam_im) / den)[..., None]
    f_im = ((ni * lam_re - nr * lam_im) / den)[..., None]
    b_re = b_re.astype(jnp.float32)
    b_im = b_im.astype(jnp.float32)
    bb_re = f_re * b_re - f_im * b_im
    bb_im = f_re * b_im + f_im * b_re
    bu_re = jnp.einsum('bsgc,gpc->bsgp', u_g, bb_re)
    bu_im = jnp.einsum('bsgc,gpc->bsgp', u_g, bb_im)
    a_re = jnp.broadcast_to(lb_re, bu_re.shape)
    a_im = jnp.broadcast_to(lb_im, bu_im.shape)

    def combine(e1, e2):
        a1r, a1i, b1r, b1i = e1
        a2r, a2i, b2r, b2i = e2
        return (a1r * a2r - a1i * a2i,
                a1r * a2i + a1i * a2r,
                a2r * b1r - a2i * b1i + b2r,
                a2r * b1i + a2i * b1r + b2i)

    _, _, x_re, x_im = lax.associative_scan(combine, (a_re, a_im, bu_re, bu_im), axis=1, reverse=reverse)
    return (jnp.einsum('bsgp,gcp->bsgc', x_re, c_re.astype(jnp.float32))
            - jnp.einsum('bsgp,gcp->bsgc', x_im, c_im.astype(jnp.float32)))


def token_mixer(h, tabs, w_in, lam_re, lam_im, log_dt, b_re, b_im, c_re, c_im, d_skip,
                w_glu, b_glu, q_norm_w, k_norm_w, w_ssm_up, w_attn_up, w_out):
    B, S, _ = h.shape
    proj = h @ w_in
    u, q, k, v, g_s, g_a = jnp.split(proj, SPLITS, axis=-1)
    uf = u.astype(jnp.float32)
    ug = uf.reshape(B, S, N_SSM_GROUPS, SSM_GROUP)
    y = (s5_direction(ug, lam_re[0], lam_im[0], log_dt[0], b_re[0], b_im[0], c_re[0], c_im[0], False)
         + s5_direction(ug, lam_re[1], lam_im[1], log_dt[1], b_re[1], b_im[1], c_re[1], c_im[1], True))
    y = y.reshape(B, S, D_SSM) + d_skip.astype(jnp.float32) * uf
    z = jax.nn.gelu(y).astype(h.dtype)
    ssm_out = z * jax.nn.sigmoid(z @ w_glu + b_glu)
    q = rope_axial(rmsnorm(q.reshape(B, S, N_Q_HEADS, HEAD_DIM), q_norm_w), tabs)
    k = rope_axial(rmsnorm(k.reshape(B, S, N_KV_HEADS, HEAD_DIM), k_norm_w), tabs)
    v = v.reshape(B, S, N_KV_HEADS, HEAD_DIM)
    attn_out = block_attention(q, k, v)
    merged = jax.nn.sigmoid(g_s) * (ssm_out @ w_ssm_up) + jax.nn.sigmoid(g_a) * (attn_out @ w_attn_up)
    return merged @ w_out


def expert_choice_moe(h, w_router, w_g, w_u, w_d):
    B, S, D = h.shape
    cap = CAPACITY_FACTOR * S // N_EXPERTS
    aff = jax.nn.softmax((h @ w_router).astype(jnp.float32), axis=-1)
    gates, idx = lax.top_k(jnp.swapaxes(aff, 1, 2), cap)
    xg = jax.vmap(lambda hb, ib: hb[ib])(h, idx)
    hid = jax.nn.silu(jnp.einsum('becd,edf->becf', xg, w_g)) * jnp.einsum('becd,edf->becf', xg, w_u)
    ye = jnp.einsum('becf,efd->becd', hid, w_d) * gates[..., None].astype(h.dtype)
    out = jax.vmap(lambda yb, ib: jax.ops.segment_sum(yb.reshape(-1, D), ib.reshape(-1), num_segments=S))(ye, idx)
    return out.astype(h.dtype)


def setup_inputs(seed: int = 0) -> dict:
    key = jax.random.key(seed)
    ks = jax.random.split(key, 32)
    L, D, G, P, Cg = DEPTH, D_MODEL, N_SSM_GROUPS, SSM_STATE, SSM_GROUP
    nrm = lambda k, shape, s: jax.random.normal(k, shape, jnp.float32) * s
    lam_im_init = jnp.broadcast_to(math.pi * jnp.arange(P, dtype=jnp.float32), (L, 2, G, P))
    return {
        "x": nrm(ks[0], (BATCH, SEQ, D), 1.0),
        "c": nrm(ks[1], (BATCH, D), 1.0),
        "w_mod": nrm(ks[2], (L, D, 6 * D), 0.5 * D ** -0.5),
        "b_mod": nrm(ks[3], (L, 6 * D), 0.02),
        "ln1_w": 1.0 + nrm(ks[4], (L, D), 0.02),
        "ln2_w": 1.0 + nrm(ks[5], (L, D), 0.02),
        "w_in": nrm(ks[6], (L, D, N_IN), D ** -0.5),
        "ssm_lam_re": -0.5 + nrm(ks[7], (L, 2, G, P), 0.01),
        "ssm_lam_im": lam_im_init + nrm(ks[8], (L, 2, G, P), 0.01),
        "ssm_log_dt": jax.random.uniform(ks[9], (L, 2, G), jnp.float32, math.log(DT_MIN), math.log(DT_MAX)),
        "ssm_b_re": nrm(ks[10], (L, 2, G, P, Cg), (2 * Cg) ** -0.5),
        "ssm_b_im": nrm(ks[11], (L, 2, G, P, Cg), (2 * Cg) ** -0.5),
        "ssm_c_re": nrm(ks[12], (L, 2, G, Cg, P), P ** -0.5),
        "ssm_c_im": nrm(ks[13], (L, 2, G, Cg, P), P ** -0.5),
        "ssm_d": nrm(ks[14], (L, D_SSM), 1.0),
        "w_glu": nrm(ks[15], (L, D_SSM, D_SSM), D_SSM ** -0.5),
        "b_glu": nrm(ks[16], (L, D_SSM), 0.02),
        "q_norm_w": 1.0 + nrm(ks[17], (L, HEAD_DIM), 0.02),
        "k_norm_w": 1.0 + nrm(ks[18], (L, HEAD_DIM), 0.02),
        "w_ssm_up": nrm(ks[19], (L, D_SSM, D), D_SSM ** -0.5),
        "w_attn_up": nrm(ks[20], (L, D_ATTN, D), D_ATTN ** -0.5),
        "w_out": nrm(ks[21], (L, D, D), D ** -0.5),
        "w_router": nrm(ks[22], (L, D, N_EXPERTS), D ** -0.5),
        "w_exp_gate": nrm(ks[23], (L, N_EXPERTS, D, D_FF_EXPERT), D ** -0.5),
        "w_exp_up": nrm(ks[24], (L, N_EXPERTS, D, D_FF_EXPERT), D ** -0.5),
        "w_exp_down": nrm(ks[25], (L, N_EXPERTS, D_FF_EXPERT, D), D_FF_EXPERT ** -0.5),
    }


def reference(x, c, w_mod, b_mod, ln1_w, ln2_w, w_in, ssm_lam_re, ssm_lam_im, ssm_log_dt,
              ssm_b_re, ssm_b_im, ssm_c_re, ssm_c_im, ssm_d, w_glu, b_glu, q_norm_w, k_norm_w,
              w_ssm_up, w_attn_up, w_out, w_router, w_exp_gate, w_exp_up, w_exp_down):
    tabs = axial_rope_tables(x.shape[1])
    c_act = jax.nn.silu(c)
    for l in range(DEPTH):
        mod = c_act @ w_mod[l] + b_mod[l]
        sh1, sc1, g1, sh2, sc2, g2 = jnp.split(mod, 6, axis=-1)
        h = modulate(x, ln1_w[l], sh1, sc1)
        mix = token_mixer(h, tabs, w_in[l], ssm_lam_re[l], ssm_lam_im[l], ssm_log_dt[l],
                          ssm_b_re[l], ssm_b_im[l], ssm_c_re[l], ssm_c_im[l], ssm_d[l],
                          w_glu[l], b_glu[l], q_norm_w[l], k_norm_w[l],
                          w_ssm_up[l], w_attn_up[l], w_out[l])
        x = x + g1[:, None, :] * mix
        h = modulate(x, ln2_w[l], sh2, sc2)
        x = x + g2[:, None, :] * expert_choice_moe(h, w_router[l], w_exp_gate[l], w_exp_up[l], w_exp_down[l])
    return x
```

```python
import functools
import math

import jax
import jax.numpy as jnp
from jax import lax
from jax.experimental import pallas as pl
from jax.experimental.pallas import tpu as pltpu

F32 = jnp.float32
BF16 = jnp.bfloat16

HEAD_DIM = 64
N_Q_HEADS = 8
N_KV_HEADS = 2
Q_PER_KV = N_Q_HEADS // N_KV_HEADS
ROT_HALF = HEAD_DIM // 2
ROPE_THETA = 10000.0
GRID_W = 64
SSM_GROUP = 16
SSM_STATE = 64
N_EXPERTS = 16
CAPACITY_FACTOR = 2
EPS = 1e-6

S5_CHUNK = 16
S5_PAIR = 2
LANES = 128
V7X_VMEM_BYTES = 64 * 1024 * 1024
LOG2E = 1.4426950408889634


def _cparams(semantics, vmem_mb):
    return pltpu.CompilerParams(dimension_semantics=semantics,
                                vmem_limit_bytes=int(vmem_mb * 1024 * 1024))


def _mod_kernel(c_ref, w_ref, b_ref, o_ref):
    c = c_ref[...]
    ca = c * jax.nn.sigmoid(c)
    o_ref[0] = jnp.dot(ca, w_ref[0], preferred_element_type=F32,
                       precision=lax.Precision.HIGHEST) + b_ref[0]


def _mod_call(c_pad, w_mod, b_mod):
    depth, d, n = w_mod.shape
    rows = c_pad.shape[0]
    tn = 1536
    return pl.pallas_call(
        _mod_kernel,
        grid=(depth, n // tn),
        in_specs=[pl.BlockSpec((rows, d), lambda l, j: (0, 0)),
                  pl.BlockSpec((1, d, tn), lambda l, j: (l, 0, j)),
                  pl.BlockSpec((1, 1, tn), lambda l, j: (l, 0, j))],
        out_specs=pl.BlockSpec((1, rows, tn), lambda l, j: (l, 0, j)),
        out_shape=jax.ShapeDtypeStruct((depth, rows, n), F32),
        name="adaln_mod",
        compiler_params=_cparams(("arbitrary", "arbitrary"), 32),
    )(c_pad, w_mod, b_mod.reshape(depth, 1, n))


def _head_mean_square(xsq, hm):
    hi = xsq.astype(BF16)
    lo = (xsq - hi.astype(F32)).astype(BF16)
    s = jnp.dot(hi, hm, preferred_element_type=F32) + jnp.dot(lo, hm, preferred_element_type=F32)
    return s * (1.0 / HEAD_DIM)


def _rope_block(blk, cos, sa, sb):
    return (blk * cos + pltpu.roll(blk, LANES - ROT_HALF // 2, 1) * sa
            + pltpu.roll(blk, ROT_HALF // 2, 1) * sb)


def _in_kernel(x_ref, mod_ref, ln_ref, w_ref, cos_ref, sa_ref, sb_ref, qw_ref, kw_ref, hm_ref,
               u_ref, q_ref, k_ref, v_ref, gs_ref, ga_ref, *, d_model, d_ssm, d_attn, d_kv):
    x = x_ref[...]
    mod = mod_ref[0]
    sh = mod[:, 0:d_model]
    sc = mod[:, d_model:2 * d_model]
    ms = jnp.mean(x * x, axis=-1, keepdims=True)
    h = (x * lax.rsqrt(ms + EPS) * ln_ref[...]) * (1.0 + sc) + sh
    proj = jnp.dot(h.astype(BF16), w_ref[...], preferred_element_type=F32)
    o = 0
    u_ref[...] = proj[:, o:o + d_ssm].astype(u_ref.dtype)
    o += d_ssm
    q = proj[:, o:o + d_attn]
    o += d_attn
    k = proj[:, o:o + d_kv]
    o += d_kv
    v_ref[...] = proj[:, o:o + d_kv].astype(v_ref.dtype)
    o += d_kv
    gs_ref[...] = jax.nn.sigmoid(proj[:, o:o + d_model]).astype(gs_ref.dtype)
    o += d_model
    ga_ref[...] = jax.nn.sigmoid(proj[:, o:o + d_model]).astype(ga_ref.dtype)

    hm = hm_ref[...]
    cos = cos_ref[...]
    sa = sa_ref[...]
    sb = sb_ref[...]
    qn = q * lax.rsqrt(_head_mean_square(q * q, hm) + EPS) * qw_ref[...]
    kn = k * lax.rsqrt(_head_mean_square(k * k, hm[:d_kv, :d_kv]) + EPS) * kw_ref[...]
    qscale = (HEAD_DIM ** -0.5) * LOG2E
    qr = [_rope_block(qn[:, j * LANES:(j + 1) * LANES], cos, sa, sb) for j in range(d_attn // LANES)]
    q_ref[...] = (jnp.concatenate(qr, axis=-1) * qscale).astype(q_ref.dtype)
    kr = [_rope_block(kn[:, j * LANES:(j + 1) * LANES], cos, sa, sb) for j in range(d_kv // LANES)]
    k_ref[...] = jnp.concatenate(kr, axis=-1).astype(k_ref.dtype)


def _in_call(x2, mod3, ln_w, w_in, tabs, qw, kw, hm, *, seq, tm):
    t, d = x2.shape
    n_in = w_in.shape[1]
    d_ssm = d // 2
    d_attn = N_Q_HEADS * HEAD_DIM
    d_kv = N_KV_HEADS * HEAD_DIM
    tiles_per_seq = seq // tm
    cos, sa, sb = tabs
    kern = functools.partial(_in_kernel, d_model=d, d_ssm=d_ssm, d_attn=d_attn, d_kv=d_kv)
    row = lambda i: (i, 0)
    const = lambda i: (0, 0)
    tab = lambda i: (i % tiles_per_seq, 0)
    return pl.pallas_call(
        kern,
        grid=(t // tm,),
        in_specs=[pl.BlockSpec((tm, d), row),
                  pl.BlockSpec((1, 1, mod3.shape[2]), lambda i: (i // tiles_per_seq, 0, 0)),
                  pl.BlockSpec((1, d), const),
                  pl.BlockSpec((d, n_in), const),
                  pl.BlockSpec((tm, LANES), tab),
                  pl.BlockSpec((tm, LANES), tab),
                  pl.BlockSpec((tm, LANES), tab),
                  pl.BlockSpec((1, d_attn), const),
                  pl.BlockSpec((1, d_kv), const),
                  pl.BlockSpec((d_attn, d_attn), const)],
        out_specs=[pl.BlockSpec((tm, d_ssm), row),
                   pl.BlockSpec((tm, d_attn), row),
                   pl.BlockSpec((tm, d_kv), row),
                   pl.BlockSpec((tm, d_kv), row),
                   pl.BlockSpec((tm, d), row),
                   pl.BlockSpec((tm, d), row)],
        out_shape=[jax.ShapeDtypeStruct((t, d_ssm), BF16),
                   jax.ShapeDtypeStruct((t, d_attn), BF16),
                   jax.ShapeDtypeStruct((t, d_kv), BF16),
                   jax.ShapeDtypeStruct((t, d_kv), BF16),
                   jax.ShapeDtypeStruct((t, d), BF16),
                   jax.ShapeDtypeStruct((t, d), BF16)],
        name="in_proj",
        compiler_params=_cparams(("arbitrary",), 48),
    )(x2, mod3, ln_w, w_in, cos, sa, sb, qw, kw, hm)


def _rope_tables(seq):
    pos = jnp.arange(seq)
    row = (pos // GRID_W).astype(F32)
    col = (pos % GRID_W).astype(F32)
    inv = 1.0 / (ROPE_THETA ** (jnp.arange(0, ROT_HALF, 2, dtype=F32) / ROT_HALF))
    ang_r = row[:, None] * inv[None, :]
    ang_c = col[:, None] * inv[None, :]
    zeros = jnp.zeros_like(ang_r)
    cos_h = jnp.concatenate([jnp.cos(ang_r), jnp.cos(ang_r), jnp.cos(ang_c), jnp.cos(ang_c)], axis=-1)
    sa_h = jnp.concatenate([-jnp.sin(ang_r), zeros, -jnp.sin(ang_c), zeros], axis=-1)
    sb_h = jnp.concatenate([zeros, jnp.sin(ang_r), zeros, jnp.sin(ang_c)], axis=-1)
    rep = LANES // HEAD_DIM
    return tuple(jnp.tile(a, (1, rep)) for a in (cos_h, sa_h, sb_h))


def _attn_kernel(q_ref, k_ref, v_ref, o_ref):
    outs = []
    for kvh in range(N_KV_HEADS):
        k = k_ref[:, kvh * HEAD_DIM:(kvh + 1) * HEAD_DIM]
        v = v_ref[:, kvh * HEAD_DIM:(kvh + 1) * HEAD_DIM]
        for g in range(Q_PER_KV):
            hd = kvh * Q_PER_KV + g
            q = q_ref[:, hd * HEAD_DIM:(hd + 1) * HEAD_DIM]
            s = lax.dot_general(q, k, (((1,), (1,)), ((), ())), preferred_element_type=F32)
            m = jnp.max(s, axis=-1, keepdims=True)
            p = jnp.exp2(s - m)
            l = jnp.sum(p, axis=-1, keepdims=True)
            o = jnp.dot(p.astype(BF16), v, preferred_element_type=F32)
            outs.append(o / l)
    o_ref[...] = jnp.concatenate(outs, axis=-1).astype(o_ref.dtype)


def _attn_call(q, k, v, *, seq, tq):
    t, d_attn = q.shape
    d_kv = k.shape[1]
    nq = seq // tq
    return pl.pallas_call(
        _attn_kernel,
        grid=(t // seq, nq),
        in_specs=[pl.BlockSpec((tq, d_attn), lambda b, i: (b * nq + i, 0)),
                  pl.BlockSpec((seq, d_kv), lambda b, i: (b, 0)),
                  pl.BlockSpec((seq, d_kv), lambda b, i: (b, 0))],
        out_specs=pl.BlockSpec((tq, d_attn), lambda b, i: (b * nq + i, 0)),
        out_shape=jax.ShapeDtypeStruct((t, d_attn), BF16),
        name="attention",
        compiler_params=_cparams(("arbitrary", "arbitrary"), 48),
    )(q, k, v)


def _s5_weights(lam_re, lam_im, log_dt, b_re, b_im, c_re, c_im, d_skip):
    hp = lax.Precision.HIGHEST
    ng, p = lam_re.shape[1], lam_re.shape[2]
    cg = SSM_GROUP
    L = S5_CHUNK
    dt = jnp.exp(log_dt)[..., None]
    zr = lam_re * dt
    zi = lam_im * dt
    d = jnp.arange(L + 1, dtype=F32)[:, None, None, None]
    pmag = jnp.exp(zr[None] * d)
    pw_re = pmag * jnp.cos(zi[None] * d)
    pw_im = pmag * jnp.sin(zi[None] * d)
    lb_re, lb_im = pw_re[1], pw_im[1]
    nr, ni = lb_re - 1.0, lb_im
    den = lam_re * lam_re + lam_im * lam_im
    f_re = ((nr * lam_re + ni * lam_im) / den)[..., None]
    f_im = ((ni * lam_re - nr * lam_im) / den)[..., None]
    bb_re = f_re * b_re - f_im * b_im
    bb_im = f_re * b_im + f_im * b_re
    ab_re = pw_re[:L, ..., None] * bb_re[None] - pw_im[:L, ..., None] * bb_im[None]
    ab_im = pw_re[:L, ..., None] * bb_im[None] + pw_im[:L, ..., None] * bb_re[None]
    kker = (jnp.einsum('xgcp,dxgpe->dxgce', c_re, ab_re, precision=hp)
            - jnp.einsum('xgcp,dxgpe->dxgce', c_im, ab_im, precision=hp))
    s_idx = jnp.arange(L)[:, None]
    j_idx = jnp.arange(L)[None, :]
    lag_f = j_idx - s_idx
    lag_b = s_idx - j_idx
    kf = jnp.where((lag_f >= 0)[:, :, None, None, None], kker[jnp.clip(lag_f, 0, L - 1), 0], 0.0)
    kb = jnp.where((lag_b >= 0)[:, :, None, None, None], kker[jnp.clip(lag_b, 0, L - 1), 1], 0.0)
    eye_c = jnp.eye(cg, dtype=F32)
    skip = (lag_f == 0)[:, :, None, None, None] * (d_skip.reshape(ng, cg)[None, None, :, :, None] * eye_c)
    tt = kf + kb + skip
    tt = tt.transpose(2, 0, 4, 1, 3)
    tmat = tt.reshape(ng, L * cg, L * cg)
    w1f_re = ab_re[::-1, 0].transpose(1, 0, 3, 2).reshape(ng, L * cg, p)
    w1f_im = ab_im[::-1, 0].transpose(1, 0, 3, 2).reshape(ng, L * cg, p)
    w1b_re = ab_re[:, 1].transpose(1, 0, 3, 2).reshape(ng, L * cg, p)
    w1b_im = ab_im[:, 1].transpose(1, 0, 3, 2).reshape(ng, L * cg, p)
    powf_re, powf_im = pw_re[1:, 0], pw_im[1:, 0]
    powb_re, powb_im = pw_re[1:, 1][::-1], pw_im[1:, 1][::-1]
    cr, ci = c_re, c_im

    def readout(pr, pi, crd, cid):
        re = crd[None] * pr[:, :, None, :] - cid[None] * pi[:, :, None, :]
        im = crd[None] * pi[:, :, None, :] + cid[None] * pr[:, :, None, :]
        to = lambda a: a.transpose(1, 3, 0, 2).reshape(ng, p, L * cg)
        return to(re), to(-im)

    w2f_re, w2f_im = readout(powf_re, powf_im, cr[0], ci[0])
    w2b_re, w2b_im = readout(powb_re, powb_im, cr[1], ci[1])

    nq = ng // S5_PAIR
    kd = L * cg

    def pair_diag(m):
        r, c = m.shape[1], m.shape[2]
        m = m.reshape(nq, S5_PAIR, r, c)
        eye = jnp.eye(S5_PAIR, dtype=m.dtype)
        return jnp.einsum('qgrc,gh->qgrhc', m, eye).reshape(nq, S5_PAIR * r, S5_PAIR * c)

    wa = jnp.concatenate([pair_diag(tmat), pair_diag(w1f_re), pair_diag(w1f_im),
                          pair_diag(w1b_re), pair_diag(w1b_im)], axis=-1)
    wb = jnp.concatenate([pair_diag(w2f_re), pair_diag(w2f_im),
                          pair_diag(w2b_re), pair_diag(w2b_im)], axis=1)
    a_re = pw_re[L].reshape(2, ng * p)
    a_im = pw_im[L].reshape(2, ng * p)
    return wa.astype(BF16), wb.astype(BF16), a_re, a_im


def _s5a_kernel(z_ref, wa_ref, yi_ref, sfr_ref, sfi_ref, sbr_ref, sbi_ref, *, kd2, sp):
    res = jnp.dot(z_ref[0], wa_ref[0], preferred_element_type=F32)
    yi_ref[0] = res[:, :kd2]
    sfr_ref[...] = res[:, kd2:kd2 + sp]
    sfi_ref[...] = res[:, kd2 + sp:kd2 + 2 * sp]
    sbr_ref[...] = res[:, kd2 + 2 * sp:kd2 + 3 * sp]
    sbi_ref[...] = res[:, kd2 + 3 * sp:kd2 + 4 * sp]


def _s5a_call(z, wa):
    nq, n, kd2 = z.shape
    sp = (wa.shape[2] - kd2) // 4
    st = jax.ShapeDtypeStruct((n, nq * sp), F32)
    sspec = pl.BlockSpec((n, sp), lambda q: (0, q))
    return pl.pallas_call(
        functools.partial(_s5a_kernel, kd2=kd2, sp=sp),
        grid=(nq,),
        in_specs=[pl.BlockSpec((1, n, kd2), lambda q: (q, 0, 0)),
                  pl.BlockSpec((1, kd2, wa.shape[2]), lambda q: (q, 0, 0))],
        out_specs=[pl.BlockSpec((1, n, kd2), lambda q: (q, 0, 0)), sspec, sspec, sspec, sspec],
        out_shape=[jax.ShapeDtypeStruct((nq, n, kd2), F32), st, st, st, st],
        name="s5_chunk",
        compiler_params=_cparams(("arbitrary",), 32),
    )(z, wa)


def _s5scan_kernel(sfr_ref, sfi_ref, sbr_ref, sbi_ref, afr_ref, afi_ref, abr_ref, abi_ref,
                   xfr_ref, xfi_ref, xbr_ref, xbi_ref, *, nchunk):
    afr, afi = afr_ref[...], afi_ref[...]
    abr, abi = abr_ref[...], abi_ref[...]
    zero = jnp.zeros_like(afr)

    def body(k, carry):
        fr, fi, br, bi = carry
        kb = nchunk - 1 - k
        rowf = pl.ds(k, 1)
        rowb = pl.ds(kb, 1)
        xfr_ref[rowf, :] = fr
        xfi_ref[rowf, :] = fi
        xbr_ref[rowb, :] = br
        xbi_ref[rowb, :] = bi
        nfr = fr * afr - fi * afi + sfr_ref[rowf, :]
        nfi = fr * afi + fi * afr + sfi_ref[rowf, :]
        nbr = br * abr - bi * abi + sbr_ref[rowb, :]
        nbi = br * abi + bi * abr + sbi_ref[rowb, :]
        return nfr, nfi, nbr, nbi

    lax.fori_loop(0, nchunk, body, (zero, zero, zero, zero))


def _s5scan_call(s4, a4, *, nchunk, tl):
    w = s4[0].shape[1]
    sspec = pl.BlockSpec((nchunk, tl), lambda j: (0, j))
    aspec = pl.BlockSpec((1, tl), lambda j: (0, j))
    st = jax.ShapeDtypeStruct((nchunk, w), F32)
    return pl.pallas_call(
        functools.partial(_s5scan_kernel, nchunk=nchunk),
        grid=(w // tl,),
        in_specs=[sspec] * 4 + [aspec] * 4,
        out_specs=[sspec] * 4,
        out_shape=[st] * 4,
        name="s5_scan",
        compiler_params=_cparams(("arbitrary",), 32),
    )(*s4, *a4)


def _s5c_kernel(yi_ref, xfr_ref, xfi_ref, xbr_ref, xbi_ref, wb_ref, y_ref):
    xs = jnp.concatenate([xfr_ref[...], xfi_ref[...], xbr_ref[...], xbi_ref[...]], axis=-1)
    y_ref[0] = yi_ref[0] + jnp.dot(xs.astype(BF16), wb_ref[0], preferred_element_type=F32)


def _s5c_call(yi, x4, wb):
    nq, n, kd2 = yi.shape
    sp = wb.shape[1] // 4
    xspec = pl.BlockSpec((n, sp), lambda q: (0, q))
    return pl.pallas_call(
        _s5c_kernel,
        grid=(nq,),
        in_specs=[pl.BlockSpec((1, n, kd2), lambda q: (q, 0, 0)), xspec, xspec, xspec, xspec,
                  pl.BlockSpec((1, wb.shape[1], kd2), lambda q: (q, 0, 0))],
        out_specs=pl.BlockSpec((1, n, kd2), lambda q: (q, 0, 0)),
        out_shape=jax.ShapeDtypeStruct((nq, n, kd2), F32),
        name="s5_readout",
        compiler_params=_cparams(("arbitrary",), 32),
    )(yi, *x4, wb)


def _s5_mixer(u, wa, wb, a_re, a_im, *, batch, seq):
    t, d_ssm = u.shape
    L, cg = S5_CHUNK, SSM_GROUP
    nchunk = seq // L
    ng = d_ssm // cg
    nq = ng // S5_PAIR
    n = nchunk * batch
    z = u.reshape(batch, nchunk, L, nq, S5_PAIR, cg).transpose(3, 1, 0, 4, 2, 5)
    z = z.reshape(nq, n, S5_PAIR * L * cg)
    yi, sfr, sfi, sbr, sbi = _s5a_call(z, wa)
    w = sfr.shape[1]
    s4 = [a.reshape(nchunk, batch * w) for a in (sfr, sfi, sbr, sbi)]
    a4 = [jnp.tile(a.reshape(1, w), (1, batch)) for a in (a_re[0], a_im[0], a_re[1], a_im[1])]
    tl = min(1024, batch * w)
    x4 = _s5scan_call(s4, a4, nchunk=nchunk, tl=tl)
    x4 = [a.reshape(n, w) for a in x4]
    y = _s5c_call(yi, x4, wb)
    y = y.reshape(nq, nchunk, batch, S5_PAIR, L, cg).transpose(2, 1, 4, 0, 3, 5)
    return y.reshape(t, d_ssm)


def _post_kernel(y_ref, at_ref, gs_ref, ga_ref, x_ref, mod_ref, ln_ref,
                 wg_ref, bg_ref, wsu_ref, wau_ref, wo_ref, wr_ref,
                 x1_ref, h2_ref, aff_ref, *, d_model):
    z = jax.nn.gelu(y_ref[...], approximate=True)
    glu = jnp.dot(z.astype(BF16), wg_ref[...], preferred_element_type=F32) + bg_ref[...]
    ssm = z * jax.nn.sigmoid(glu)
    su = jnp.dot(ssm.astype(BF16), wsu_ref[...], preferred_element_type=F32)
    au = jnp.dot(at_ref[...], wau_ref[...], preferred_element_type=F32)
    merged = gs_ref[...].astype(F32) * su + ga_ref[...].astype(F32) * au
    mix = jnp.dot(merged.astype(BF16), wo_ref[...], preferred_element_type=F32)
    mod = mod_ref[0]
    g1 = mod[:, 2 * d_model:3 * d_model]
    sh2 = mod[:, 3 * d_model:4 * d_model]
    sc2 = mod[:, 4 * d_model:5 * d_model]
    x1 = x_ref[...] + g1 * mix
    x1_ref[...] = x1
    ms = jnp.mean(x1 * x1, axis=-1, keepdims=True)
    h2 = (x1 * lax.rsqrt(ms + EPS) * ln_ref[...]) * (1.0 + sc2) + sh2
    h2_ref[...] = h2.astype(h2_ref.dtype)
    lg = lax.dot_general(wr_ref[...], h2, (((1,), (1,)), ((), ())), preferred_element_type=F32,
                         precision=lax.Precision.HIGHEST)
    lg = lg - jnp.max(lg, axis=0, keepdims=True)
    ex = jnp.exp(lg)
    aff_ref[0] = ex / jnp.sum(ex, axis=0, keepdims=True)


def _post_call(y, attn, gs, ga, x2, mod3, ln_w, wg, bg, wsu, wau, wo, wr_t, *, seq, tm):
    t, d = x2.shape
    d_ssm = y.shape[1]
    d_attn = attn.shape[1]
    ne = wr_t.shape[0]
    tiles_per_seq = seq // tm
    row = lambda i: (i, 0)
    const = lambda i: (0, 0)
    return pl.pallas_call(
        functools.partial(_post_kernel, d_model=d),
        grid=(t // tm,),
        in_specs=[pl.BlockSpec((tm, d_ssm), row),
                  pl.BlockSpec((tm, d_attn), row),
                  pl.BlockSpec((tm, d), row),
                  pl.BlockSpec((tm, d), row),
                  pl.BlockSpec((tm, d), row),
                  pl.BlockSpec((1, 1, mod3.shape[2]), lambda i: (i // tiles_per_seq, 0, 0)),
                  pl.BlockSpec((1, d), const),
                  pl.BlockSpec(wg.shape, const),
                  pl.BlockSpec((1, d_ssm), const),
                  pl.BlockSpec(wsu.shape, const),
                  pl.BlockSpec(wau.shape, const),
                  pl.BlockSpec(wo.shape, const),
                  pl.BlockSpec(wr_t.shape, const)],
        out_specs=[pl.BlockSpec((tm, d), row),
                   pl.BlockSpec((tm, d), row),
                   pl.BlockSpec((1, ne, tm), lambda i: (i // tiles_per_seq, 0, i % tiles_per_seq))],
        out_shape=[jax.ShapeDtypeStruct((t, d), F32),
                   jax.ShapeDtypeStruct((t, d), BF16),
                   jax.ShapeDtypeStruct((t // seq, ne, seq), F32)],
        name="post_mix",
        compiler_params=_cparams(("arbitrary",), 48),
    )(y, attn, gs, ga, x2, mod3, ln_w, wg, bg, wsu, wau, wo, wr_t)


def _lane_cumsum(mask_f, tri):
    ne, seq = mask_f.shape
    run = jnp.zeros((ne, 1), F32)
    parts = []
    for j in range(seq // LANES):
        blk = mask_f[:, j * LANES:(j + 1) * LANES]
        cs = jnp.dot(blk.astype(BF16), tri, preferred_element_type=F32) + run
        parts.append(cs)
        run = run + jnp.sum(blk, axis=1, keepdims=True)
    return jnp.concatenate(parts, axis=-1)


def _select_kernel(aff_ref, tri_ref, rank_ref, gate_ref, *, cap):
    aff = aff_ref[0]
    ne = aff.shape[0]
    tri = tri_ref[...]

    def body(i, thr):
        cand = thr | (jnp.int32(1) << (30 - i))
        cnt = jnp.sum(jnp.where(aff >= pltpu.bitcast(cand, F32), 1.0, 0.0), axis=1, keepdims=True)
        return jnp.where(cnt >= cap, cand, thr)

    thr = lax.fori_loop(0, 31, body, jnp.zeros((ne, 1), jnp.int32))
    gt = aff >= pltpu.bitcast(thr + 1, F32)
    eq = (aff >= pltpu.bitcast(thr, F32)) & jnp.logical_not(gt)
    need = cap - jnp.sum(jnp.where(gt, 1.0, 0.0), axis=1, keepdims=True)
    eq_f = jnp.where(eq, 1.0, 0.0)
    eq_rank = _lane_cumsum(eq_f, tri) - eq_f
    sel = gt | (eq & (eq_rank < need))
    sel_f = jnp.where(sel, 1.0, 0.0)
    rank = _lane_cumsum(sel_f, tri) - sel_f
    rank_ref[0] = jnp.where(sel, rank.astype(jnp.int32), -1)
    gate_ref[0] = jnp.where(sel, aff, 0.0)


def _select_call(aff_t, tri, *, cap):
    b, ne, seq = aff_t.shape
    spec = pl.BlockSpec((1, ne, seq), lambda i: (i, 0, 0))
    return pl.pallas_call(
        functools.partial(_select_kernel, cap=cap),
        grid=(b,),
        in_specs=[spec, pl.BlockSpec((LANES, LANES), lambda i: (0, 0))],
        out_specs=[spec, spec],
        out_shape=[jax.ShapeDtypeStruct((b, ne, seq), jnp.int32),
                   jax.ShapeDtypeStruct((b, ne, seq), F32)],
        name="expert_select",
        compiler_params=_cparams(("arbitrary",), 32),
    )(aff_t, tri)


def _moe_kernel(h_ref, rank_ref, gate_ref, wg_ref, wu_ref, wd_ref, ye_ref, xg_ref, gc_ref, *, cap, ts):
    f = pl.program_id(2)
    seq = h_ref.shape[0]

    @pl.when(f == 0)
    def _gather():
        slot = lax.broadcasted_iota(jnp.int32, (cap, ts), 0)
        acc = jnp.zeros(xg_ref.shape, F32)
        gc = jnp.zeros((cap, 1), F32)
        for j in range(seq // ts):
            hit = rank_ref[0, :, j * ts:(j + 1) * ts] == slot
            onehot = jnp.where(hit, 1.0, 0.0)
            acc = acc + jnp.dot(onehot.astype(BF16), h_ref[j * ts:(j + 1) * ts, :],
                                preferred_element_type=F32)
            gc = gc + jnp.sum(onehot * gate_ref[0, :, j * ts:(j + 1) * ts], axis=1, keepdims=True)
        xg_ref[...] = acc.astype(xg_ref.dtype)
        gc_ref[...] = gc
        ye_ref[...] = jnp.zeros_like(ye_ref)

    xg = xg_ref[...]
    hg = jnp.dot(xg, wg_ref[0], preferred_element_type=F32)
    hu = jnp.dot(xg, wu_ref[0], preferred_element_type=F32)
    hid = (hg * jax.nn.sigmoid(hg) * hu).astype(BF16)
    ye_ref[0, 0] = ye_ref[0, 0] + jnp.dot(hid, wd_ref[0], preferred_element_type=F32) * gc_ref[...]


def _moe_call(h2, rank3, gate3, wg, wu, wd, *, batch, seq, cap, nf):
    t, d = h2.shape
    ne, _, dff = wg.shape
    tf = dff // nf
    ts = min(1024, seq)
    return pl.pallas_call(
        functools.partial(_moe_kernel, cap=cap, ts=ts),
        grid=(ne, batch, nf),
        in_specs=[pl.BlockSpec((seq, d), lambda e, b, f: (b, 0)),
                  pl.BlockSpec((1, 1, seq), lambda e, b, f: (b * ne + e, 0, 0)),
                  pl.BlockSpec((1, 1, seq), lambda e, b, f: (b * ne + e, 0, 0)),
                  pl.BlockSpec((1, d, tf), lambda e, b, f: (e, 0, f)),
                  pl.BlockSpec((1, d, tf), lambda e, b, f: (e, 0, f)),
                  pl.BlockSpec((1, tf, d), lambda e, b, f: (e, f, 0))],
        out_specs=pl.BlockSpec((1, 1, cap, d), lambda e, b, f: (b, e, 0, 0)),
        out_shape=jax.ShapeDtypeStruct((batch, ne, cap, d), F32),
        scratch_shapes=[pltpu.VMEM((cap, d), BF16), pltpu.VMEM((cap, 1), F32)],
        name="expert_ffn",
        compiler_params=_cparams(("arbitrary", "arbitrary", "arbitrary"), 56),
    )(h2, rank3, gate3, wg, wu, wd)


def _combine_kernel(ye_ref, rank_ref, x1_ref, mod_ref, o_ref, *, cap, d_model):
    e = pl.program_id(1)

    @pl.when(e == 0)
    def _init():
        o_ref[...] = jnp.zeros_like(o_ref)

    ts = o_ref.shape[0]
    slot = lax.broadcasted_iota(jnp.int32, (cap, ts), 0)
    onehot = jnp.where(rank_ref[0] == slot, 1.0, 0.0).astype(BF16)
    o_ref[...] += lax.dot_general(onehot, ye_ref[0, 0].astype(BF16), (((0,), (0,)), ((), ())),
                                  preferred_element_type=F32)

    @pl.when(e == pl.num_programs(1) - 1)
    def _fin():
        g2 = mod_ref[0][:, 5 * d_model:6 * d_model]
        o_ref[...] = x1_ref[...] + g2 * o_ref[...]


def _combine_call(ye, rank3, x1, mod3, *, batch, seq, cap, ts):
    t, d = x1.shape
    ne = ye.shape[1]
    nt = seq // ts
    return pl.pallas_call(
        functools.partial(_combine_kernel, cap=cap, d_model=d),
        grid=(t // ts, ne),
        in_specs=[pl.BlockSpec((1, 1, cap, d), lambda i, e: (i // nt, e, 0, 0)),
                  pl.BlockSpec((1, 1, ts), lambda i, e: ((i // nt) * ne + e, 0, i % nt)),
                  pl.BlockSpec((ts, d), lambda i, e: (i, 0)),
                  pl.BlockSpec((1, 1, mod3.shape[2]), lambda i, e: (i // nt, 0, 0))],
        out_specs=pl.BlockSpec((ts, d), lambda i, e: (i, 0)),
        out_shape=jax.ShapeDtypeStruct((t, d), F32),
        name="moe_combine",
        compiler_params=_cparams(("arbitrary", "arbitrary"), 48),
    )(ye, rank3, x1, mod3)


def kernel(x, c, w_mod, b_mod, ln1_w, ln2_w, w_in, ssm_lam_re, ssm_lam_im, ssm_log_dt, ssm_b_re,
           ssm_b_im, ssm_c_re, ssm_c_im, ssm_d, w_glu, b_glu, q_norm_w, k_norm_w, w_ssm_up,
           w_attn_up, w_out, w_router, w_exp_gate, w_exp_up, w_exp_down):
    batch, seq, d = x.shape
    depth = w_mod.shape[0]
    t = batch * seq
    d_attn = N_Q_HEADS * HEAD_DIM
    d_kv = N_KV_HEADS * HEAD_DIM
    cap = CAPACITY_FACTOR * seq // N_EXPERTS
    tm = min(512, seq)
    tq = min(128, seq)

    tabs = _rope_tables(seq)
    head_id = jnp.arange(d_attn) // HEAD_DIM
    hm = (head_id[:, None] == head_id[None, :]).astype(BF16)
    tri = (jnp.arange(LANES)[:, None] <= jnp.arange(LANES)[None, :]).astype(BF16)
    c_pad = jnp.zeros((8, d), F32).at[:batch].set(c)
    mod_all = _mod_call(c_pad, w_mod, b_mod)[:, :batch]

    x2 = x.reshape(t, d)
    for l in range(depth):
        mod3 = mod_all[l].reshape(batch, 1, 6 * d)
        qw = jnp.tile(q_norm_w[l], N_Q_HEADS).reshape(1, d_attn)
        kw = jnp.tile(k_norm_w[l], N_KV_HEADS).reshape(1, d_kv)
        u, q, k, v, gs, ga = _in_call(x2, mod3, ln1_w[l].reshape(1, d), w_in[l].astype(BF16), tabs,
                                      qw, kw, hm, seq=seq, tm=tm)
        wa, wb, a_re, a_im = _s5_weights(ssm_lam_re[l], ssm_lam_im[l], ssm_log_dt[l], ssm_b_re[l],
                                         ssm_b_im[l], ssm_c_re[l], ssm_c_im[l], ssm_d[l])
        y = _s5_mixer(u, wa, wb, a_re, a_im, batch=batch, seq=seq)
        attn = _attn_call(q, k, v, seq=seq, tq=tq)
        x1, h2, aff_t = _post_call(
            y, attn, gs, ga, x2, mod3, ln2_w[l].reshape(1, d), w_glu[l].astype(BF16),
            b_glu[l].reshape(1, -1), w_ssm_up[l].astype(BF16), w_attn_up[l].astype(BF16),
            w_out[l].astype(BF16), w_router[l].T, seq=seq, tm=tm)
        rank, gate = _select_call(aff_t, tri, cap=cap)
        rank3 = rank.reshape(batch * N_EXPERTS, 1, seq)
        gate3 = gate.reshape(batch * N_EXPERTS, 1, seq)
        ye = _moe_call(h2, rank3, gate3, w_exp_gate[l].astype(BF16), w_exp_up[l].astype(BF16),
                       w_exp_down[l].astype(BF16), batch=batch, seq=seq, cap=cap, nf=2)
        x2 = _combine_call(ye, rank3, x1, mod3, batch=batch, seq=seq, cap=cap, ts=min(1024, seq))
    return x2.reshape(batch, seq, d)
```

```python
import functools
import math

import jax
import jax.numpy as jnp
from jax import lax
from jax.experimental import pallas as pl
from jax.experimental.pallas import tpu as pltpu

F32 = jnp.float32
BF16 = jnp.bfloat16

HEAD_DIM = 64
N_Q_HEADS = 8
N_KV_HEADS = 2
Q_PER_KV = N_Q_HEADS // N_KV_HEADS
ROT_HALF = HEAD_DIM // 2
ROPE_THETA = 10000.0
GRID_W = 64
SSM_GROUP = 16
SSM_STATE = 64
N_EXPERTS = 16
CAPACITY_FACTOR = 2
EPS = 1e-6

S5_CHUNK = 16
S5_PAIR = 2
LANES = 128
V7X_VMEM_BYTES = 64 * 1024 * 1024
LOG2E = 1.4426950408889634


def _cparams(semantics, vmem_mb):
    return pltpu.CompilerParams(dimension_semantics=semantics,
                                vmem_limit_bytes=int(vmem_mb * 1024 * 1024))


def _mod_kernel(c_ref, w_ref, b_ref, o_ref):
    c = c_ref[...]
    ca = c * jax.nn.sigmoid(c)
    o_ref[0] = jnp.dot(ca, w_ref[0], preferred_element_type=F32,
                       precision=lax.Precision.HIGHEST) + b_ref[0]


def _mod_call(c_pad, w_mod, b_mod):
    depth, d, n = w_mod.shape
    rows = c_pad.shape[0]
    tn = 1536
    return pl.pallas_call(
        _mod_kernel,
        grid=(depth, n // tn),
        in_specs=[pl.BlockSpec((rows, d), lambda l, j: (0, 0)),
                  pl.BlockSpec((1, d, tn), lambda l, j: (l, 0, j)),
                  pl.BlockSpec((1, 1, tn), lambda l, j: (l, 0, j))],
        out_specs=pl.BlockSpec((1, rows, tn), lambda l, j: (l, 0, j)),
        out_shape=jax.ShapeDtypeStruct((depth, rows, n), F32),
        name="adaln_mod",
        compiler_params=_cparams(("arbitrary", "arbitrary"), 32),
    )(c_pad, w_mod, b_mod.reshape(depth, 1, n))


def _head_mean_square(xsq, hm):
    hi = xsq.astype(BF16)
    lo = (xsq - hi.astype(F32)).astype(BF16)
    s = jnp.dot(hi, hm, preferred_element_type=F32) + jnp.dot(lo, hm, preferred_element_type=F32)
    return s * (1.0 / HEAD_DIM)


def _rope_block(blk, cos, sa, sb):
    return (blk * cos + pltpu.roll(blk, LANES - ROT_HALF // 2, 1) * sa
            + pltpu.roll(blk, ROT_HALF // 2, 1) * sb)


def _in_kernel(x_ref, mod_ref, ln_ref, w_ref, cos_ref, sa_ref, sb_ref, qw_ref, kw_ref, hm_ref,
               u_ref, q_ref, k_ref, v_ref, gs_ref, ga_ref, *, d_model, d_ssm, d_attn, d_kv):
    x = x_ref[...]
    mod = mod_ref[0]
    sh = mod[:, 0:d_model]
    sc = mod[:, d_model:2 * d_model]
    ms = jnp.mean(x * x, axis=-1, keepdims=True)
    h = (x * lax.rsqrt(ms + EPS) * ln_ref[...]) * (1.0 + sc) + sh
    proj = jnp.dot(h.astype(BF16), w_ref[...], preferred_element_type=F32)
    o = 0
    u_ref[...] = proj[:, o:o + d_ssm].astype(u_ref.dtype)
    o += d_ssm
    q = proj[:, o:o + d_attn]
    o += d_attn
    k = proj[:, o:o + d_kv]
    o += d_kv
    v_ref[0] = proj[:, o:o + d_kv].T.astype(v_ref.dtype)
    o += d_kv
    gs_ref[...] = jax.nn.sigmoid(proj[:, o:o + d_model]).astype(gs_ref.dtype)
    o += d_model
    ga_ref[...] = jax.nn.sigmoid(proj[:, o:o + d_model]).astype(ga_ref.dtype)

    hm = hm_ref[...]
    cos = cos_ref[...]
    sa = sa_ref[...]
    sb = sb_ref[...]
    qn = q * lax.rsqrt(_head_mean_square(q * q, hm) + EPS) * qw_ref[...]
    kn = k * lax.rsqrt(_head_mean_square(k * k, hm[:d_kv, :d_kv]) + EPS) * kw_ref[...]
    qscale = (HEAD_DIM ** -0.5) * LOG2E
    qr = [_rope_block(qn[:, j * LANES:(j + 1) * LANES], cos, sa, sb) for j in range(d_attn // LANES)]
    q_ref[...] = (jnp.concatenate(qr, axis=-1) * qscale).astype(q_ref.dtype)
    kr = [_rope_block(kn[:, j * LANES:(j + 1) * LANES], cos, sa, sb) for j in range(d_kv // LANES)]
    k_ref[...] = jnp.concatenate(kr, axis=-1).astype(k_ref.dtype)


def _in_call(x2, mod3, ln_w, w_in, tabs, qw, kw, hm, *, seq, tm):
    t, d = x2.shape
    n_in = w_in.shape[1]
    d_ssm = d // 2
    d_attn = N_Q_HEADS * HEAD_DIM
    d_kv = N_KV_HEADS * HEAD_DIM
    tiles_per_seq = seq // tm
    cos, sa, sb = tabs
    kern = functools.partial(_in_kernel, d_model=d, d_ssm=d_ssm, d_attn=d_attn, d_kv=d_kv)
    row = lambda i: (i, 0)
    const = lambda i: (0, 0)
    tab = lambda i: (i % tiles_per_seq, 0)
    return pl.pallas_call(
        kern,
        grid=(t // tm,),
        in_specs=[pl.BlockSpec((tm, d), row),
                  pl.BlockSpec((1, 1, mod3.shape[2]), lambda i: (i // tiles_per_seq, 0, 0)),
                  pl.BlockSpec((1, d), const),
                  pl.BlockSpec((d, n_in), const),
                  pl.BlockSpec((tm, LANES), tab),
                  pl.BlockSpec((tm, LANES), tab),
                  pl.BlockSpec((tm, LANES), tab),
                  pl.BlockSpec((1, d_attn), const),
                  pl.BlockSpec((1, d_kv), const),
                  pl.BlockSpec((d_attn, d_attn), const)],
        out_specs=[pl.BlockSpec((tm, d_ssm), row),
                   pl.BlockSpec((tm, d_attn), row),
                   pl.BlockSpec((tm, d_kv), row),
                   pl.BlockSpec((1, d_kv, tm), lambda i: (i // tiles_per_seq, 0, i % tiles_per_seq)),
                   pl.BlockSpec((tm, d), row),
                   pl.BlockSpec((tm, d), row)],
        out_shape=[jax.ShapeDtypeStruct((t, d_ssm), F32),
                   jax.ShapeDtypeStruct((t, d_attn), BF16),
                   jax.ShapeDtypeStruct((t, d_kv), BF16),
                   jax.ShapeDtypeStruct((t // seq, d_kv, seq), BF16),
                   jax.ShapeDtypeStruct((t, d), BF16),
                   jax.ShapeDtypeStruct((t, d), BF16)],
        name="in_proj",
        compiler_params=_cparams(("arbitrary",), 48),
    )(x2, mod3, ln_w, w_in, cos, sa, sb, qw, kw, hm)


def _rope_tables(seq):
    pos = jnp.arange(seq)
    row = (pos // GRID_W).astype(F32)
    col = (pos % GRID_W).astype(F32)
    inv = 1.0 / (ROPE_THETA ** (jnp.arange(0, ROT_HALF, 2, dtype=F32) / ROT_HALF))
    ang_r = row[:, None] * inv[None, :]
    ang_c = col[:, None] * inv[None, :]
    zeros = jnp.zeros_like(ang_r)
    cos_h = jnp.concatenate([jnp.cos(ang_r), jnp.cos(ang_r), jnp.cos(ang_c), jnp.cos(ang_c)], axis=-1)
    sa_h = jnp.concatenate([-jnp.sin(ang_r), zeros, -jnp.sin(ang_c), zeros], axis=-1)
    sb_h = jnp.concatenate([zeros, jnp.sin(ang_r), zeros, jnp.sin(ang_c)], axis=-1)
    rep = LANES // HEAD_DIM
    return tuple(jnp.tile(a, (1, rep)) for a in (cos_h, sa_h, sb_h))


def _attn_heads_out(o_t, tq):
    return [o_t[:, g * tq:(g + 1) * tq].T for g in range(Q_PER_KV)]


def _attn_kernel(q_ref, k_ref, vt_ref, o_ref, *, kb):
    tq = q_ref.shape[0]
    seq = k_ref.shape[0]
    nblk = seq // kb
    width = Q_PER_KV * HEAD_DIM
    q_ts = []
    for kvh in range(N_KV_HEADS):
        qt = q_ref[:, kvh * width:(kvh + 1) * width].astype(F32).T.astype(BF16)
        q_ts.append(jnp.concatenate([qt[g * HEAD_DIM:(g + 1) * HEAD_DIM] for g in range(Q_PER_KV)],
                                    axis=1))

    def scores(kvh, j):
        lo = kvh * HEAD_DIM
        return jnp.dot(k_ref[j * kb:(j + 1) * kb, lo:lo + HEAD_DIM], q_ts[kvh],
                       preferred_element_type=F32)

    def weighted_v(kvh, j, p):
        lo = kvh * HEAD_DIM
        return jnp.dot(vt_ref[0, lo:lo + HEAD_DIM, j * kb:(j + 1) * kb], p.astype(BF16),
                       preferred_element_type=F32)

    outs = []
    worst = jnp.zeros((1, 1), F32)
    for kvh in range(N_KV_HEADS):
        s0 = scores(kvh, 0)
        m0 = jnp.max(s0, axis=0, keepdims=True)
        l = jnp.zeros_like(m0)
        acc = jnp.zeros((HEAD_DIM, m0.shape[1]), F32)
        s_cur = s0
        for j in range(nblk):
            s_next = scores(kvh, j + 1) if j + 1 < nblk else None
            p = jnp.exp2(s_cur - m0)
            l = l + jnp.sum(p, axis=0, keepdims=True)
            acc = acc + weighted_v(kvh, j, p)
            s_cur = s_next
        worst = jnp.maximum(worst, jnp.max(l, axis=1, keepdims=True))
        outs.extend(_attn_heads_out(acc / l, tq))
    o_ref[...] = jnp.concatenate(outs, axis=-1).astype(o_ref.dtype)

    @pl.when(worst[0, 0] > jnp.finfo(F32).max)
    def _exact():
        outs = []
        for kvh in range(N_KV_HEADS):
            m = jnp.full((1, Q_PER_KV * tq), -jnp.inf, F32)
            l = jnp.zeros_like(m)
            acc = jnp.zeros((HEAD_DIM, Q_PER_KV * tq), F32)
            for j in range(nblk):
                s = scores(kvh, j)
                m_new = jnp.maximum(m, jnp.max(s, axis=0, keepdims=True))
                alpha = jnp.exp2(m - m_new)
                p = jnp.exp2(s - m_new)
                l = l * alpha + jnp.sum(p, axis=0, keepdims=True)
                acc = acc * alpha + weighted_v(kvh, j, p)
                m = m_new
            outs.extend(_attn_heads_out(acc / l, tq))
        o_ref[...] = jnp.concatenate(outs, axis=-1).astype(o_ref.dtype)


def _attn_call(q, k, vt, *, seq, tq):
    t, d_attn = q.shape
    d_kv = k.shape[1]
    nq = seq // tq
    return pl.pallas_call(
        functools.partial(_attn_kernel, kb=min(256, seq)),
        grid=(t // seq, nq),
        in_specs=[pl.BlockSpec((tq, d_attn), lambda b, i: (b * nq + i, 0)),
                  pl.BlockSpec((seq, d_kv), lambda b, i: (b, 0)),
                  pl.BlockSpec((1, d_kv, seq), lambda b, i: (b, 0, 0))],
        out_specs=pl.BlockSpec((tq, d_attn), lambda b, i: (b * nq + i, 0)),
        out_shape=jax.ShapeDtypeStruct((t, d_attn), BF16),
        name="attention",
        compiler_params=_cparams(("arbitrary", "arbitrary"), 48),
    )(q, k, vt)


def _s5_weights(lam_re, lam_im, log_dt, b_re, b_im, c_re, c_im, d_skip):
    hp = lax.Precision.HIGHEST
    nl, _, ng, p = lam_re.shape
    cg = SSM_GROUP
    L = S5_CHUNK
    kd = L * cg
    dt = jnp.exp(log_dt)[..., None]
    zr = lam_re * dt
    zi = lam_im * dt

    def cpow(e):
        mag = jnp.exp(zr[..., None] * e)
        return mag * jnp.cos(zi[..., None] * e), mag * jnp.sin(zi[..., None] * e)

    lb_re, lb_im = jnp.exp(zr) * jnp.cos(zi), jnp.exp(zr) * jnp.sin(zi)
    nr, ni = lb_re - 1.0, lb_im
    den = lam_re * lam_re + lam_im * lam_im
    f_re = ((nr * lam_re + ni * lam_im) / den)[..., None]
    f_im = ((ni * lam_re - nr * lam_im) / den)[..., None]
    bb_re = f_re * b_re - f_im * b_im
    bb_im = f_re * b_im + f_im * b_re

    m_idx = jnp.arange(L + 1, dtype=F32)
    expo = jnp.stack([m_idx, L - m_idx])[None, :, None, None, :]
    pw_re, pw_im = cpow(expo)
    rep = jnp.repeat(jnp.eye(L + 1, dtype=F32), cg, axis=1)
    til = jnp.tile(jnp.eye(cg, dtype=F32), (1, L + 1))
    pwx_re = jnp.einsum('dxgpm,mn->dxgpn', pw_re, rep, precision=hp)
    pwx_im = jnp.einsum('dxgpm,mn->dxgpn', pw_im, rep, precision=hp)
    ct_re = jnp.einsum('dxgcp,cn->dxgpn', c_re, til, precision=hp)
    ct_im = jnp.einsum('dxgcp,cn->dxgpn', c_im, til, precision=hp)
    ca_re = ct_re * pwx_re - ct_im * pwx_im
    ca_im = ct_re * pwx_im + ct_im * pwx_re
    ef_re, ef_im = ca_re[:, 0, ..., :kd], ca_im[:, 0, ..., :kd]
    w2f_re, w2f_im = ca_re[:, 0, ..., cg:], ca_im[:, 0, ..., cg:]
    w2b_re, w2b_im = ca_re[:, 1, ..., :kd], ca_im[:, 1, ..., :kd]
    eb_re, eb_im = ca_re[:, 1, ..., cg:], ca_im[:, 1, ..., cg:]

    def lag_rows(x, e_re, e_im):
        return (jnp.einsum('dgpe,dgpn->dgen', bb_re[:, x], e_re, precision=hp)
                - jnp.einsum('dgpe,dgpn->dgen', bb_im[:, x], e_im, precision=hp))

    kf = lag_rows(0, ef_re, ef_im)
    kb = lag_rows(1, eb_re, eb_im)
    zeros = jnp.zeros_like(kf)
    pf = jnp.concatenate([zeros, kf], axis=-1)
    pb = jnp.concatenate([kb, zeros], axis=-1)
    tf = jnp.stack([pf[..., kd - cg * s:2 * kd - cg * s] for s in range(L)], axis=2)
    tb = jnp.stack([pb[..., cg * (L - 1 - s):cg * (L - 1 - s) + kd] for s in range(L)], axis=2)
    skip = jnp.eye(kd, dtype=F32) * jnp.tile(d_skip.reshape(nl, ng, 1, cg), (1, 1, 1, L))
    tmat = (tf + tb).reshape(nl, ng, kd, kd) + skip

    s_idx = jnp.arange(L, dtype=F32)
    sexp = jnp.stack([L - 1 - s_idx, s_idx])[None, :, None, None, :]
    ps_re, ps_im = cpow(sexp)
    w1_re = (jnp.einsum('dxgps,dxgpe->dxgsep', ps_re, bb_re)
             - jnp.einsum('dxgps,dxgpe->dxgsep', ps_im, bb_im)).reshape(nl, 2, ng, kd, p)
    w1_im = (jnp.einsum('dxgps,dxgpe->dxgsep', ps_re, bb_im)
             + jnp.einsum('dxgps,dxgpe->dxgsep', ps_im, bb_re)).reshape(nl, 2, ng, kd, p)

    nq = ng // S5_PAIR
    eye2 = jnp.eye(S5_PAIR, dtype=F32)

    def pair_diag(mat):
        r, c = mat.shape[2], mat.shape[3]
        mat = mat.reshape(nl, nq, S5_PAIR, r, c)
        return jnp.einsum('dqgrc,gh->dqgrhc', mat, eye2).reshape(nl, nq, S5_PAIR * r, S5_PAIR * c)

    wa = jnp.concatenate([pair_diag(tmat), pair_diag(w1_re[:, 0]), pair_diag(w1_im[:, 0]),
                          pair_diag(w1_re[:, 1]), pair_diag(w1_im[:, 1])], axis=-1)
    wb = jnp.concatenate([pair_diag(w2f_re), pair_diag(-w2f_im),
                          pair_diag(w2b_re), pair_diag(-w2b_im)], axis=2)
    al_re, al_im = cpow(jnp.float32(L))
    a_re = al_re.reshape(nl, 2, ng * p)
    a_im = al_im.reshape(nl, 2, ng * p)
    return wa.astype(BF16), wb.astype(BF16), a_re, a_im


GROUPS_PER_VREG = LANES // SSM_GROUP
PAIRS_PER_VREG = GROUPS_PER_VREG // S5_PAIR
TOKENS_PER_HALF = LANES // SSM_GROUP


def _granule_transpose(arrs):
    gran = lax.broadcasted_iota(jnp.int32, arrs[0].shape, 1) // SSM_GROUP
    cur = list(arrs)
    for dist in (4, 2, 1):
        keep = (gran & dist) == 0
        nxt = list(cur)
        for i in range(len(cur)):
            if i & dist:
                continue
            a, b = cur[i], cur[i + dist]
            nxt[i] = jnp.where(keep, a, pltpu.roll(b, dist * SSM_GROUP, 1))
            nxt[i + dist] = jnp.where(keep, pltpu.roll(a, LANES - dist * SSM_GROUP, 1), b)
        cur = nxt
    return cur


def _s5_chunk_kernel(u_ref, wa_ref, yi_ref, sfr_ref, sfi_ref, sbr_ref, sbi_ref, *, nchunk):
    halves = []
    for jh in range(S5_CHUNK // TOKENS_PER_HALF):
        toks = [u_ref[pl.ds(jh * TOKENS_PER_HALF + jj, nchunk, stride=S5_CHUNK), :]
                for jj in range(TOKENS_PER_HALF)]
        halves.append([o.astype(BF16) for o in _granule_transpose(toks)])
    kd2 = S5_PAIR * S5_CHUNK * SSM_GROUP
    for qq in range(PAIRS_PER_VREG):
        z = jnp.concatenate([halves[jh][S5_PAIR * qq + gl] for gl in range(S5_PAIR)
                             for jh in range(len(halves))], axis=-1)
        res = jnp.dot(z, wa_ref[qq], preferred_element_type=F32)
        lo = qq * LANES
        yi_ref[0, qq] = res[:, :kd2]
        sfr_ref[:, lo:lo + LANES] = res[:, kd2:kd2 + LANES]
        sfi_ref[:, lo:lo + LANES] = res[:, kd2 + LANES:kd2 + 2 * LANES]
        sbr_ref[:, lo:lo + LANES] = res[:, kd2 + 2 * LANES:kd2 + 3 * LANES]
        sbi_ref[:, lo:lo + LANES] = res[:, kd2 + 3 * LANES:kd2 + 4 * LANES]


def _s5_chunk_call(u, wa, *, batch, seq):
    t, d_ssm = u.shape
    nchunk = seq // S5_CHUNK
    nv = d_ssm // LANES
    nq, kd2, ncol = wa.shape
    sw = PAIRS_PER_VREG * LANES
    st = jax.ShapeDtypeStruct((nchunk, batch * nv * sw), F32)
    sspec = pl.BlockSpec((nchunk, sw), lambda b, v: (0, b * nv + v))
    return pl.pallas_call(
        functools.partial(_s5_chunk_kernel, nchunk=nchunk),
        grid=(batch, nv),
        in_specs=[pl.BlockSpec((seq, LANES), lambda b, v: (b, v)),
                  pl.BlockSpec((PAIRS_PER_VREG, kd2, ncol), lambda b, v: (v, 0, 0))],
        out_specs=[pl.BlockSpec((1, PAIRS_PER_VREG, nchunk, kd2), lambda b, v: (b, v, 0, 0)),
                   sspec, sspec, sspec, sspec],
        out_shape=[jax.ShapeDtypeStruct((batch, nq, nchunk, kd2), F32), st, st, st, st],
        name="s5_chunk",
        compiler_params=_cparams(("arbitrary", "arbitrary"), 40),
    )(u, wa)


def _s5scan_kernel(sfr_ref, sfi_ref, sbr_ref, sbi_ref, afr_ref, afi_ref, abr_ref, abi_ref,
                   xfr_ref, xfi_ref, xbr_ref, xbi_ref, *, nchunk):
    afr, afi = afr_ref[...], afi_ref[...]
    abr, abi = abr_ref[...], abi_ref[...]
    zero = jnp.zeros_like(afr)

    def body(k, carry):
        fr, fi, br, bi = carry
        kb = nchunk - 1 - k
        rowf = pl.ds(k, 1)
        rowb = pl.ds(kb, 1)
        xfr_ref[rowf, :] = fr
        xfi_ref[rowf, :] = fi
        xbr_ref[rowb, :] = br
        xbi_ref[rowb, :] = bi
        nfr = fr * afr - fi * afi + sfr_ref[rowf, :]
        nfi = fr * afi + fi * afr + sfi_ref[rowf, :]
        nbr = br * abr - bi * abi + sbr_ref[rowb, :]
        nbi = br * abi + bi * abr + sbi_ref[rowb, :]
        return nfr, nfi, nbr, nbi

    lax.fori_loop(0, nchunk, body, (zero, zero, zero, zero))


def _s5scan_call(s4, a4, *, nchunk, tl):
    w = s4[0].shape[1]
    sspec = pl.BlockSpec((nchunk, tl), lambda j: (0, j))
    aspec = pl.BlockSpec((1, tl), lambda j: (0, j))
    st = jax.ShapeDtypeStruct((nchunk, w), F32)
    return pl.pallas_call(
        functools.partial(_s5scan_kernel, nchunk=nchunk),
        grid=(w // tl,),
        in_specs=[sspec] * 4 + [aspec] * 4,
        out_specs=[sspec] * 4,
        out_shape=[st] * 4,
        name="s5_scan",
        compiler_params=_cparams(("arbitrary",), 32),
    )(*s4, *a4)


def _s5_readout_kernel(yi_ref, xfr_ref, xfi_ref, xbr_ref, xbi_ref, wb_ref, y_ref, *, nchunk):
    nhalf = S5_CHUNK // TOKENS_PER_HALF
    pieces = [[None] * GROUPS_PER_VREG for _ in range(nhalf)]
    for qq in range(PAIRS_PER_VREG):
        lo = qq * LANES
        xs = jnp.concatenate([xfr_ref[:, lo:lo + LANES], xfi_ref[:, lo:lo + LANES],
                              xbr_ref[:, lo:lo + LANES], xbi_ref[:, lo:lo + LANES]], axis=-1)
        y = yi_ref[0, qq] + jnp.dot(xs.astype(BF16), wb_ref[qq], preferred_element_type=F32)
        for gl in range(S5_PAIR):
            for jh in range(nhalf):
                c0 = (gl * nhalf + jh) * LANES
                pieces[jh][S5_PAIR * qq + gl] = y[:, c0:c0 + LANES]
    for jh in range(nhalf):
        toks = _granule_transpose(pieces[jh])
        for jj in range(TOKENS_PER_HALF):
            y_ref[pl.ds(jh * TOKENS_PER_HALF + jj, nchunk, stride=S5_CHUNK), :] = toks[jj]


def _s5_readout_call(yi, x4, wb, *, batch, seq):
    _, nq, nchunk, kd2 = yi.shape
    nv = nq // PAIRS_PER_VREG
    sw = PAIRS_PER_VREG * LANES
    xspec = pl.BlockSpec((nchunk, sw), lambda b, v: (0, b * nv + v))
    return pl.pallas_call(
        functools.partial(_s5_readout_kernel, nchunk=nchunk),
        grid=(batch, nv),
        in_specs=[pl.BlockSpec((1, PAIRS_PER_VREG, nchunk, kd2), lambda b, v: (b, v, 0, 0)),
                  xspec, xspec, xspec, xspec,
                  pl.BlockSpec((PAIRS_PER_VREG, wb.shape[1], kd2), lambda b, v: (v, 0, 0))],
        out_specs=pl.BlockSpec((seq, LANES), lambda b, v: (b, v)),
        out_shape=jax.ShapeDtypeStruct((batch * seq, nv * LANES), F32),
        name="s5_readout",
        compiler_params=_cparams(("arbitrary", "arbitrary"), 40),
    )(yi, *x4, wb)


def _s5_mixer(u, wa, wb, a_re, a_im, *, batch, seq):
    nchunk = seq // S5_CHUNK
    yi, sfr, sfi, sbr, sbi = _s5_chunk_call(u, wa, batch=batch, seq=seq)
    w = a_re.shape[1]
    a4 = [jnp.tile(a.reshape(1, w), (1, batch)) for a in (a_re[0], a_im[0], a_re[1], a_im[1])]
    tl = min(1024, batch * w)
    x4 = _s5scan_call([sfr, sfi, sbr, sbi], a4, nchunk=nchunk, tl=tl)
    return _s5_readout_call(yi, x4, wb, batch=batch, seq=seq)


def _post_kernel(y_ref, at_ref, gs_ref, ga_ref, x_ref, mod_ref, ln_ref,
                 wg_ref, bg_ref, wsu_ref, wau_ref, wo_ref, wr_ref,
                 x1_ref, h2_ref, aff_ref, *, d_model):
    z = jax.nn.gelu(y_ref[...], approximate=True)
    glu = jnp.dot(z.astype(BF16), wg_ref[...], preferred_element_type=F32) + bg_ref[...]
    ssm = z * jax.nn.sigmoid(glu)
    su = jnp.dot(ssm.astype(BF16), wsu_ref[...], preferred_element_type=F32)
    au = jnp.dot(at_ref[...], wau_ref[...], preferred_element_type=F32)
    merged = gs_ref[...].astype(F32) * su + ga_ref[...].astype(F32) * au
    mix = jnp.dot(merged.astype(BF16), wo_ref[...], preferred_element_type=F32)
    mod = mod_ref[0]
    g1 = mod[:, 2 * d_model:3 * d_model]
    sh2 = mod[:, 3 * d_model:4 * d_model]
    sc2 = mod[:, 4 * d_model:5 * d_model]
    x1 = x_ref[...] + g1 * mix
    x1_ref[...] = x1
    ms = jnp.mean(x1 * x1, axis=-1, keepdims=True)
    h2 = (x1 * lax.rsqrt(ms + EPS) * ln_ref[...]) * (1.0 + sc2) + sh2
    h2_ref[...] = h2.astype(h2_ref.dtype)
    lg = lax.dot_general(wr_ref[...], h2, (((1,), (1,)), ((), ())), preferred_element_type=F32,
                         precision=lax.Precision.HIGHEST)
    lg = lg - jnp.max(lg, axis=0, keepdims=True)
    ex = jnp.exp(lg)
    aff_ref[0] = ex / jnp.sum(ex, axis=0, keepdims=True)


def _post_call(y, attn, gs, ga, x2, mod3, ln_w, wg, bg, wsu, wau, wo, wr_t, *, seq, tm):
    t, d = x2.shape
    d_ssm = y.shape[1]
    d_attn = attn.shape[1]
    ne = wr_t.shape[0]
    tiles_per_seq = seq // tm
    row = lambda i: (i, 0)
    const = lambda i: (0, 0)
    return pl.pallas_call(
        functools.partial(_post_kernel, d_model=d),
        grid=(t // tm,),
        in_specs=[pl.BlockSpec((tm, d_ssm), row),
                  pl.BlockSpec((tm, d_attn), row),
                  pl.BlockSpec((tm, d), row),
                  pl.BlockSpec((tm, d), row),
                  pl.BlockSpec((tm, d), row),
                  pl.BlockSpec((1, 1, mod3.shape[2]), lambda i: (i // tiles_per_seq, 0, 0)),
                  pl.BlockSpec((1, d), const),
                  pl.BlockSpec(wg.shape, const),
                  pl.BlockSpec((1, d_ssm), const),
                  pl.BlockSpec(wsu.shape, const),
                  pl.BlockSpec(wau.shape, const),
                  pl.BlockSpec(wo.shape, const),
                  pl.BlockSpec(wr_t.shape, const)],
        out_specs=[pl.BlockSpec((tm, d), row),
                   pl.BlockSpec((tm, d), row),
                   pl.BlockSpec((1, ne, tm), lambda i: (i // tiles_per_seq, 0, i % tiles_per_seq))],
        out_shape=[jax.ShapeDtypeStruct((t, d), F32),
                   jax.ShapeDtypeStruct((t, d), BF16),
                   jax.ShapeDtypeStruct((t // seq, ne, seq), F32)],
        name="post_mix",
        compiler_params=_cparams(("arbitrary",), 48),
    )(y, attn, gs, ga, x2, mod3, ln_w, wg, bg, wsu, wau, wo, wr_t)


def _lane_cumsum(mask_f, tri):
    ne, seq = mask_f.shape
    run = jnp.zeros((ne, 1), F32)
    parts = []
    for j in range(seq // LANES):
        blk = mask_f[:, j * LANES:(j + 1) * LANES]
        cs = jnp.dot(blk.astype(BF16), tri, preferred_element_type=F32) + run
        parts.append(cs)
        run = run + jnp.sum(blk, axis=1, keepdims=True)
    return jnp.concatenate(parts, axis=-1)


def _select_kernel(aff_ref, tri_ref, rank_ref, gate_ref, *, cap):
    aff = aff_ref[0]
    ne = aff.shape[0]
    tri = tri_ref[...]

    def body(i, thr):
        cand = thr | (jnp.int32(1) << (30 - i))
        cnt = jnp.sum(jnp.where(aff >= pltpu.bitcast(cand, F32), 1.0, 0.0), axis=1, keepdims=True)
        return jnp.where(cnt >= cap, cand, thr)

    thr = lax.fori_loop(0, 31, body, jnp.zeros((ne, 1), jnp.int32))
    gt = aff >= pltpu.bitcast(thr + 1, F32)
    eq = (aff >= pltpu.bitcast(thr, F32)) & jnp.logical_not(gt)
    need = cap - jnp.sum(jnp.where(gt, 1.0, 0.0), axis=1, keepdims=True)
    eq_f = jnp.where(eq, 1.0, 0.0)
    eq_rank = _lane_cumsum(eq_f, tri) - eq_f
    sel = gt | (eq & (eq_rank < need))
    sel_f = jnp.where(sel, 1.0, 0.0)
    rank = _lane_cumsum(sel_f, tri) - sel_f
    rank_ref[0] = jnp.where(sel, rank.astype(jnp.int32), -1)
    gate_ref[0] = jnp.where(sel, aff, 0.0)


def _select_call(aff_t, tri, *, cap):
    b, ne, seq = aff_t.shape
    spec = pl.BlockSpec((1, ne, seq), lambda i: (i, 0, 0))
    return pl.pallas_call(
        functools.partial(_select_kernel, cap=cap),
        grid=(b,),
        in_specs=[spec, pl.BlockSpec((LANES, LANES), lambda i: (0, 0))],
        out_specs=[spec, spec],
        out_shape=[jax.ShapeDtypeStruct((b, ne, seq), jnp.int32),
                   jax.ShapeDtypeStruct((b, ne, seq), F32)],
        name="expert_select",
        compiler_params=_cparams(("arbitrary",), 32),
    )(aff_t, tri)


def _moe_kernel(h_ref, rank_ref, gate_ref, wg_ref, wu_ref, wd_ref, ye_ref, xg_ref, gc_ref, *, cap, ts):
    f = pl.program_id(2)
    seq = h_ref.shape[0]

    @pl.when(f == 0)
    def _gather():
        slot = lax.broadcasted_iota(jnp.int32, (cap, ts), 0)
        acc = jnp.zeros(xg_ref.shape, F32)
        gc = jnp.zeros((cap, 1), F32)
        for j in range(seq // ts):
            hit = rank_ref[0, :, j * ts:(j + 1) * ts] == slot
            onehot = jnp.where(hit, 1.0, 0.0)
            acc = acc + jnp.dot(onehot.astype(BF16), h_ref[j * ts:(j + 1) * ts, :],
                                preferred_element_type=F32)
            gc = gc + jnp.sum(onehot * gate_ref[0, :, j * ts:(j + 1) * ts], axis=1, keepdims=True)
        xg_ref[...] = acc.astype(xg_ref.dtype)
        gc_ref[...] = gc
        ye_ref[...] = jnp.zeros_like(ye_ref)

    xg = xg_ref[...]
    hg = jnp.dot(xg, wg_ref[0], preferred_element_type=F32)
    hu = jnp.dot(xg, wu_ref[0], preferred_element_type=F32)
    hid = (hg * jax.nn.sigmoid(hg) * hu).astype(BF16)
    ye_ref[0, 0] = ye_ref[0, 0] + jnp.dot(hid, wd_ref[0], preferred_element_type=F32) * gc_ref[...]


def _moe_call(h2, rank3, gate3, wg, wu, wd, *, batch, seq, cap, nf):
    t, d = h2.shape
    ne, _, dff = wg.shape
    tf = dff // nf
    ts = min(1024, seq)
    return pl.pallas_call(
        functools.partial(_moe_kernel, cap=cap, ts=ts),
        grid=(ne, batch, nf),
        in_specs=[pl.BlockSpec((seq, d), lambda e, b, f: (b, 0)),
                  pl.BlockSpec((1, 1, seq), lambda e, b, f: (b * ne + e, 0, 0)),
                  pl.BlockSpec((1, 1, seq), lambda e, b, f: (b * ne + e, 0, 0)),
                  pl.BlockSpec((1, d, tf), lambda e, b, f: (e, 0, f)),
                  pl.BlockSpec((1, d, tf), lambda e, b, f: (e, 0, f)),
                  pl.BlockSpec((1, tf, d), lambda e, b, f: (e, f, 0))],
        out_specs=pl.BlockSpec((1, 1, cap, d), lambda e, b, f: (b, e, 0, 0)),
        out_shape=jax.ShapeDtypeStruct((batch, ne, cap, d), F32),
        scratch_shapes=[pltpu.VMEM((cap, d), BF16), pltpu.VMEM((cap, 1), F32)],
        name="expert_ffn",
        compiler_params=_cparams(("arbitrary", "arbitrary", "arbitrary"), 56),
    )(h2, rank3, gate3, wg, wu, wd)


def _combine_kernel(ye_ref, rank_ref, x1_ref, mod_ref, o_ref, *, cap, d_model):
    e = pl.program_id(1)

    @pl.when(e == 0)
    def _init():
        o_ref[...] = jnp.zeros_like(o_ref)

    ts = o_ref.shape[0]
    slot = lax.broadcasted_iota(jnp.int32, (cap, ts), 0)
    onehot = jnp.where(rank_ref[0] == slot, 1.0, 0.0).astype(BF16)
    o_ref[...] += lax.dot_general(onehot, ye_ref[0, 0].astype(BF16), (((0,), (0,)), ((), ())),
                                  preferred_element_type=F32)

    @pl.when(e == pl.num_programs(1) - 1)
    def _fin():
        g2 = mod_ref[0][:, 5 * d_model:6 * d_model]
        o_ref[...] = x1_ref[...] + g2 * o_ref[...]


def _combine_call(ye, rank3, x1, mod3, *, batch, seq, cap, ts):
    t, d = x1.shape
    ne = ye.shape[1]
    nt = seq // ts
    return pl.pallas_call(
        functools.partial(_combine_kernel, cap=cap, d_model=d),
        grid=(t // ts, ne),
        in_specs=[pl.BlockSpec((1, 1, cap, d), lambda i, e: (i // nt, e, 0, 0)),
                  pl.BlockSpec((1, 1, ts), lambda i, e: ((i // nt) * ne + e, 0, i % nt)),
                  pl.BlockSpec((ts, d), lambda i, e: (i, 0)),
                  pl.BlockSpec((1, 1, mod3.shape[2]), lambda i, e: (i // nt, 0, 0))],
        out_specs=pl.BlockSpec((ts, d), lambda i, e: (i, 0)),
        out_shape=jax.ShapeDtypeStruct((t, d), F32),
        name="moe_combine",
        compiler_params=_cparams(("arbitrary", "arbitrary"), 48),
    )(ye, rank3, x1, mod3)


def kernel(x, c, w_mod, b_mod, ln1_w, ln2_w, w_in, ssm_lam_re, ssm_lam_im, ssm_log_dt, ssm_b_re,
           ssm_b_im, ssm_c_re, ssm_c_im, ssm_d, w_glu, b_glu, q_norm_w, k_norm_w, w_ssm_up,
           w_attn_up, w_out, w_router, w_exp_gate, w_exp_up, w_exp_down):
    batch, seq, d = x.shape
    depth = w_mod.shape[0]
    t = batch * seq
    d_attn = N_Q_HEADS * HEAD_DIM
    d_kv = N_KV_HEADS * HEAD_DIM
    cap = CAPACITY_FACTOR * seq // N_EXPERTS
    tm = min(512, seq)
    tq = min(256, seq)

    tabs = _rope_tables(seq)
    head_id = jnp.arange(d_attn) // HEAD_DIM
    hm = (head_id[:, None] == head_id[None, :]).astype(BF16)
    tri = (jnp.arange(LANES)[:, None] <= jnp.arange(LANES)[None, :]).astype(BF16)
    c_pad = jnp.zeros((8, d), F32).at[:batch].set(c)
    mod_all = _mod_call(c_pad, w_mod, b_mod)[:, :batch]
    wa_all, wb_all, a_re_all, a_im_all = _s5_weights(ssm_lam_re, ssm_lam_im, ssm_log_dt, ssm_b_re,
                                                     ssm_b_im, ssm_c_re, ssm_c_im, ssm_d)

    x2 = x.reshape(t, d)
    for l in range(depth):
        mod3 = mod_all[l].reshape(batch, 1, 6 * d)
        qw = jnp.tile(q_norm_w[l], N_Q_HEADS).reshape(1, d_attn)
        kw = jnp.tile(k_norm_w[l], N_KV_HEADS).reshape(1, d_kv)
        u, q, k, v, gs, ga = _in_call(x2, mod3, ln1_w[l].reshape(1, d), w_in[l].astype(BF16), tabs,
                                      qw, kw, hm, seq=seq, tm=tm)
        y = _s5_mixer(u, wa_all[l], wb_all[l], a_re_all[l], a_im_all[l], batch=batch, seq=seq)
        attn = _attn_call(q, k, v, seq=seq, tq=tq)
        x1, h2, aff_t = _post_call(
            y, attn, gs, ga, x2, mod3, ln2_w[l].reshape(1, d), w_glu[l].astype(BF16),
            b_glu[l].reshape(1, -1), w_ssm_up[l].astype(BF16), w_attn_up[l].astype(BF16),
            w_out[l].astype(BF16), w_router[l].T, seq=seq, tm=tm)
        rank, gate = _select_call(aff_t, tri, cap=cap)
        rank3 = rank.reshape(batch * N_EXPERTS, 1, seq)
        gate3 = gate.reshape(batch * N_EXPERTS, 1, seq)
        ye = _moe_call(h2, rank3, gate3, w_exp_gate[l].astype(BF16), w_exp_up[l].astype(BF16),
                       w_exp_down[l].astype(BF16), batch=batch, seq=seq, cap=cap, nf=2)
        x2 = _combine_call(ye, rank3, x1, mod3, batch=batch, seq=seq, cap=cap, ts=min(1024, seq))
    return x2.reshape(batch, seq, d)
```

```python
import functools
import math

import jax
import jax.numpy as jnp
from jax import lax
from jax.experimental import pallas as pl
from jax.experimental.pallas import tpu as pltpu

F32 = jnp.float32
BF16 = jnp.bfloat16

HEAD_DIM = 64
N_Q_HEADS = 8
N_KV_HEADS = 2
Q_PER_KV = N_Q_HEADS // N_KV_HEADS
ROT_HALF = HEAD_DIM // 2
ROPE_THETA = 10000.0
GRID_W = 64
SSM_GROUP = 16
SSM_STATE = 64
N_EXPERTS = 16
CAPACITY_FACTOR = 2
EPS = 1e-6

S5_CHUNK = 16
S5_PAIR = 2
LANES = 128
V7X_VMEM_BYTES = 64 * 1024 * 1024
LOG2E = 1.4426950408889634


def _cparams(semantics, vmem_mb):
    return pltpu.CompilerParams(dimension_semantics=semantics,
                                vmem_limit_bytes=int(vmem_mb * 1024 * 1024))


def _mod_kernel(c_ref, w_ref, b_ref, o_ref):
    c = c_ref[...]
    ca = c * jax.nn.sigmoid(c)
    o_ref[0] = jnp.dot(ca, w_ref[0], preferred_element_type=F32,
                       precision=lax.Precision.HIGHEST) + b_ref[0]


def _mod_call(c_pad, w_mod, b_mod):
    depth, d, n = w_mod.shape
    rows = c_pad.shape[0]
    tn = 1536
    return pl.pallas_call(
        _mod_kernel,
        grid=(depth, n // tn),
        in_specs=[pl.BlockSpec((rows, d), lambda l, j: (0, 0)),
                  pl.BlockSpec((1, d, tn), lambda l, j: (l, 0, j)),
                  pl.BlockSpec((1, 1, tn), lambda l, j: (l, 0, j))],
        out_specs=pl.BlockSpec((1, rows, tn), lambda l, j: (l, 0, j)),
        out_shape=jax.ShapeDtypeStruct((depth, rows, n), F32),
        name="adaln_mod",
        compiler_params=_cparams(("arbitrary", "arbitrary"), 32),
    )(c_pad, w_mod, b_mod.reshape(depth, 1, n))


def _head_mean_square(xsq, hm):
    hi = xsq.astype(BF16)
    lo = (xsq - hi.astype(F32)).astype(BF16)
    s = jnp.dot(hi, hm, preferred_element_type=F32) + jnp.dot(lo, hm, preferred_element_type=F32)
    return s * (1.0 / HEAD_DIM)


def _rope_block(blk, cos, sa, sb):
    return (blk * cos + pltpu.roll(blk, LANES - ROT_HALF // 2, 1) * sa
            + pltpu.roll(blk, ROT_HALF // 2, 1) * sb)


def _in_kernel(x_ref, mod_ref, ln_ref, w_ref, cos_ref, sa_ref, sb_ref, qw_ref, kw_ref, hm_ref,
               u_ref, q_ref, k_ref, v_ref, gs_ref, ga_ref, *, d_model, d_ssm, d_attn, d_kv):
    x = x_ref[...]
    mod = mod_ref[0]
    sh = mod[:, 0:d_model]
    sc = mod[:, d_model:2 * d_model]
    ms = jnp.mean(x * x, axis=-1, keepdims=True)
    h = (x * lax.rsqrt(ms + EPS) * ln_ref[...]) * (1.0 + sc) + sh
    proj = jnp.dot(h.astype(BF16), w_ref[...], preferred_element_type=F32)
    o = 0
    u_ref[...] = proj[:, o:o + d_ssm].astype(u_ref.dtype)
    o += d_ssm
    q = proj[:, o:o + d_attn]
    o += d_attn
    k = proj[:, o:o + d_kv]
    o += d_kv
    v_ref[0] = proj[:, o:o + d_kv].T.astype(v_ref.dtype)
    o += d_kv
    gs_ref[...] = jax.nn.sigmoid(proj[:, o:o + d_model]).astype(gs_ref.dtype)
    o += d_model
    ga_ref[...] = jax.nn.sigmoid(proj[:, o:o + d_model]).astype(ga_ref.dtype)

    hm = hm_ref[...]
    cos = cos_ref[...]
    sa = sa_ref[...]
    sb = sb_ref[...]
    qn = q * lax.rsqrt(_head_mean_square(q * q, hm) + EPS) * qw_ref[...]
    kn = k * lax.rsqrt(_head_mean_square(k * k, hm[:d_kv, :d_kv]) + EPS) * kw_ref[...]
    qscale = (HEAD_DIM ** -0.5) * LOG2E
    qr = [_rope_block(qn[:, j * LANES:(j + 1) * LANES], cos, sa, sb) for j in range(d_attn // LANES)]
    q_ref[...] = (jnp.concatenate(qr, axis=-1) * qscale).astype(q_ref.dtype)
    kr = [_rope_block(kn[:, j * LANES:(j + 1) * LANES], cos, sa, sb) for j in range(d_kv // LANES)]
    k_ref[...] = jnp.concatenate(kr, axis=-1).astype(k_ref.dtype)


def _in_call(x2, mod3, ln_w, w_in, tabs, qw, kw, hm, *, seq, tm):
    t, d = x2.shape
    n_in = w_in.shape[1]
    d_ssm = d // 2
    d_attn = N_Q_HEADS * HEAD_DIM
    d_kv = N_KV_HEADS * HEAD_DIM
    tiles_per_seq = seq // tm
    cos, sa, sb = tabs
    kern = functools.partial(_in_kernel, d_model=d, d_ssm=d_ssm, d_attn=d_attn, d_kv=d_kv)
    row = lambda i: (i, 0)
    const = lambda i: (0, 0)
    tab = lambda i: (i % tiles_per_seq, 0)
    return pl.pallas_call(
        kern,
        grid=(t // tm,),
        in_specs=[pl.BlockSpec((tm, d), row),
                  pl.BlockSpec((1, 1, mod3.shape[2]), lambda i: (i // tiles_per_seq, 0, 0)),
                  pl.BlockSpec((1, d), const),
                  pl.BlockSpec((d, n_in), const),
                  pl.BlockSpec((tm, LANES), tab),
                  pl.BlockSpec((tm, LANES), tab),
                  pl.BlockSpec((tm, LANES), tab),
                  pl.BlockSpec((1, d_attn), const),
                  pl.BlockSpec((1, d_kv), const),
                  pl.BlockSpec((d_attn, d_attn), const)],
        out_specs=[pl.BlockSpec((tm, d_ssm), row),
                   pl.BlockSpec((tm, d_attn), row),
                   pl.BlockSpec((tm, d_kv), row),
                   pl.BlockSpec((1, d_kv, tm), lambda i: (i // tiles_per_seq, 0, i % tiles_per_seq)),
                   pl.BlockSpec((tm, d), row),
                   pl.BlockSpec((tm, d), row)],
        out_shape=[jax.ShapeDtypeStruct((t, d_ssm), F32),
                   jax.ShapeDtypeStruct((t, d_attn), BF16),
                   jax.ShapeDtypeStruct((t, d_kv), BF16),
                   jax.ShapeDtypeStruct((t // seq, d_kv, seq), BF16),
                   jax.ShapeDtypeStruct((t, d), BF16),
                   jax.ShapeDtypeStruct((t, d), BF16)],
        name="in_proj",
        compiler_params=_cparams(("arbitrary",), 48),
    )(x2, mod3, ln_w, w_in, cos, sa, sb, qw, kw, hm)


def _rope_tables(seq):
    pos = jnp.arange(seq)
    row = (pos // GRID_W).astype(F32)
    col = (pos % GRID_W).astype(F32)
    inv = 1.0 / (ROPE_THETA ** (jnp.arange(0, ROT_HALF, 2, dtype=F32) / ROT_HALF))
    ang_r = row[:, None] * inv[None, :]
    ang_c = col[:, None] * inv[None, :]
    zeros = jnp.zeros_like(ang_r)
    cos_h = jnp.concatenate([jnp.cos(ang_r), jnp.cos(ang_r), jnp.cos(ang_c), jnp.cos(ang_c)], axis=-1)
    sa_h = jnp.concatenate([-jnp.sin(ang_r), zeros, -jnp.sin(ang_c), zeros], axis=-1)
    sb_h = jnp.concatenate([zeros, jnp.sin(ang_r), zeros, jnp.sin(ang_c)], axis=-1)
    rep = LANES // HEAD_DIM
    return tuple(jnp.tile(a, (1, rep)) for a in (cos_h, sa_h, sb_h))


def _attn_heads_out(o_t, tq):
    return [o_t[:, g * tq:(g + 1) * tq].T for g in range(Q_PER_KV)]


def _attn_kernel(q_ref, k_ref, vt_ref, o_ref, *, kb):
    tq = q_ref.shape[0]
    seq = k_ref.shape[0]
    nblk = seq // kb
    width = Q_PER_KV * HEAD_DIM
    q_ts = []
    for kvh in range(N_KV_HEADS):
        qt = q_ref[:, kvh * width:(kvh + 1) * width].astype(F32).T.astype(BF16)
        q_ts.append(jnp.concatenate([qt[g * HEAD_DIM:(g + 1) * HEAD_DIM] for g in range(Q_PER_KV)],
                                    axis=1))

    def scores(kvh, j):
        lo = kvh * HEAD_DIM
        return jnp.dot(k_ref[j * kb:(j + 1) * kb, lo:lo + HEAD_DIM], q_ts[kvh],
                       preferred_element_type=F32)

    def weighted_v(kvh, j, p):
        lo = kvh * HEAD_DIM
        return jnp.dot(vt_ref[0, lo:lo + HEAD_DIM, j * kb:(j + 1) * kb], p.astype(BF16),
                       preferred_element_type=F32)

    outs = []
    worst = jnp.zeros((1, 1), F32)
    for kvh in range(N_KV_HEADS):
        s0 = scores(kvh, 0)
        m0 = jnp.max(s0, axis=0, keepdims=True)
        l = jnp.zeros_like(m0)
        acc = jnp.zeros((HEAD_DIM, m0.shape[1]), F32)
        s_cur = s0
        for j in range(nblk):
            s_next = scores(kvh, j + 1) if j + 1 < nblk else None
            p = jnp.exp2(s_cur - m0)
            l = l + jnp.sum(p, axis=0, keepdims=True)
            acc = acc + weighted_v(kvh, j, p)
            s_cur = s_next
        worst = jnp.maximum(worst, jnp.max(l, axis=1, keepdims=True))
        outs.extend(_attn_heads_out(acc / l, tq))
    o_ref[...] = jnp.concatenate(outs, axis=-1).astype(o_ref.dtype)

    @pl.when(worst[0, 0] > jnp.finfo(F32).max)
    def _exact():
        outs = []
        for kvh in range(N_KV_HEADS):
            m = jnp.full((1, Q_PER_KV * tq), -jnp.inf, F32)
            l = jnp.zeros_like(m)
            acc = jnp.zeros((HEAD_DIM, Q_PER_KV * tq), F32)
            for j in range(nblk):
                s = scores(kvh, j)
                m_new = jnp.maximum(m, jnp.max(s, axis=0, keepdims=True))
                alpha = jnp.exp2(m - m_new)
                p = jnp.exp2(s - m_new)
                l = l * alpha + jnp.sum(p, axis=0, keepdims=True)
                acc = acc * alpha + weighted_v(kvh, j, p)
                m = m_new
            outs.extend(_attn_heads_out(acc / l, tq))
        o_ref[...] = jnp.concatenate(outs, axis=-1).astype(o_ref.dtype)


def _attn_call(q, k, vt, *, seq, tq):
    t, d_attn = q.shape
    d_kv = k.shape[1]
    nq = seq // tq
    return pl.pallas_call(
        functools.partial(_attn_kernel, kb=min(256, seq)),
        grid=(t // seq, nq),
        in_specs=[pl.BlockSpec((tq, d_attn), lambda b, i: (b * nq + i, 0)),
                  pl.BlockSpec((seq, d_kv), lambda b, i: (b, 0)),
                  pl.BlockSpec((1, d_kv, seq), lambda b, i: (b, 0, 0))],
        out_specs=pl.BlockSpec((tq, d_attn), lambda b, i: (b * nq + i, 0)),
        out_shape=jax.ShapeDtypeStruct((t, d_attn), BF16),
        name="attention",
        compiler_params=_cparams(("arbitrary", "arbitrary"), 48),
    )(q, k, vt)


def _s5_weights(lam_re, lam_im, log_dt, b_re, b_im, c_re, c_im, d_skip):
    hp = lax.Precision.HIGHEST
    nl, _, ng, p = lam_re.shape
    cg = SSM_GROUP
    L = S5_CHUNK
    kd = L * cg
    dt = jnp.exp(log_dt)[..., None]
    zr = lam_re * dt
    zi = lam_im * dt

    def cpow(e):
        mag = jnp.exp(zr[..., None] * e)
        return mag * jnp.cos(zi[..., None] * e), mag * jnp.sin(zi[..., None] * e)

    lb_re, lb_im = jnp.exp(zr) * jnp.cos(zi), jnp.exp(zr) * jnp.sin(zi)
    nr, ni = lb_re - 1.0, lb_im
    den = lam_re * lam_re + lam_im * lam_im
    f_re = ((nr * lam_re + ni * lam_im) / den)[..., None]
    f_im = ((ni * lam_re - nr * lam_im) / den)[..., None]
    bb_re = f_re * b_re - f_im * b_im
    bb_im = f_re * b_im + f_im * b_re

    m_idx = jnp.arange(L + 1, dtype=F32)
    expo = jnp.stack([m_idx, L - m_idx])[None, :, None, None, :]
    pw_re, pw_im = cpow(expo)
    rep = jnp.repeat(jnp.eye(L + 1, dtype=F32), cg, axis=1)
    til = jnp.tile(jnp.eye(cg, dtype=F32), (1, L + 1))
    pwx_re = jnp.einsum('dxgpm,mn->dxgpn', pw_re, rep, precision=hp)
    pwx_im = jnp.einsum('dxgpm,mn->dxgpn', pw_im, rep, precision=hp)
    ct_re = jnp.einsum('dxgcp,cn->dxgpn', c_re, til, precision=hp)
    ct_im = jnp.einsum('dxgcp,cn->dxgpn', c_im, til, precision=hp)
    ca_re = ct_re * pwx_re - ct_im * pwx_im
    ca_im = ct_re * pwx_im + ct_im * pwx_re
    ef_re, ef_im = ca_re[:, 0, ..., :kd], ca_im[:, 0, ..., :kd]
    w2f_re, w2f_im = ca_re[:, 0, ..., cg:], ca_im[:, 0, ..., cg:]
    w2b_re, w2b_im = ca_re[:, 1, ..., :kd], ca_im[:, 1, ..., :kd]
    eb_re, eb_im = ca_re[:, 1, ..., cg:], ca_im[:, 1, ..., cg:]

    def lag_rows(x, e_re, e_im):
        return (jnp.einsum('dgpe,dgpn->dgen', bb_re[:, x], e_re, precision=hp)
                - jnp.einsum('dgpe,dgpn->dgen', bb_im[:, x], e_im, precision=hp))

    kf = lag_rows(0, ef_re, ef_im)
    kb = lag_rows(1, eb_re, eb_im)
    zeros = jnp.zeros_like(kf)
    pf = jnp.concatenate([zeros, kf], axis=-1)
    pb = jnp.concatenate([kb, zeros], axis=-1)
    tf = jnp.stack([pf[..., kd - cg * s:2 * kd - cg * s] for s in range(L)], axis=2)
    tb = jnp.stack([pb[..., cg * (L - 1 - s):cg * (L - 1 - s) + kd] for s in range(L)], axis=2)
    skip = jnp.eye(kd, dtype=F32) * jnp.tile(d_skip.reshape(nl, ng, 1, cg), (1, 1, 1, L))
    tmat = (tf + tb).reshape(nl, ng, kd, kd) + skip

    s_idx = jnp.arange(L, dtype=F32)
    sexp = jnp.stack([L - 1 - s_idx, s_idx])[None, :, None, None, :]
    ps_re, ps_im = cpow(sexp)
    w1_re = (jnp.einsum('dxgps,dxgpe->dxgsep', ps_re, bb_re)
             - jnp.einsum('dxgps,dxgpe->dxgsep', ps_im, bb_im)).reshape(nl, 2, ng, kd, p)
    w1_im = (jnp.einsum('dxgps,dxgpe->dxgsep', ps_re, bb_im)
             + jnp.einsum('dxgps,dxgpe->dxgsep', ps_im, bb_re)).reshape(nl, 2, ng, kd, p)

    nq = ng // S5_PAIR
    eye2 = jnp.eye(S5_PAIR, dtype=F32)

    def pair_diag(mat):
        r, c = mat.shape[2], mat.shape[3]
        mat = mat.reshape(nl, nq, S5_PAIR, r, c)
        return jnp.einsum('dqgrc,gh->dqgrhc', mat, eye2).reshape(nl, nq, S5_PAIR * r, S5_PAIR * c)

    wa = jnp.concatenate([pair_diag(tmat), pair_diag(w1_re[:, 0]), pair_diag(w1_im[:, 0]),
                          pair_diag(w1_re[:, 1]), pair_diag(w1_im[:, 1])], axis=-1)
    wb = jnp.concatenate([pair_diag(w2f_re), pair_diag(-w2f_im),
                          pair_diag(w2b_re), pair_diag(-w2b_im)], axis=2)
    al_re, al_im = cpow(jnp.float32(L))
    a_re = al_re.reshape(nl, 2, ng * p)
    a_im = al_im.reshape(nl, 2, ng * p)
    return wa.astype(BF16), wb.astype(BF16), a_re, a_im


GROUPS_PER_VREG = LANES // SSM_GROUP
PAIRS_PER_VREG = GROUPS_PER_VREG // S5_PAIR
TOKENS_PER_HALF = LANES // SSM_GROUP


def _granule_transpose(arrs):
    gran = lax.broadcasted_iota(jnp.int32, arrs[0].shape, 1) // SSM_GROUP
    cur = list(arrs)
    for dist in (4, 2, 1):
        keep = (gran & dist) == 0
        nxt = list(cur)
        for i in range(len(cur)):
            if i & dist:
                continue
            a, b = cur[i], cur[i + dist]
            nxt[i] = jnp.where(keep, a, pltpu.roll(b, dist * SSM_GROUP, 1))
            nxt[i + dist] = jnp.where(keep, pltpu.roll(a, LANES - dist * SSM_GROUP, 1), b)
        cur = nxt
    return cur


def _s5_chunk_kernel(u_ref, wa_ref, yi_ref, sfr_ref, sfi_ref, sbr_ref, sbi_ref, *, nchunk):
    halves = []
    for jh in range(S5_CHUNK // TOKENS_PER_HALF):
        toks = [u_ref[pl.ds(jh * TOKENS_PER_HALF + jj, nchunk, stride=S5_CHUNK), :]
                for jj in range(TOKENS_PER_HALF)]
        halves.append([o.astype(BF16) for o in _granule_transpose(toks)])
    kd2 = S5_PAIR * S5_CHUNK * SSM_GROUP
    for qq in range(PAIRS_PER_VREG):
        z = jnp.concatenate([halves[jh][S5_PAIR * qq + gl] for gl in range(S5_PAIR)
                             for jh in range(len(halves))], axis=-1)
        res = jnp.dot(z, wa_ref[qq], preferred_element_type=F32)
        lo = qq * LANES
        yi_ref[0, qq] = res[:, :kd2]
        sfr_ref[:, lo:lo + LANES] = res[:, kd2:kd2 + LANES]
        sfi_ref[:, lo:lo + LANES] = res[:, kd2 + LANES:kd2 + 2 * LANES]
        sbr_ref[:, lo:lo + LANES] = res[:, kd2 + 2 * LANES:kd2 + 3 * LANES]
        sbi_ref[:, lo:lo + LANES] = res[:, kd2 + 3 * LANES:kd2 + 4 * LANES]


def _s5_chunk_call(u, wa, *, batch, seq):
    t, d_ssm = u.shape
    nchunk = seq // S5_CHUNK
    nv = d_ssm // LANES
    nq, kd2, ncol = wa.shape
    sw = PAIRS_PER_VREG * LANES
    st = jax.ShapeDtypeStruct((nchunk, batch * nv * sw), F32)
    sspec = pl.BlockSpec((nchunk, sw), lambda b, v: (0, b * nv + v))
    return pl.pallas_call(
        functools.partial(_s5_chunk_kernel, nchunk=nchunk),
        grid=(batch, nv),
        in_specs=[pl.BlockSpec((seq, LANES), lambda b, v: (b, v)),
                  pl.BlockSpec((PAIRS_PER_VREG, kd2, ncol), lambda b, v: (v, 0, 0))],
        out_specs=[pl.BlockSpec((1, PAIRS_PER_VREG, nchunk, kd2), lambda b, v: (b, v, 0, 0)),
                   sspec, sspec, sspec, sspec],
        out_shape=[jax.ShapeDtypeStruct((batch, nq, nchunk, kd2), F32), st, st, st, st],
        name="s5_chunk",
        compiler_params=_cparams(("arbitrary", "arbitrary"), 40),
    )(u, wa)


def _s5scan_kernel(sfr_ref, sfi_ref, sbr_ref, sbi_ref, afr_ref, afi_ref, abr_ref, abi_ref,
                   xfr_ref, xfi_ref, xbr_ref, xbi_ref, *, nchunk):
    afr, afi = afr_ref[...], afi_ref[...]
    abr, abi = abr_ref[...], abi_ref[...]
    zero = jnp.zeros_like(afr)

    def body(k, carry):
        fr, fi, br, bi = carry
        kb = nchunk - 1 - k
        rowf = pl.ds(k, 1)
        rowb = pl.ds(kb, 1)
        xfr_ref[rowf, :] = fr
        xfi_ref[rowf, :] = fi
        xbr_ref[rowb, :] = br
        xbi_ref[rowb, :] = bi
        nfr = fr * afr - fi * afi + sfr_ref[rowf, :]
        nfi = fr * afi + fi * afr + sfi_ref[rowf, :]
        nbr = br * abr - bi * abi + sbr_ref[rowb, :]
        nbi = br * abi + bi * abr + sbi_ref[rowb, :]
        return nfr, nfi, nbr, nbi

    lax.fori_loop(0, nchunk, body, (zero, zero, zero, zero))


def _s5scan_call(s4, a4, *, nchunk, tl):
    w = s4[0].shape[1]
    sspec = pl.BlockSpec((nchunk, tl), lambda j: (0, j))
    aspec = pl.BlockSpec((1, tl), lambda j: (0, j))
    st = jax.ShapeDtypeStruct((nchunk, w), F32)
    return pl.pallas_call(
        functools.partial(_s5scan_kernel, nchunk=nchunk),
        grid=(w // tl,),
        in_specs=[sspec] * 4 + [aspec] * 4,
        out_specs=[sspec] * 4,
        out_shape=[st] * 4,
        name="s5_scan",
        compiler_params=_cparams(("arbitrary",), 32),
    )(*s4, *a4)


def _s5_readout_kernel(yi_ref, xfr_ref, xfi_ref, xbr_ref, xbi_ref, wb_ref, y_ref, *, nchunk):
    nhalf = S5_CHUNK // TOKENS_PER_HALF
    pieces = [[None] * GROUPS_PER_VREG for _ in range(nhalf)]
    for qq in range(PAIRS_PER_VREG):
        lo = qq * LANES
        xs = jnp.concatenate([xfr_ref[:, lo:lo + LANES], xfi_ref[:, lo:lo + LANES],
                              xbr_ref[:, lo:lo + LANES], xbi_ref[:, lo:lo + LANES]], axis=-1)
        y = yi_ref[0, qq] + jnp.dot(xs.astype(BF16), wb_ref[qq], preferred_element_type=F32)
        for gl in range(S5_PAIR):
            for jh in range(nhalf):
                c0 = (gl * nhalf + jh) * LANES
                pieces[jh][S5_PAIR * qq + gl] = y[:, c0:c0 + LANES]
    for jh in range(nhalf):
        toks = _granule_transpose(pieces[jh])
        for jj in range(TOKENS_PER_HALF):
            y_ref[pl.ds(jh * TOKENS_PER_HALF + jj, nchunk, stride=S5_CHUNK), :] = toks[jj]


def _s5_readout_call(yi, x4, wb, *, batch, seq):
    _, nq, nchunk, kd2 = yi.shape
    nv = nq // PAIRS_PER_VREG
    sw = PAIRS_PER_VREG * LANES
    xspec = pl.BlockSpec((nchunk, sw), lambda b, v: (0, b * nv + v))
    return pl.pallas_call(
        functools.partial(_s5_readout_kernel, nchunk=nchunk),
        grid=(batch, nv),
        in_specs=[pl.BlockSpec((1, PAIRS_PER_VREG, nchunk, kd2), lambda b, v: (b, v, 0, 0)),
                  xspec, xspec, xspec, xspec,
                  pl.BlockSpec((PAIRS_PER_VREG, wb.shape[1], kd2), lambda b, v: (v, 0, 0))],
        out_specs=pl.BlockSpec((seq, LANES), lambda b, v: (b, v)),
        out_shape=jax.ShapeDtypeStruct((batch * seq, nv * LANES), F32),
        name="s5_readout",
        compiler_params=_cparams(("arbitrary", "arbitrary"), 40),
    )(yi, *x4, wb)


def _s5_mixer(u, wa, wb, a_re, a_im, *, batch, seq):
    nchunk = seq // S5_CHUNK
    yi, sfr, sfi, sbr, sbi = _s5_chunk_call(u, wa, batch=batch, seq=seq)
    w = a_re.shape[1]
    a4 = [jnp.tile(a.reshape(1, w), (1, batch)) for a in (a_re[0], a_im[0], a_re[1], a_im[1])]
    tl = min(1024, batch * w)
    x4 = _s5scan_call([sfr, sfi, sbr, sbi], a4, nchunk=nchunk, tl=tl)
    return _s5_readout_call(yi, x4, wb, batch=batch, seq=seq)


def _post_kernel(y_ref, at_ref, gs_ref, ga_ref, x_ref, mod_ref, ln_ref,
                 wg_ref, bg_ref, wsu_ref, wau_ref, wo_ref, wr_ref,
                 x1_ref, h2_ref, aff_ref, *, d_model):
    z = jax.nn.gelu(y_ref[...], approximate=True)
    glu = jnp.dot(z.astype(BF16), wg_ref[...], preferred_element_type=F32) + bg_ref[...]
    ssm = z * jax.nn.sigmoid(glu)
    su = jnp.dot(ssm.astype(BF16), wsu_ref[...], preferred_element_type=F32)
    au = jnp.dot(at_ref[...], wau_ref[...], preferred_element_type=F32)
    merged = gs_ref[...].astype(F32) * su + ga_ref[...].astype(F32) * au
    mix = jnp.dot(merged.astype(BF16), wo_ref[...], preferred_element_type=F32)
    mod = mod_ref[0]
    g1 = mod[:, 2 * d_model:3 * d_model]
    sh2 = mod[:, 3 * d_model:4 * d_model]
    sc2 = mod[:, 4 * d_model:5 * d_model]
    x1 = x_ref[...] + g1 * mix
    x1_ref[...] = x1
    ms = jnp.mean(x1 * x1, axis=-1, keepdims=True)
    h2 = (x1 * lax.rsqrt(ms + EPS) * ln_ref[...]) * (1.0 + sc2) + sh2
    h2_ref[...] = h2.astype(h2_ref.dtype)
    lg = lax.dot_general(wr_ref[...], h2, (((1,), (1,)), ((), ())), preferred_element_type=F32,
                         precision=lax.Precision.HIGHEST)
    lg = lg - jnp.max(lg, axis=0, keepdims=True)
    ex = jnp.exp(lg)
    aff_ref[0] = ex / jnp.sum(ex, axis=0, keepdims=True)


def _post_call(y, attn, gs, ga, x2, mod3, ln_w, wg, bg, wsu, wau, wo, wr_t, *, seq, tm):
    t, d = x2.shape
    d_ssm = y.shape[1]
    d_attn = attn.shape[1]
    ne = wr_t.shape[0]
    tiles_per_seq = seq // tm
    row = lambda i: (i, 0)
    const = lambda i: (0, 0)
    return pl.pallas_call(
        functools.partial(_post_kernel, d_model=d),
        grid=(t // tm,),
        in_specs=[pl.BlockSpec((tm, d_ssm), row),
                  pl.BlockSpec((tm, d_attn), row),
                  pl.BlockSpec((tm, d), row),
                  pl.BlockSpec((tm, d), row),
                  pl.BlockSpec((tm, d), row),
                  pl.BlockSpec((1, 1, mod3.shape[2]), lambda i: (i // tiles_per_seq, 0, 0)),
                  pl.BlockSpec((1, d), const),
                  pl.BlockSpec(wg.shape, const),
                  pl.BlockSpec((1, d_ssm), const),
                  pl.BlockSpec(wsu.shape, const),
                  pl.BlockSpec(wau.shape, const),
                  pl.BlockSpec(wo.shape, const),
                  pl.BlockSpec(wr_t.shape, const)],
        out_specs=[pl.BlockSpec((tm, d), row),
                   pl.BlockSpec((tm, d), row),
                   pl.BlockSpec((1, ne, tm), lambda i: (i // tiles_per_seq, 0, i % tiles_per_seq))],
        out_shape=[jax.ShapeDtypeStruct((t, d), F32),
                   jax.ShapeDtypeStruct((t, d), BF16),
                   jax.ShapeDtypeStruct((t // seq, ne, seq), F32)],
        name="post_mix",
        compiler_params=_cparams(("arbitrary",), 48),
    )(y, attn, gs, ga, x2, mod3, ln_w, wg, bg, wsu, wau, wo, wr_t)


def _lane_cumsum(mask_f, tri):
    ne, seq = mask_f.shape
    run = jnp.zeros((ne, 1), F32)
    parts = []
    for j in range(seq // LANES):
        blk = mask_f[:, j * LANES:(j + 1) * LANES]
        cs = jnp.dot(blk.astype(BF16), tri, preferred_element_type=F32) + run
        parts.append(cs)
        run = run + jnp.sum(blk, axis=1, keepdims=True)
    return jnp.concatenate(parts, axis=-1)


def _select_kernel(aff_ref, tri_ref, rank_ref, gate_ref, *, cap):
    aff = aff_ref[0]
    ne = aff.shape[0]
    tri = tri_ref[...]

    def body(i, thr):
        cand = thr | (jnp.int32(1) << (30 - i))
        cnt = jnp.sum(jnp.where(aff >= pltpu.bitcast(cand, F32), 1.0, 0.0), axis=1, keepdims=True)
        return jnp.where(cnt >= cap, cand, thr)

    thr = lax.fori_loop(0, 31, body, jnp.zeros((ne, 1), jnp.int32))
    gt = aff >= pltpu.bitcast(thr + 1, F32)
    eq = (aff >= pltpu.bitcast(thr, F32)) & jnp.logical_not(gt)
    need = cap - jnp.sum(jnp.where(gt, 1.0, 0.0), axis=1, keepdims=True)
    eq_f = jnp.where(eq, 1.0, 0.0)
    eq_rank = _lane_cumsum(eq_f, tri) - eq_f
    sel = gt | (eq & (eq_rank < need))
    sel_f = jnp.where(sel, 1.0, 0.0)
    rank = _lane_cumsum(sel_f, tri) - sel_f
    rank_ref[0] = jnp.where(sel, rank.astype(jnp.int32), -1)
    gate_ref[0] = jnp.where(sel, aff, 0.0)


def _select_call(aff_t, tri, *, cap):
    b, ne, seq = aff_t.shape
    spec = pl.BlockSpec((1, ne, seq), lambda i: (i, 0, 0))
    return pl.pallas_call(
        functools.partial(_select_kernel, cap=cap),
        grid=(b,),
        in_specs=[spec, pl.BlockSpec((LANES, LANES), lambda i: (0, 0))],
        out_specs=[spec, spec],
        out_shape=[jax.ShapeDtypeStruct((b, ne, seq), jnp.int32),
                   jax.ShapeDtypeStruct((b, ne, seq), F32)],
        name="expert_select",
        compiler_params=_cparams(("arbitrary",), 32),
    )(aff_t, tri)


def _gather_kernel(h_ref, rank_ref, gate_ref, xg_ref, gc_ref, *, cap, ts):
    seq = h_ref.shape[0]
    slot = lax.broadcasted_iota(jnp.int32, (cap, ts), 0)
    acc = jnp.zeros((cap, h_ref.shape[1]), F32)
    gc = jnp.zeros((cap, 1), F32)
    for j in range(seq // ts):
        hit = rank_ref[0, :, j * ts:(j + 1) * ts] == slot
        onehot = jnp.where(hit, 1.0, 0.0)
        acc = acc + jnp.dot(onehot.astype(BF16), h_ref[j * ts:(j + 1) * ts, :],
                            preferred_element_type=F32)
        gc = gc + jnp.sum(onehot * gate_ref[0, :, j * ts:(j + 1) * ts], axis=1, keepdims=True)
    xg_ref[0] = acc.astype(xg_ref.dtype)
    gc_ref[0] = gc


def _gather_call(h2, rank3, gate3, *, batch, seq, cap, ne):
    t, d = h2.shape
    ts = min(1024, seq)
    return pl.pallas_call(
        functools.partial(_gather_kernel, cap=cap, ts=ts),
        grid=(batch, ne),
        in_specs=[pl.BlockSpec((seq, d), lambda b, e: (b, 0)),
                  pl.BlockSpec((1, 1, seq), lambda b, e: (b * ne + e, 0, 0)),
                  pl.BlockSpec((1, 1, seq), lambda b, e: (b * ne + e, 0, 0))],
        out_specs=[pl.BlockSpec((1, cap, d), lambda b, e: (e, b, 0)),
                   pl.BlockSpec((1, cap, 1), lambda b, e: (e, b, 0))],
        out_shape=[jax.ShapeDtypeStruct((ne, batch * cap, d), BF16),
                   jax.ShapeDtypeStruct((ne, batch * cap, 1), F32)],
        name="expert_gather",
        compiler_params=_cparams(("arbitrary", "arbitrary"), 40),
    )(h2, rank3, gate3)


def _ffn_kernel(x_ref, gc_ref, wg_ref, wu_ref, wd_ref, ye_ref, *, rows):
    f = pl.program_id(1)

    @pl.when(f == 0)
    def _init():
        ye_ref[...] = jnp.zeros_like(ye_ref)

    wg = wg_ref[0, 0].astype(BF16)
    wu = wu_ref[0, 0].astype(BF16)
    wd = wd_ref[0, 0].astype(BF16)
    for r in range(x_ref.shape[1] // rows):
        sl = slice(r * rows, (r + 1) * rows)
        x = x_ref[0, sl, :]
        hg = jnp.dot(x, wg, preferred_element_type=F32)
        hu = jnp.dot(x, wu, preferred_element_type=F32)
        hid = (hg * jax.nn.sigmoid(hg) * hu).astype(BF16)
        ye_ref[0, sl, :] += jnp.dot(hid, wd, preferred_element_type=F32)

    @pl.when(f == pl.num_programs(1) - 1)
    def _gate():
        ye_ref[0] = ye_ref[0] * gc_ref[0]


def _ffn_call(xg, gc, wg, wu, wd, *, layer, tf, rows):
    ne, n, d = xg.shape
    dff = wg.shape[3]
    return pl.pallas_call(
        functools.partial(_ffn_kernel, rows=rows),
        grid=(ne, dff // tf),
        in_specs=[pl.BlockSpec((1, n, d), lambda e, f: (e, 0, 0)),
                  pl.BlockSpec((1, n, 1), lambda e, f: (e, 0, 0)),
                  pl.BlockSpec((1, 1, d, tf), lambda e, f: (layer, e, 0, f)),
                  pl.BlockSpec((1, 1, d, tf), lambda e, f: (layer, e, 0, f)),
                  pl.BlockSpec((1, 1, tf, d), lambda e, f: (layer, e, f, 0))],
        out_specs=pl.BlockSpec((1, n, d), lambda e, f: (e, 0, 0)),
        out_shape=jax.ShapeDtypeStruct((ne, n, d), F32),
        name="expert_ffn",
        compiler_params=_cparams(("arbitrary", "arbitrary"), 56),
    )(xg, gc, wg, wu, wd)


def _combine_kernel(ye_ref, rank_ref, x1_ref, mod_ref, o_ref, *, cap, d_model):
    e = pl.program_id(1)

    @pl.when(e == 0)
    def _init():
        o_ref[...] = jnp.zeros_like(o_ref)

    ts = o_ref.shape[0]
    slot = lax.broadcasted_iota(jnp.int32, (cap, ts), 0)
    onehot = jnp.where(rank_ref[0] == slot, 1.0, 0.0).astype(BF16)
    o_ref[...] += lax.dot_general(onehot, ye_ref[0].astype(BF16), (((0,), (0,)), ((), ())),
                                  preferred_element_type=F32)

    @pl.when(e == pl.num_programs(1) - 1)
    def _fin():
        g2 = mod_ref[0][:, 5 * d_model:6 * d_model]
        o_ref[...] = x1_ref[...] + g2 * o_ref[...]


def _combine_call(ye, rank3, x1, mod3, *, batch, seq, cap, ts):
    t, d = x1.shape
    ne = ye.shape[0]
    nt = seq // ts
    return pl.pallas_call(
        functools.partial(_combine_kernel, cap=cap, d_model=d),
        grid=(t // ts, ne),
        in_specs=[pl.BlockSpec((1, cap, d), lambda i, e: (e, i // nt, 0)),
                  pl.BlockSpec((1, 1, ts), lambda i, e: ((i // nt) * ne + e, 0, i % nt)),
                  pl.BlockSpec((ts, d), lambda i, e: (i, 0)),
                  pl.BlockSpec((1, 1, mod3.shape[2]), lambda i, e: (i // nt, 0, 0))],
        out_specs=pl.BlockSpec((ts, d), lambda i, e: (i, 0)),
        out_shape=jax.ShapeDtypeStruct((t, d), F32),
        name="moe_combine",
        compiler_params=_cparams(("arbitrary", "arbitrary"), 48),
    )(ye, rank3, x1, mod3)


def kernel(x, c, w_mod, b_mod, ln1_w, ln2_w, w_in, ssm_lam_re, ssm_lam_im, ssm_log_dt, ssm_b_re,
           ssm_b_im, ssm_c_re, ssm_c_im, ssm_d, w_glu, b_glu, q_norm_w, k_norm_w, w_ssm_up,
           w_attn_up, w_out, w_router, w_exp_gate, w_exp_up, w_exp_down):
    batch, seq, d = x.shape
    depth = w_mod.shape[0]
    t = batch * seq
    d_attn = N_Q_HEADS * HEAD_DIM
    d_kv = N_KV_HEADS * HEAD_DIM
    cap = CAPACITY_FACTOR * seq // N_EXPERTS
    tm = min(512, seq)
    tq = min(256, seq)

    tabs = _rope_tables(seq)
    head_id = jnp.arange(d_attn) // HEAD_DIM
    hm = (head_id[:, None] == head_id[None, :]).astype(BF16)
    tri = (jnp.arange(LANES)[:, None] <= jnp.arange(LANES)[None, :]).astype(BF16)
    c_pad = jnp.zeros((8, d), F32).at[:batch].set(c)
    mod_all = _mod_call(c_pad, w_mod, b_mod)[:, :batch]
    wa_all, wb_all, a_re_all, a_im_all = _s5_weights(ssm_lam_re, ssm_lam_im, ssm_log_dt, ssm_b_re,
                                                     ssm_b_im, ssm_c_re, ssm_c_im, ssm_d)

    x2 = x.reshape(t, d)
    for l in range(depth):
        mod3 = mod_all[l].reshape(batch, 1, 6 * d)
        qw = jnp.tile(q_norm_w[l], N_Q_HEADS).reshape(1, d_attn)
        kw = jnp.tile(k_norm_w[l], N_KV_HEADS).reshape(1, d_kv)
        u, q, k, v, gs, ga = _in_call(x2, mod3, ln1_w[l].reshape(1, d), w_in[l].astype(BF16), tabs,
                                      qw, kw, hm, seq=seq, tm=tm)
        y = _s5_mixer(u, wa_all[l], wb_all[l], a_re_all[l], a_im_all[l], batch=batch, seq=seq)
        attn = _attn_call(q, k, v, seq=seq, tq=tq)
        x1, h2, aff_t = _post_call(
            y, attn, gs, ga, x2, mod3, ln2_w[l].reshape(1, d), w_glu[l].astype(BF16),
            b_glu[l].reshape(1, -1), w_ssm_up[l].astype(BF16), w_attn_up[l].astype(BF16),
            w_out[l].astype(BF16), w_router[l].T, seq=seq, tm=tm)
        rank, gate = _select_call(aff_t, tri, cap=cap)
        rank3 = rank.reshape(batch * N_EXPERTS, 1, seq)
        gate3 = gate.reshape(batch * N_EXPERTS, 1, seq)
        xg, gc = _gather_call(h2, rank3, gate3, batch=batch, seq=seq, cap=cap, ne=N_EXPERTS)
        ye = _ffn_call(xg, gc, w_exp_gate, w_exp_up, w_exp_down, layer=l, tf=512,
                       rows=min(1024, batch * cap))
        x2 = _combine_call(ye, rank3, x1, mod3, batch=batch, seq=seq, cap=cap, ts=min(1024, seq))
    return x2.reshape(batch, seq, d)
```

```python
import functools
import math

import jax
import jax.numpy as jnp
from jax import lax
from jax.experimental import pallas as pl
from jax.experimental.pallas import tpu as pltpu

F32 = jnp.float32
BF16 = jnp.bfloat16

HEAD_DIM = 64
N_Q_HEADS = 8
N_KV_HEADS = 2
Q_PER_KV = N_Q_HEADS // N_KV_HEADS
ROT_HALF = HEAD_DIM // 2
ROPE_THETA = 10000.0
GRID_W = 64
SSM_GROUP = 16
SSM_STATE = 64
N_EXPERTS = 16
CAPACITY_FACTOR = 2
EPS = 1e-6

S5_CHUNK = 16
S5_PAIR = 2
LANES = 128
V7X_VMEM_BYTES = 64 * 1024 * 1024
LOG2E = 1.4426950408889634


def _cparams(semantics, vmem_mb):
    return pltpu.CompilerParams(dimension_semantics=semantics,
                                vmem_limit_bytes=int(vmem_mb * 1024 * 1024))


def _mod_kernel(c_ref, w_ref, b_ref, o_ref):
    c = c_ref[...]
    ca = c * jax.nn.sigmoid(c)
    o_ref[0] = jnp.dot(ca, w_ref[0], preferred_element_type=F32,
                       precision=lax.Precision.HIGHEST) + b_ref[0]


def _mod_call(c_pad, w_mod, b_mod):
    depth, d, n = w_mod.shape
    rows = c_pad.shape[0]
    tn = 1536
    return pl.pallas_call(
        _mod_kernel,
        grid=(depth, n // tn),
        in_specs=[pl.BlockSpec((rows, d), lambda l, j: (0, 0)),
                  pl.BlockSpec((1, d, tn), lambda l, j: (l, 0, j)),
                  pl.BlockSpec((1, 1, tn), lambda l, j: (l, 0, j))],
        out_specs=pl.BlockSpec((1, rows, tn), lambda l, j: (l, 0, j)),
        out_shape=jax.ShapeDtypeStruct((depth, rows, n), F32),
        name="adaln_mod",
        compiler_params=_cparams(("arbitrary", "arbitrary"), 32),
    )(c_pad, w_mod, b_mod.reshape(depth, 1, n))


def _head_mean_square(xsq, hm):
    hi = xsq.astype(BF16)
    lo = (xsq - hi.astype(F32)).astype(BF16)
    s = jnp.dot(hi, hm, preferred_element_type=F32) + jnp.dot(lo, hm, preferred_element_type=F32)
    return s * (1.0 / HEAD_DIM)


def _rope_block(blk, cos, sa, sb):
    return (blk * cos + pltpu.roll(blk, LANES - ROT_HALF // 2, 1) * sa
            + pltpu.roll(blk, ROT_HALF // 2, 1) * sb)


def _in_kernel(x_ref, mod_ref, ln_ref, w_ref, cos_ref, sa_ref, sb_ref, qw_ref, kw_ref, hm_ref,
               u_ref, q_ref, k_ref, v_ref, gs_ref, ga_ref, *, d_model, d_ssm, d_attn, d_kv):
    x = x_ref[...]
    mod = mod_ref[0]
    sh = mod[:, 0:d_model]
    sc = mod[:, d_model:2 * d_model]
    ms = jnp.mean(x * x, axis=-1, keepdims=True)
    h = (x * lax.rsqrt(ms + EPS) * ln_ref[...]) * (1.0 + sc) + sh
    proj = jnp.dot(h.astype(BF16), w_ref[...], preferred_element_type=F32)
    o = 0
    u_ref[...] = proj[:, o:o + d_ssm].astype(u_ref.dtype)
    o += d_ssm
    q = proj[:, o:o + d_attn]
    o += d_attn
    k = proj[:, o:o + d_kv]
    o += d_kv
    v_ref[0] = proj[:, o:o + d_kv].T.astype(v_ref.dtype)
    o += d_kv
    gs_ref[...] = jax.nn.sigmoid(proj[:, o:o + d_model]).astype(gs_ref.dtype)
    o += d_model
    ga_ref[...] = jax.nn.sigmoid(proj[:, o:o + d_model]).astype(ga_ref.dtype)

    hm = hm_ref[...]
    cos = cos_ref[...]
    sa = sa_ref[...]
    sb = sb_ref[...]
    qn = q * lax.rsqrt(_head_mean_square(q * q, hm) + EPS) * qw_ref[...]
    kn = k * lax.rsqrt(_head_mean_square(k * k, hm[:d_kv, :d_kv]) + EPS) * kw_ref[...]
    qscale = (HEAD_DIM ** -0.5) * LOG2E
    qr = [_rope_block(qn[:, j * LANES:(j + 1) * LANES], cos, sa, sb) for j in range(d_attn // LANES)]
    q_ref[...] = (jnp.concatenate(qr, axis=-1) * qscale).astype(q_ref.dtype)
    kr = [_rope_block(kn[:, j * LANES:(j + 1) * LANES], cos, sa, sb) for j in range(d_kv // LANES)]
    k_ref[...] = jnp.concatenate(kr, axis=-1).astype(k_ref.dtype)


def _in_call(x2, mod3, ln_w, w_in, tabs, qw, kw, hm, *, seq, tm):
    t, d = x2.shape
    n_in = w_in.shape[1]
    d_ssm = d // 2
    d_attn = N_Q_HEADS * HEAD_DIM
    d_kv = N_KV_HEADS * HEAD_DIM
    tiles_per_seq = seq // tm
    cos, sa, sb = tabs
    kern = functools.partial(_in_kernel, d_model=d, d_ssm=d_ssm, d_attn=d_attn, d_kv=d_kv)
    row = lambda i: (i, 0)
    const = lambda i: (0, 0)
    tab = lambda i: (i % tiles_per_seq, 0)
    return pl.pallas_call(
        kern,
        grid=(t // tm,),
        in_specs=[pl.BlockSpec((tm, d), row),
                  pl.BlockSpec((1, 1, mod3.shape[2]), lambda i: (i // tiles_per_seq, 0, 0)),
                  pl.BlockSpec((1, d), const),
                  pl.BlockSpec((d, n_in), const),
                  pl.BlockSpec((tm, LANES), tab),
                  pl.BlockSpec((tm, LANES), tab),
                  pl.BlockSpec((tm, LANES), tab),
                  pl.BlockSpec((1, d_attn), const),
                  pl.BlockSpec((1, d_kv), const),
                  pl.BlockSpec((d_attn, d_attn), const)],
        out_specs=[pl.BlockSpec((tm, d_ssm), row),
                   pl.BlockSpec((tm, d_attn), row),
                   pl.BlockSpec((tm, d_kv), row),
                   pl.BlockSpec((1, d_kv, tm), lambda i: (i // tiles_per_seq, 0, i % tiles_per_seq)),
                   pl.BlockSpec((tm, d), row),
                   pl.BlockSpec((tm, d), row)],
        out_shape=[jax.ShapeDtypeStruct((t, d_ssm), F32),
                   jax.ShapeDtypeStruct((t, d_attn), BF16),
                   jax.ShapeDtypeStruct((t, d_kv), BF16),
                   jax.ShapeDtypeStruct((t // seq, d_kv, seq), BF16),
                   jax.ShapeDtypeStruct((t, d), BF16),
                   jax.ShapeDtypeStruct((t, d), BF16)],
        name="in_proj",
        compiler_params=_cparams(("arbitrary",), 48),
    )(x2, mod3, ln_w, w_in, cos, sa, sb, qw, kw, hm)


def _rope_tables(seq):
    pos = jnp.arange(seq)
    row = (pos // GRID_W).astype(F32)
    col = (pos % GRID_W).astype(F32)
    inv = 1.0 / (ROPE_THETA ** (jnp.arange(0, ROT_HALF, 2, dtype=F32) / ROT_HALF))
    ang_r = row[:, None] * inv[None, :]
    ang_c = col[:, None] * inv[None, :]
    zeros = jnp.zeros_like(ang_r)
    cos_h = jnp.concatenate([jnp.cos(ang_r), jnp.cos(ang_r), jnp.cos(ang_c), jnp.cos(ang_c)], axis=-1)
    sa_h = jnp.concatenate([-jnp.sin(ang_r), zeros, -jnp.sin(ang_c), zeros], axis=-1)
    sb_h = jnp.concatenate([zeros, jnp.sin(ang_r), zeros, jnp.sin(ang_c)], axis=-1)
    rep = LANES // HEAD_DIM
    return tuple(jnp.tile(a, (1, rep)) for a in (cos_h, sa_h, sb_h))


def _attn_heads_out(o_t, tq):
    return [o_t[:, g * tq:(g + 1) * tq].T for g in range(Q_PER_KV)]


def _attn_kernel(q_ref, k_ref, vt_ref, o_ref, *, kb):
    tq = q_ref.shape[0]
    seq = k_ref.shape[0]
    nblk = seq // kb
    width = Q_PER_KV * HEAD_DIM
    q_ts = []
    for kvh in range(N_KV_HEADS):
        qt = q_ref[:, kvh * width:(kvh + 1) * width].astype(F32).T.astype(BF16)
        q_ts.append(jnp.concatenate([qt[g * HEAD_DIM:(g + 1) * HEAD_DIM] for g in range(Q_PER_KV)],
                                    axis=1))

    def scores(kvh, j):
        lo = kvh * HEAD_DIM
        return jnp.dot(k_ref[j * kb:(j + 1) * kb, lo:lo + HEAD_DIM], q_ts[kvh],
                       preferred_element_type=F32)

    def weighted_v(kvh, j, p):
        lo = kvh * HEAD_DIM
        return jnp.dot(vt_ref[0, lo:lo + HEAD_DIM, j * kb:(j + 1) * kb], p.astype(BF16),
                       preferred_element_type=F32)

    outs = []
    worst = jnp.zeros((1, 1), F32)
    for kvh in range(N_KV_HEADS):
        s0 = scores(kvh, 0)
        m0 = jnp.max(s0, axis=0, keepdims=True)
        l = jnp.zeros_like(m0)
        acc = jnp.zeros((HEAD_DIM, m0.shape[1]), F32)
        s_cur = s0
        for j in range(nblk):
            s_next = scores(kvh, j + 1) if j + 1 < nblk else None
            p = jnp.exp2(s_cur - m0)
            l = l + jnp.sum(p, axis=0, keepdims=True)
            acc = acc + weighted_v(kvh, j, p)
            s_cur = s_next
        worst = jnp.maximum(worst, jnp.max(l, axis=1, keepdims=True))
        outs.extend(_attn_heads_out(acc / l, tq))
    o_ref[...] = jnp.concatenate(outs, axis=-1).astype(o_ref.dtype)

    @pl.when(worst[0, 0] > jnp.finfo(F32).max)
    def _exact():
        outs = []
        for kvh in range(N_KV_HEADS):
            m = jnp.full((1, Q_PER_KV * tq), -jnp.inf, F32)
            l = jnp.zeros_like(m)
            acc = jnp.zeros((HEAD_DIM, Q_PER_KV * tq), F32)
            for j in range(nblk):
                s = scores(kvh, j)
                m_new = jnp.maximum(m, jnp.max(s, axis=0, keepdims=True))
                alpha = jnp.exp2(m - m_new)
                p = jnp.exp2(s - m_new)
                l = l * alpha + jnp.sum(p, axis=0, keepdims=True)
                acc = acc * alpha + weighted_v(kvh, j, p)
                m = m_new
            outs.extend(_attn_heads_out(acc / l, tq))
        o_ref[...] = jnp.concatenate(outs, axis=-1).astype(o_ref.dtype)


def _attn_call(q, k, vt, *, seq, tq):
    t, d_attn = q.shape
    d_kv = k.shape[1]
    nq = seq // tq
    return pl.pallas_call(
        functools.partial(_attn_kernel, kb=min(256, seq)),
        grid=(t // seq, nq),
        in_specs=[pl.BlockSpec((tq, d_attn), lambda b, i: (b * nq + i, 0)),
                  pl.BlockSpec((seq, d_kv), lambda b, i: (b, 0)),
                  pl.BlockSpec((1, d_kv, seq), lambda b, i: (b, 0, 0))],
        out_specs=pl.BlockSpec((tq, d_attn), lambda b, i: (b * nq + i, 0)),
        out_shape=jax.ShapeDtypeStruct((t, d_attn), BF16),
        name="attention",
        compiler_params=_cparams(("arbitrary", "arbitrary"), 48),
    )(q, k, vt)


def _s5_weights(lam_re, lam_im, log_dt, b_re, b_im, c_re, c_im, d_skip):
    hp = lax.Precision.HIGHEST
    nl, _, ng, p = lam_re.shape
    cg = SSM_GROUP
    L = S5_CHUNK
    kd = L * cg
    dt = jnp.exp(log_dt)[..., None]
    zr = lam_re * dt
    zi = lam_im * dt

    def cpow(e):
        mag = jnp.exp(zr[..., None] * e)
        return mag * jnp.cos(zi[..., None] * e), mag * jnp.sin(zi[..., None] * e)

    lb_re, lb_im = jnp.exp(zr) * jnp.cos(zi), jnp.exp(zr) * jnp.sin(zi)
    nr, ni = lb_re - 1.0, lb_im
    den = lam_re * lam_re + lam_im * lam_im
    f_re = ((nr * lam_re + ni * lam_im) / den)[..., None]
    f_im = ((ni * lam_re - nr * lam_im) / den)[..., None]
    bb_re = f_re * b_re - f_im * b_im
    bb_im = f_re * b_im + f_im * b_re

    m_idx = jnp.arange(L + 1, dtype=F32)
    expo = jnp.stack([m_idx, L - m_idx])[None, :, None, None, :]
    pw_re, pw_im = cpow(expo)
    rep = jnp.repeat(jnp.eye(L + 1, dtype=F32), cg, axis=1)
    til = jnp.tile(jnp.eye(cg, dtype=F32), (1, L + 1))
    pwx_re = jnp.einsum('dxgpm,mn->dxgpn', pw_re, rep, precision=hp)
    pwx_im = jnp.einsum('dxgpm,mn->dxgpn', pw_im, rep, precision=hp)
    ct_re = jnp.einsum('dxgcp,cn->dxgpn', c_re, til, precision=hp)
    ct_im = jnp.einsum('dxgcp,cn->dxgpn', c_im, til, precision=hp)
    ca_re = ct_re * pwx_re - ct_im * pwx_im
    ca_im = ct_re * pwx_im + ct_im * pwx_re
    ef_re, ef_im = ca_re[:, 0, ..., :kd], ca_im[:, 0, ..., :kd]
    w2f_re, w2f_im = ca_re[:, 0, ..., cg:], ca_im[:, 0, ..., cg:]
    w2b_re, w2b_im = ca_re[:, 1, ..., :kd], ca_im[:, 1, ..., :kd]
    eb_re, eb_im = ca_re[:, 1, ..., cg:], ca_im[:, 1, ..., cg:]

    def lag_rows(x, e_re, e_im):
        return (jnp.einsum('dgpe,dgpn->dgen', bb_re[:, x], e_re, precision=hp)
                - jnp.einsum('dgpe,dgpn->dgen', bb_im[:, x], e_im, precision=hp))

    kf = lag_rows(0, ef_re, ef_im)
    kb = lag_rows(1, eb_re, eb_im)
    zeros = jnp.zeros_like(kf)
    pf = jnp.concatenate([zeros, kf], axis=-1)
    pb = jnp.concatenate([kb, zeros], axis=-1)
    tf = jnp.stack([pf[..., kd - cg * s:2 * kd - cg * s] for s in range(L)], axis=2)
    tb = jnp.stack([pb[..., cg * (L - 1 - s):cg * (L - 1 - s) + kd] for s in range(L)], axis=2)
    skip = jnp.eye(kd, dtype=F32) * jnp.tile(d_skip.reshape(nl, ng, 1, cg), (1, 1, 1, L))
    tmat = (tf + tb).reshape(nl, ng, kd, kd) + skip

    s_idx = jnp.arange(L, dtype=F32)
    sexp = jnp.stack([L - 1 - s_idx, s_idx])[None, :, None, None, :]
    ps_re, ps_im = cpow(sexp)
    w1_re = (jnp.einsum('dxgps,dxgpe->dxgsep', ps_re, bb_re)
             - jnp.einsum('dxgps,dxgpe->dxgsep', ps_im, bb_im)).reshape(nl, 2, ng, kd, p)
    w1_im = (jnp.einsum('dxgps,dxgpe->dxgsep', ps_re, bb_im)
             + jnp.einsum('dxgps,dxgpe->dxgsep', ps_im, bb_re)).reshape(nl, 2, ng, kd, p)

    nq = ng // S5_PAIR
    eye2 = jnp.eye(S5_PAIR, dtype=F32)

    def pair_diag(mat):
        r, c = mat.shape[2], mat.shape[3]
        mat = mat.reshape(nl, nq, S5_PAIR, r, c)
        return jnp.einsum('dqgrc,gh->dqgrhc', mat, eye2).reshape(nl, nq, S5_PAIR * r, S5_PAIR * c)

    wa = jnp.concatenate([pair_diag(tmat), pair_diag(w1_re[:, 0]), pair_diag(w1_im[:, 0]),
                          pair_diag(w1_re[:, 1]), pair_diag(w1_im[:, 1])], axis=-1)
    wb = jnp.concatenate([pair_diag(w2f_re), pair_diag(-w2f_im),
                          pair_diag(w2b_re), pair_diag(-w2b_im)], axis=2)
    al_re, al_im = cpow(jnp.float32(L))
    a_re = al_re.reshape(nl, 2, ng * p)
    a_im = al_im.reshape(nl, 2, ng * p)
    return wa.astype(BF16), wb.astype(BF16), a_re, a_im


GROUPS_PER_VREG = LANES // SSM_GROUP
PAIRS_PER_VREG = GROUPS_PER_VREG // S5_PAIR
TOKENS_PER_HALF = LANES // SSM_GROUP


def _granule_transpose(arrs):
    gran = lax.broadcasted_iota(jnp.int32, arrs[0].shape, 1) // SSM_GROUP
    cur = list(arrs)
    for dist in (4, 2, 1):
        keep = (gran & dist) == 0
        nxt = list(cur)
        for i in range(len(cur)):
            if i & dist:
                continue
            a, b = cur[i], cur[i + dist]
            nxt[i] = jnp.where(keep, a, pltpu.roll(b, dist * SSM_GROUP, 1))
            nxt[i + dist] = jnp.where(keep, pltpu.roll(a, LANES - dist * SSM_GROUP, 1), b)
        cur = nxt
    return cur


def _s5_chunk_kernel(u_ref, wa_ref, yi_ref, sfr_ref, sfi_ref, sbr_ref, sbi_ref, *, nchunk):
    halves = []
    for jh in range(S5_CHUNK // TOKENS_PER_HALF):
        toks = [u_ref[pl.ds(jh * TOKENS_PER_HALF + jj, nchunk, stride=S5_CHUNK), :]
                for jj in range(TOKENS_PER_HALF)]
        halves.append([o.astype(BF16) for o in _granule_transpose(toks)])
    kd2 = S5_PAIR * S5_CHUNK * SSM_GROUP
    for qq in range(PAIRS_PER_VREG):
        z = jnp.concatenate([halves[jh][S5_PAIR * qq + gl] for gl in range(S5_PAIR)
                             for jh in range(len(halves))], axis=-1)
        res = jnp.dot(z, wa_ref[qq], preferred_element_type=F32)
        lo = qq * LANES
        yi_ref[0, qq] = res[:, :kd2]
        sfr_ref[:, lo:lo + LANES] = res[:, kd2:kd2 + LANES]
        sfi_ref[:, lo:lo + LANES] = res[:, kd2 + LANES:kd2 + 2 * LANES]
        sbr_ref[:, lo:lo + LANES] = res[:, kd2 + 2 * LANES:kd2 + 3 * LANES]
        sbi_ref[:, lo:lo + LANES] = res[:, kd2 + 3 * LANES:kd2 + 4 * LANES]


def _s5_chunk_call(u, wa, *, batch, seq):
    t, d_ssm = u.shape
    nchunk = seq // S5_CHUNK
    nv = d_ssm // LANES
    nq, kd2, ncol = wa.shape
    sw = PAIRS_PER_VREG * LANES
    st = jax.ShapeDtypeStruct((nchunk, batch * nv * sw), F32)
    sspec = pl.BlockSpec((nchunk, sw), lambda b, v: (0, b * nv + v))
    return pl.pallas_call(
        functools.partial(_s5_chunk_kernel, nchunk=nchunk),
        grid=(batch, nv),
        in_specs=[pl.BlockSpec((seq, LANES), lambda b, v: (b, v)),
                  pl.BlockSpec((PAIRS_PER_VREG, kd2, ncol), lambda b, v: (v, 0, 0))],
        out_specs=[pl.BlockSpec((1, PAIRS_PER_VREG, nchunk, kd2), lambda b, v: (b, v, 0, 0)),
                   sspec, sspec, sspec, sspec],
        out_shape=[jax.ShapeDtypeStruct((batch, nq, nchunk, kd2), F32), st, st, st, st],
        name="s5_chunk",
        compiler_params=_cparams(("arbitrary", "arbitrary"), 40),
    )(u, wa)


def _s5scan_kernel(sfr_ref, sfi_ref, sbr_ref, sbi_ref, afr_ref, afi_ref, abr_ref, abi_ref,
                   xfr_ref, xfi_ref, xbr_ref, xbi_ref, *, nchunk):
    afr, afi = afr_ref[...], afi_ref[...]
    abr, abi = abr_ref[...], abi_ref[...]
    zero = jnp.zeros_like(afr)

    def body(k, carry):
        fr, fi, br, bi = carry
        kb = nchunk - 1 - k
        rowf = pl.ds(k, 1)
        rowb = pl.ds(kb, 1)
        xfr_ref[rowf, :] = fr
        xfi_ref[rowf, :] = fi
        xbr_ref[rowb, :] = br
        xbi_ref[rowb, :] = bi
        nfr = fr * afr - fi * afi + sfr_ref[rowf, :]
        nfi = fr * afi + fi * afr + sfi_ref[rowf, :]
        nbr = br * abr - bi * abi + sbr_ref[rowb, :]
        nbi = br * abi + bi * abr + sbi_ref[rowb, :]
        return nfr, nfi, nbr, nbi

    lax.fori_loop(0, nchunk, body, (zero, zero, zero, zero))


def _s5scan_call(s4, a4, *, nchunk, tl):
    w = s4[0].shape[1]
    sspec = pl.BlockSpec((nchunk, tl), lambda j: (0, j))
    aspec = pl.BlockSpec((1, tl), lambda j: (0, j))
    st = jax.ShapeDtypeStruct((nchunk, w), F32)
    return pl.pallas_call(
        functools.partial(_s5scan_kernel, nchunk=nchunk),
        grid=(w // tl,),
        in_specs=[sspec] * 4 + [aspec] * 4,
        out_specs=[sspec] * 4,
        out_shape=[st] * 4,
        name="s5_scan",
        compiler_params=_cparams(("arbitrary",), 32),
    )(*s4, *a4)


def _s5_readout_kernel(yi_ref, xfr_ref, xfi_ref, xbr_ref, xbi_ref, wb_ref, y_ref, *, nchunk):
    nhalf = S5_CHUNK // TOKENS_PER_HALF
    pieces = [[None] * GROUPS_PER_VREG for _ in range(nhalf)]
    for qq in range(PAIRS_PER_VREG):
        lo = qq * LANES
        xs = jnp.concatenate([xfr_ref[:, lo:lo + LANES], xfi_ref[:, lo:lo + LANES],
                              xbr_ref[:, lo:lo + LANES], xbi_ref[:, lo:lo + LANES]], axis=-1)
        y = yi_ref[0, qq] + jnp.dot(xs.astype(BF16), wb_ref[qq], preferred_element_type=F32)
        for gl in range(S5_PAIR):
            for jh in range(nhalf):
                c0 = (gl * nhalf + jh) * LANES
                pieces[jh][S5_PAIR * qq + gl] = y[:, c0:c0 + LANES]
    for jh in range(nhalf):
        toks = _granule_transpose(pieces[jh])
        for jj in range(TOKENS_PER_HALF):
            y_ref[pl.ds(jh * TOKENS_PER_HALF + jj, nchunk, stride=S5_CHUNK), :] = toks[jj]


def _s5_readout_call(yi, x4, wb, *, batch, seq):
    _, nq, nchunk, kd2 = yi.shape
    nv = nq // PAIRS_PER_VREG
    sw = PAIRS_PER_VREG * LANES
    xspec = pl.BlockSpec((nchunk, sw), lambda b, v: (0, b * nv + v))
    return pl.pallas_call(
        functools.partial(_s5_readout_kernel, nchunk=nchunk),
        grid=(batch, nv),
        in_specs=[pl.BlockSpec((1, PAIRS_PER_VREG, nchunk, kd2), lambda b, v: (b, v, 0, 0)),
                  xspec, xspec, xspec, xspec,
                  pl.BlockSpec((PAIRS_PER_VREG, wb.shape[1], kd2), lambda b, v: (v, 0, 0))],
        out_specs=pl.BlockSpec((seq, LANES), lambda b, v: (b, v)),
        out_shape=jax.ShapeDtypeStruct((batch * seq, nv * LANES), F32),
        name="s5_readout",
        compiler_params=_cparams(("arbitrary", "arbitrary"), 40),
    )(yi, *x4, wb)


def _s5_mixer(u, wa, wb, a_re, a_im, *, batch, seq):
    nchunk = seq // S5_CHUNK
    yi, sfr, sfi, sbr, sbi = _s5_chunk_call(u, wa, batch=batch, seq=seq)
    w = a_re.shape[1]
    a4 = [jnp.tile(a.reshape(1, w), (1, batch)) for a in (a_re[0], a_im[0], a_re[1], a_im[1])]
    tl = min(1024, batch * w)
    x4 = _s5scan_call([sfr, sfi, sbr, sbi], a4, nchunk=nchunk, tl=tl)
    return _s5_readout_call(yi, x4, wb, batch=batch, seq=seq)


def _post_kernel(y_ref, at_ref, gs_ref, ga_ref, x_ref, mod_ref, ln_ref,
                 wg_ref, bg_ref, wsu_ref, wau_ref, wo_ref, wr_ref,
                 x1_ref, h2_ref, aff_ref, afft_ref, *, d_model):
    z = jax.nn.gelu(y_ref[...], approximate=True)
    glu = jnp.dot(z.astype(BF16), wg_ref[...], preferred_element_type=F32) + bg_ref[...]
    ssm = z * jax.nn.sigmoid(glu)
    su = jnp.dot(ssm.astype(BF16), wsu_ref[...], preferred_element_type=F32)
    au = jnp.dot(at_ref[...], wau_ref[...], preferred_element_type=F32)
    merged = gs_ref[...].astype(F32) * su + ga_ref[...].astype(F32) * au
    mix = jnp.dot(merged.astype(BF16), wo_ref[...], preferred_element_type=F32)
    mod = mod_ref[0]
    g1 = mod[:, 2 * d_model:3 * d_model]
    sh2 = mod[:, 3 * d_model:4 * d_model]
    sc2 = mod[:, 4 * d_model:5 * d_model]
    x1 = x_ref[...] + g1 * mix
    x1_ref[...] = x1
    ms = jnp.mean(x1 * x1, axis=-1, keepdims=True)
    h2 = (x1 * lax.rsqrt(ms + EPS) * ln_ref[...]) * (1.0 + sc2) + sh2
    h2_ref[...] = h2.astype(h2_ref.dtype)
    lg = lax.dot_general(wr_ref[...], h2, (((1,), (1,)), ((), ())), preferred_element_type=F32,
                         precision=lax.Precision.HIGHEST)
    lg = lg - jnp.max(lg, axis=0, keepdims=True)
    ex = jnp.exp(lg)
    aff = ex / jnp.sum(ex, axis=0, keepdims=True)
    aff_ref[0] = aff
    pad = jnp.zeros((LANES - aff.shape[0], aff.shape[1]), F32)
    afft_ref[...] = jnp.concatenate([aff, pad], axis=0).T


def _post_call(y, attn, gs, ga, x2, mod3, ln_w, wg, bg, wsu, wau, wo, wr_t, *, seq, tm):
    t, d = x2.shape
    d_ssm = y.shape[1]
    d_attn = attn.shape[1]
    ne = wr_t.shape[0]
    tiles_per_seq = seq // tm
    row = lambda i: (i, 0)
    const = lambda i: (0, 0)
    return pl.pallas_call(
        functools.partial(_post_kernel, d_model=d),
        grid=(t // tm,),
        in_specs=[pl.BlockSpec((tm, d_ssm), row),
                  pl.BlockSpec((tm, d_attn), row),
                  pl.BlockSpec((tm, d), row),
                  pl.BlockSpec((tm, d), row),
                  pl.BlockSpec((tm, d), row),
                  pl.BlockSpec((1, 1, mod3.shape[2]), lambda i: (i // tiles_per_seq, 0, 0)),
                  pl.BlockSpec((1, d), const),
                  pl.BlockSpec(wg.shape, const),
                  pl.BlockSpec((1, d_ssm), const),
                  pl.BlockSpec(wsu.shape, const),
                  pl.BlockSpec(wau.shape, const),
                  pl.BlockSpec(wo.shape, const),
                  pl.BlockSpec(wr_t.shape, const)],
        out_specs=[pl.BlockSpec((tm, d), row),
                   pl.BlockSpec((tm, d), row),
                   pl.BlockSpec((1, ne, tm), lambda i: (i // tiles_per_seq, 0, i % tiles_per_seq)),
                   pl.BlockSpec((tm, LANES), row)],
        out_shape=[jax.ShapeDtypeStruct((t, d), F32),
                   jax.ShapeDtypeStruct((t, d), F32),
                   jax.ShapeDtypeStruct((t // seq, ne, seq), F32),
                   jax.ShapeDtypeStruct((t, LANES), F32)],
        name="post_mix",
        compiler_params=_cparams(("arbitrary",), 48),
    )(y, attn, gs, ga, x2, mod3, ln_w, wg, bg, wsu, wau, wo, wr_t)


def _lane_cumsum(mask_f, tri):
    ne, seq = mask_f.shape
    run = jnp.zeros((ne, 1), F32)
    parts = []
    for j in range(seq // LANES):
        blk = mask_f[:, j * LANES:(j + 1) * LANES]
        cs = jnp.dot(blk.astype(BF16), tri, preferred_element_type=F32) + run
        parts.append(cs)
        run = run + jnp.sum(blk, axis=1, keepdims=True)
    return jnp.concatenate(parts, axis=-1)


def _select_kernel(aff_ref, tri_ref, idx_ref, half_ref, *, cap, seq, ts):
    aff = aff_ref[0]
    ne = aff.shape[0]
    tri = tri_ref[...]

    def body(i, thr):
        cand = thr | (jnp.int32(1) << (30 - i))
        cnt = jnp.sum(jnp.where(aff >= pltpu.bitcast(cand, F32), 1.0, 0.0), axis=1, keepdims=True)
        return jnp.where(cnt >= cap, cand, thr)

    thr = lax.fori_loop(0, 31, body, jnp.zeros((ne, 1), jnp.int32))
    gt = aff >= pltpu.bitcast(thr + 1, F32)
    eq = (aff >= pltpu.bitcast(thr, F32)) & jnp.logical_not(gt)
    need = cap - jnp.sum(jnp.where(gt, 1.0, 0.0), axis=1, keepdims=True)
    eq_f = jnp.where(eq, 1.0, 0.0)
    eq_rank = _lane_cumsum(eq_f, tri) - eq_f
    sel = gt | (eq & (eq_rank < need))
    sel_f = jnp.where(sel, 1.0, 0.0)
    csum = _lane_cumsum(sel_f, tri)
    rank = jnp.where(sel, csum - sel_f, -1.0)
    half_ref[0] = csum[:, seq // 2 - 1:seq // 2].astype(jnp.int32)

    slot = lax.broadcasted_iota(jnp.int32, (cap, ts), 0).astype(F32)
    lane_e = lax.broadcasted_iota(jnp.int32, (cap, ne), 1)
    idx = jnp.zeros((cap, ne), F32)
    for e in range(ne):
        col = jnp.zeros((cap, 1), F32)
        for j in range(seq // ts):
            tok = (lax.broadcasted_iota(jnp.int32, (1, ts), 1) + j * ts).astype(F32)
            hit = rank[e:e + 1, j * ts:(j + 1) * ts] == slot
            col = col + jnp.sum(jnp.where(hit, tok, 0.0), axis=1, keepdims=True)
        idx = jnp.where(lane_e == e, col, idx)
    idx_ref[0] = idx.astype(jnp.int32)


def _select_call(aff_t, tri, *, cap):
    b, ne, seq = aff_t.shape
    return pl.pallas_call(
        functools.partial(_select_kernel, cap=cap, seq=seq, ts=min(1024, seq)),
        grid=(b,),
        in_specs=[pl.BlockSpec((1, ne, seq), lambda i: (i, 0, 0)),
                  pl.BlockSpec((LANES, LANES), lambda i: (0, 0))],
        out_specs=[pl.BlockSpec((1, cap, ne), lambda i: (i, 0, 0)),
                   pl.BlockSpec((1, ne, 1), lambda i: (i, 0, 0))],
        out_shape=[jax.ShapeDtypeStruct((b, cap, ne), jnp.int32),
                   jax.ShapeDtypeStruct((b, ne, 1), jnp.int32)],
        name="expert_select",
        compiler_params=_cparams(("arbitrary",), 32),
    )(aff_t, tri)


GATHER_UNROLL = 8


def _gather_kernel(idx_ref, h_ref, aff_ref, xg_ref, gc_ref, rows_ref, arow_ref, *, cap, ne):
    b = pl.program_id(0)
    e = pl.program_id(1)
    base = (b * ne + e) * cap

    def body(g, carry):
        for k in range(GATHER_UNROLL):
            c = g * GATHER_UNROLL + k
            r = idx_ref[base + c]
            rows_ref[pl.ds(c, 1), :] = h_ref[pl.ds(r, 1), :]
            arow_ref[pl.ds(c, 1), :] = aff_ref[pl.ds(r, 1), :]
        return carry

    lax.fori_loop(0, cap // GATHER_UNROLL, body, 0)
    xg_ref[0] = rows_ref[...].astype(xg_ref.dtype)
    lane = lax.broadcasted_iota(jnp.int32, arow_ref.shape, 1)
    gc_ref[0] = jnp.sum(jnp.where(lane == e, arow_ref[...], 0.0), axis=1, keepdims=True)


def _gather_call(idx_flat, h2, aff_tm, *, batch, seq, cap, ne):
    t, d = h2.shape
    grid_spec = pltpu.PrefetchScalarGridSpec(
        num_scalar_prefetch=1,
        grid=(batch, ne),
        in_specs=[pl.BlockSpec((seq, d), lambda b, e, idx: (b, 0)),
                  pl.BlockSpec((seq, LANES), lambda b, e, idx: (b, 0))],
        out_specs=[pl.BlockSpec((1, cap, d), lambda b, e, idx: (e, b, 0)),
                   pl.BlockSpec((1, cap, 1), lambda b, e, idx: (e, b, 0))],
        scratch_shapes=[pltpu.VMEM((cap, d), F32), pltpu.VMEM((cap, LANES), F32)],
    )
    return pl.pallas_call(
        functools.partial(_gather_kernel, cap=cap, ne=ne),
        grid_spec=grid_spec,
        out_shape=[jax.ShapeDtypeStruct((ne, batch * cap, d), BF16),
                   jax.ShapeDtypeStruct((ne, batch * cap, 1), F32)],
        name="expert_gather",
        compiler_params=_cparams(("arbitrary", "arbitrary"), 48),
    )(idx_flat, h2, aff_tm)


def _ffn_kernel(x_ref, gc_ref, wg_ref, wu_ref, wd_ref, ye_ref, *, rows):
    f = pl.program_id(1)

    @pl.when(f == 0)
    def _init():
        ye_ref[...] = jnp.zeros_like(ye_ref)

    wg = wg_ref[0, 0].astype(BF16)
    wu = wu_ref[0, 0].astype(BF16)
    wd = wd_ref[0, 0].astype(BF16)
    for r in range(x_ref.shape[1] // rows):
        sl = slice(r * rows, (r + 1) * rows)
        x = x_ref[0, sl, :]
        hg = jnp.dot(x, wg, preferred_element_type=F32)
        hu = jnp.dot(x, wu, preferred_element_type=F32)
        hid = (hg * jax.nn.sigmoid(hg) * hu).astype(BF16)
        ye_ref[0, sl, :] += jnp.dot(hid, wd, preferred_element_type=F32)

    @pl.when(f == pl.num_programs(1) - 1)
    def _gate():
        ye_ref[0] = ye_ref[0] * gc_ref[0]


def _ffn_call(xg, gc, wg, wu, wd, *, layer, tf, rows):
    ne, n, d = xg.shape
    dff = wg.shape[3]
    return pl.pallas_call(
        functools.partial(_ffn_kernel, rows=rows),
        grid=(ne, dff // tf),
        in_specs=[pl.BlockSpec((1, n, d), lambda e, f: (e, 0, 0)),
                  pl.BlockSpec((1, n, 1), lambda e, f: (e, 0, 0)),
                  pl.BlockSpec((1, 1, d, tf), lambda e, f: (layer, e, 0, f)),
                  pl.BlockSpec((1, 1, d, tf), lambda e, f: (layer, e, 0, f)),
                  pl.BlockSpec((1, 1, tf, d), lambda e, f: (layer, e, f, 0))],
        out_specs=pl.BlockSpec((1, n, d), lambda e, f: (e, 0, 0)),
        out_shape=jax.ShapeDtypeStruct((ne, n, d), F32),
        name="expert_ffn",
        compiler_params=_cparams(("arbitrary", "arbitrary"), 56),
    )(xg, gc, wg, wu, wd)


SCATTER_UNROLL = 8
SEQ_HALVES = 2


def _combine_kernel(idx_ref, half_ref, ye_ref, x1_ref, mod_ref, o_ref, acc_ref, *, cap, ne, rows,
                    d_model):
    i = pl.program_id(0)
    e = pl.program_id(1)
    b = i // SEQ_HALVES
    hh = i % SEQ_HALVES

    @pl.when(e == 0)
    def _init():
        o_ref[...] = jnp.zeros_like(o_ref)
        acc_ref[...] = jnp.zeros_like(acc_ref)

    mid = half_ref[b * ne + e]
    lo = jnp.where(hh == 0, 0, mid)
    hi = jnp.where(hh == 0, mid, cap)
    base = (b * ne + e) * cap
    off = hh * rows

    def add_row(c, dst_ref):
        r = idx_ref[base + c] - off
        dst_ref[pl.ds(r, 1), :] += ye_ref[0, pl.ds(c, 1), :]

    def group(g, carry):
        for k in range(SCATTER_UNROLL):
            add_row(lo + g * SCATTER_UNROLL + k, o_ref if k % 2 == 0 else acc_ref)
        return carry

    ngroup = (hi - lo) // SCATTER_UNROLL
    lax.fori_loop(0, ngroup, group, 0)

    def tail(c, carry):
        add_row(c, o_ref)
        return carry

    lax.fori_loop(lo + ngroup * SCATTER_UNROLL, hi, tail, 0)

    @pl.when(e == pl.num_programs(1) - 1)
    def _fin():
        g2 = mod_ref[0][:, 5 * d_model:6 * d_model]
        o_ref[...] = x1_ref[...] + g2 * (o_ref[...] + acc_ref[...])


def _combine_call(idx_flat, half_flat, ye, x1, mod3, *, batch, seq, cap):
    t, d = x1.shape
    ne = ye.shape[0]
    rows = seq // SEQ_HALVES
    grid_spec = pltpu.PrefetchScalarGridSpec(
        num_scalar_prefetch=2,
        grid=(batch * SEQ_HALVES, ne),
        in_specs=[pl.BlockSpec((1, cap, d), lambda i, e, idx, half: (e, i // SEQ_HALVES, 0)),
                  pl.BlockSpec((rows, d), lambda i, e, idx, half: (i, 0)),
                  pl.BlockSpec((1, 1, mod3.shape[2]), lambda i, e, idx, half: (i // SEQ_HALVES, 0, 0))],
        out_specs=pl.BlockSpec((rows, d), lambda i, e, idx, half: (i, 0)),
        scratch_shapes=[pltpu.VMEM((rows, d), F32)],
    )
    return pl.pallas_call(
        functools.partial(_combine_kernel, cap=cap, ne=ne, rows=rows, d_model=d),
        grid_spec=grid_spec,
        out_shape=jax.ShapeDtypeStruct((t, d), F32),
        name="moe_combine",
        compiler_params=_cparams(("arbitrary", "arbitrary"), 52),
    )(idx_flat, half_flat, ye, x1, mod3)


def kernel(x, c, w_mod, b_mod, ln1_w, ln2_w, w_in, ssm_lam_re, ssm_lam_im, ssm_log_dt, ssm_b_re,
           ssm_b_im, ssm_c_re, ssm_c_im, ssm_d, w_glu, b_glu, q_norm_w, k_norm_w, w_ssm_up,
           w_attn_up, w_out, w_router, w_exp_gate, w_exp_up, w_exp_down):
    batch, seq, d = x.shape
    depth = w_mod.shape[0]
    t = batch * seq
    d_attn = N_Q_HEADS * HEAD_DIM
    d_kv = N_KV_HEADS * HEAD_DIM
    cap = CAPACITY_FACTOR * seq // N_EXPERTS
    tm = min(512, seq)
    tq = min(256, seq)

    tabs = _rope_tables(seq)
    head_id = jnp.arange(d_attn) // HEAD_DIM
    hm = (head_id[:, None] == head_id[None, :]).astype(BF16)
    tri = (jnp.arange(LANES)[:, None] <= jnp.arange(LANES)[None, :]).astype(BF16)
    c_pad = jnp.zeros((8, d), F32).at[:batch].set(c)
    mod_all = _mod_call(c_pad, w_mod, b_mod)[:, :batch]
    wa_all, wb_all, a_re_all, a_im_all = _s5_weights(ssm_lam_re, ssm_lam_im, ssm_log_dt, ssm_b_re,
                                                     ssm_b_im, ssm_c_re, ssm_c_im, ssm_d)

    x2 = x.reshape(t, d)
    for l in range(depth):
        mod3 = mod_all[l].reshape(batch, 1, 6 * d)
        qw = jnp.tile(q_norm_w[l], N_Q_HEADS).reshape(1, d_attn)
        kw = jnp.tile(k_norm_w[l], N_KV_HEADS).reshape(1, d_kv)
        u, q, k, v, gs, ga = _in_call(x2, mod3, ln1_w[l].reshape(1, d), w_in[l].astype(BF16), tabs,
                                      qw, kw, hm, seq=seq, tm=tm)
        y = _s5_mixer(u, wa_all[l], wb_all[l], a_re_all[l], a_im_all[l], batch=batch, seq=seq)
        attn = _attn_call(q, k, v, seq=seq, tq=tq)
        x1, h2, aff_t, aff_tm = _post_call(
            y, attn, gs, ga, x2, mod3, ln2_w[l].reshape(1, d), w_glu[l].astype(BF16),
            b_glu[l].reshape(1, -1), w_ssm_up[l].astype(BF16), w_attn_up[l].astype(BF16),
            w_out[l].astype(BF16), w_router[l].T, seq=seq, tm=tm)
        idx_t, half = _select_call(aff_t, tri, cap=cap)
        idx_flat = jnp.swapaxes(idx_t, 1, 2).reshape(-1)
        half_flat = half.reshape(-1)
        xg, gc = _gather_call(idx_flat, h2, aff_tm, batch=batch, seq=seq, cap=cap, ne=N_EXPERTS)
        ye = _ffn_call(xg, gc, w_exp_gate, w_exp_up, w_exp_down, layer=l, tf=512,
                       rows=min(1024, batch * cap))
        x2 = _combine_call(idx_flat, half_flat, ye, x1, mod3, batch=batch, seq=seq, cap=cap)
    return x2.reshape(batch, seq, d)
```

```python
import functools
import math

import jax
import jax.numpy as jnp
from jax import lax
from jax.experimental import pallas as pl
from jax.experimental.pallas import tpu as pltpu

F32 = jnp.float32
BF16 = jnp.bfloat16

HEAD_DIM = 64
N_Q_HEADS = 8
N_KV_HEADS = 2
Q_PER_KV = N_Q_HEADS // N_KV_HEADS
ROT_HALF = HEAD_DIM // 2
ROPE_THETA = 10000.0
GRID_W = 64
SSM_GROUP = 16
SSM_STATE = 64
N_EXPERTS = 16
CAPACITY_FACTOR = 2
EPS = 1e-6

S5_CHUNK = 16
S5_PAIR = 2
LANES = 128
V7X_VMEM_BYTES = 64 * 1024 * 1024
LOG2E = 1.4426950408889634


def _cparams(semantics, vmem_mb):
    return pltpu.CompilerParams(dimension_semantics=semantics,
                                vmem_limit_bytes=int(vmem_mb * 1024 * 1024))


def _mod_kernel(c_ref, w_ref, b_ref, o_ref):
    c = c_ref[...]
    ca = c * jax.nn.sigmoid(c)
    o_ref[0] = jnp.dot(ca, w_ref[0], preferred_element_type=F32,
                       precision=lax.Precision.HIGHEST) + b_ref[0]


def _mod_call(c_pad, w_mod, b_mod):
    depth, d, n = w_mod.shape
    rows = c_pad.shape[0]
    tn = 1536
    return pl.pallas_call(
        _mod_kernel,
        grid=(depth, n // tn),
        in_specs=[pl.BlockSpec((rows, d), lambda l, j: (0, 0)),
                  pl.BlockSpec((1, d, tn), lambda l, j: (l, 0, j)),
                  pl.BlockSpec((1, 1, tn), lambda l, j: (l, 0, j))],
        out_specs=pl.BlockSpec((1, rows, tn), lambda l, j: (l, 0, j)),
        out_shape=jax.ShapeDtypeStruct((depth, rows, n), F32),
        name="adaln_mod",
        compiler_params=_cparams(("arbitrary", "arbitrary"), 32),
    )(c_pad, w_mod, b_mod.reshape(depth, 1, n))


def _head_mean_square(xsq, hm):
    hi = xsq.astype(BF16)
    lo = (xsq - hi.astype(F32)).astype(BF16)
    s = jnp.dot(hi, hm, preferred_element_type=F32) + jnp.dot(lo, hm, preferred_element_type=F32)
    return s * (1.0 / HEAD_DIM)


def _rope_block(blk, cos, sa, sb):
    return (blk * cos + pltpu.roll(blk, LANES - ROT_HALF // 2, 1) * sa
            + pltpu.roll(blk, ROT_HALF // 2, 1) * sb)


def _in_kernel(x_ref, mod_ref, ln_ref, w_ref, cos_ref, sa_ref, sb_ref, qw_ref, kw_ref, hm_ref,
               u_ref, q_ref, k_ref, v_ref, gs_ref, ga_ref, *, d_model, d_ssm, d_attn, d_kv):
    x = x_ref[...]
    mod = mod_ref[0]
    sh = mod[:, 0:d_model]
    sc = mod[:, d_model:2 * d_model]
    ms = jnp.mean(x * x, axis=-1, keepdims=True)
    h = (x * lax.rsqrt(ms + EPS) * ln_ref[...]) * (1.0 + sc) + sh
    proj = jnp.dot(h.astype(BF16), w_ref[...], preferred_element_type=F32)
    o = 0
    u_ref[...] = proj[:, o:o + d_ssm].astype(u_ref.dtype)
    o += d_ssm
    q = proj[:, o:o + d_attn]
    o += d_attn
    k = proj[:, o:o + d_kv]
    o += d_kv
    v_ref[0] = proj[:, o:o + d_kv].T.astype(v_ref.dtype)
    o += d_kv
    gs_ref[...] = jax.nn.sigmoid(proj[:, o:o + d_model]).astype(gs_ref.dtype)
    o += d_model
    ga_ref[...] = jax.nn.sigmoid(proj[:, o:o + d_model]).astype(ga_ref.dtype)

    hm = hm_ref[...]
    cos = cos_ref[...]
    sa = sa_ref[...]
    sb = sb_ref[...]
    qn = q * lax.rsqrt(_head_mean_square(q * q, hm) + EPS) * qw_ref[...]
    kn = k * lax.rsqrt(_head_mean_square(k * k, hm[:d_kv, :d_kv]) + EPS) * kw_ref[...]
    qscale = (HEAD_DIM ** -0.5) * LOG2E
    qr = [_rope_block(qn[:, j * LANES:(j + 1) * LANES], cos, sa, sb) for j in range(d_attn // LANES)]
    q_ref[...] = (jnp.concatenate(qr, axis=-1) * qscale).astype(q_ref.dtype)
    kr = [_rope_block(kn[:, j * LANES:(j + 1) * LANES], cos, sa, sb) for j in range(d_kv // LANES)]
    k_ref[...] = jnp.concatenate(kr, axis=-1).astype(k_ref.dtype)


def _in_call(x2, mod3, ln_w, w_in, tabs, qw, kw, hm, *, seq, tm):
    t, d = x2.shape
    n_in = w_in.shape[1]
    d_ssm = d // 2
    d_attn = N_Q_HEADS * HEAD_DIM
    d_kv = N_KV_HEADS * HEAD_DIM
    tiles_per_seq = seq // tm
    cos, sa, sb = tabs
    kern = functools.partial(_in_kernel, d_model=d, d_ssm=d_ssm, d_attn=d_attn, d_kv=d_kv)
    row = lambda i: (i, 0)
    const = lambda i: (0, 0)
    tab = lambda i: (i % tiles_per_seq, 0)
    return pl.pallas_call(
        kern,
        grid=(t // tm,),
        in_specs=[pl.BlockSpec((tm, d), row),
                  pl.BlockSpec((1, 1, mod3.shape[2]), lambda i: (i // tiles_per_seq, 0, 0)),
                  pl.BlockSpec((1, d), const),
                  pl.BlockSpec((d, n_in), const),
                  pl.BlockSpec((tm, LANES), tab),
                  pl.BlockSpec((tm, LANES), tab),
                  pl.BlockSpec((tm, LANES), tab),
                  pl.BlockSpec((1, d_attn), const),
                  pl.BlockSpec((1, d_kv), const),
                  pl.BlockSpec((d_attn, d_attn), const)],
        out_specs=[pl.BlockSpec((tm, d_ssm), row),
                   pl.BlockSpec((tm, d_attn), row),
                   pl.BlockSpec((tm, d_kv), row),
                   pl.BlockSpec((1, d_kv, tm), lambda i: (i // tiles_per_seq, 0, i % tiles_per_seq)),
                   pl.BlockSpec((tm, d), row),
                   pl.BlockSpec((tm, d), row)],
        out_shape=[jax.ShapeDtypeStruct((t, d_ssm), F32),
                   jax.ShapeDtypeStruct((t, d_attn), BF16),
                   jax.ShapeDtypeStruct((t, d_kv), BF16),
                   jax.ShapeDtypeStruct((t // seq, d_kv, seq), BF16),
                   jax.ShapeDtypeStruct((t, d), BF16),
                   jax.ShapeDtypeStruct((t, d), BF16)],
        name="in_proj",
        compiler_params=_cparams(("arbitrary",), 48),
    )(x2, mod3, ln_w, w_in, cos, sa, sb, qw, kw, hm)


def _rope_tables(seq):
    pos = jnp.arange(seq)
    row = (pos // GRID_W).astype(F32)
    col = (pos % GRID_W).astype(F32)
    inv = 1.0 / (ROPE_THETA ** (jnp.arange(0, ROT_HALF, 2, dtype=F32) / ROT_HALF))
    ang_r = row[:, None] * inv[None, :]
    ang_c = col[:, None] * inv[None, :]
    zeros = jnp.zeros_like(ang_r)
    cos_h = jnp.concatenate([jnp.cos(ang_r), jnp.cos(ang_r), jnp.cos(ang_c), jnp.cos(ang_c)], axis=-1)
    sa_h = jnp.concatenate([-jnp.sin(ang_r), zeros, -jnp.sin(ang_c), zeros], axis=-1)
    sb_h = jnp.concatenate([zeros, jnp.sin(ang_r), zeros, jnp.sin(ang_c)], axis=-1)
    rep = LANES // HEAD_DIM
    return tuple(jnp.tile(a, (1, rep)) for a in (cos_h, sa_h, sb_h))


def _attn_heads_out(o_t, tq):
    return [o_t[:, g * tq:(g + 1) * tq].T for g in range(Q_PER_KV)]


def _attn_kernel(q_ref, k_ref, vt_ref, o_ref, *, kb):
    tq = q_ref.shape[0]
    seq = k_ref.shape[0]
    nblk = seq // kb
    width = Q_PER_KV * HEAD_DIM
    q_ts = []
    for kvh in range(N_KV_HEADS):
        qt = q_ref[:, kvh * width:(kvh + 1) * width].astype(F32).T.astype(BF16)
        q_ts.append(jnp.concatenate([qt[g * HEAD_DIM:(g + 1) * HEAD_DIM] for g in range(Q_PER_KV)],
                                    axis=1))

    def scores(kvh, j):
        lo = kvh * HEAD_DIM
        return jnp.dot(k_ref[j * kb:(j + 1) * kb, lo:lo + HEAD_DIM], q_ts[kvh],
                       preferred_element_type=F32)

    def weighted_v(kvh, j, p):
        lo = kvh * HEAD_DIM
        return jnp.dot(vt_ref[0, lo:lo + HEAD_DIM, j * kb:(j + 1) * kb], p.astype(BF16),
                       preferred_element_type=F32)

    outs = []
    worst = jnp.zeros((1, 1), F32)
    for kvh in range(N_KV_HEADS):
        s0 = scores(kvh, 0)
        m0 = jnp.max(s0, axis=0, keepdims=True)
        l = jnp.zeros_like(m0)
        acc = jnp.zeros((HEAD_DIM, m0.shape[1]), F32)
        s_cur = s0
        for j in range(nblk):
            s_next = scores(kvh, j + 1) if j + 1 < nblk else None
            p = jnp.exp2(s_cur - m0)
            l = l + jnp.sum(p, axis=0, keepdims=True)
            acc = acc + weighted_v(kvh, j, p)
            s_cur = s_next
        worst = jnp.maximum(worst, jnp.max(l, axis=1, keepdims=True))
        outs.extend(_attn_heads_out(acc / l, tq))
    o_ref[...] = jnp.concatenate(outs, axis=-1).astype(o_ref.dtype)

    @pl.when(worst[0, 0] > jnp.finfo(F32).max)
    def _exact():
        outs = []
        for kvh in range(N_KV_HEADS):
            m = jnp.full((1, Q_PER_KV * tq), -jnp.inf, F32)
            l = jnp.zeros_like(m)
            acc = jnp.zeros((HEAD_DIM, Q_PER_KV * tq), F32)
            for j in range(nblk):
                s = scores(kvh, j)
                m_new = jnp.maximum(m, jnp.max(s, axis=0, keepdims=True))
                alpha = jnp.exp2(m - m_new)
                p = jnp.exp2(s - m_new)
                l = l * alpha + jnp.sum(p, axis=0, keepdims=True)
                acc = acc * alpha + weighted_v(kvh, j, p)
                m = m_new
            outs.extend(_attn_heads_out(acc / l, tq))
        o_ref[...] = jnp.concatenate(outs, axis=-1).astype(o_ref.dtype)


def _attn_call(q, k, vt, *, seq, tq):
    t, d_attn = q.shape
    d_kv = k.shape[1]
    nq = seq // tq
    return pl.pallas_call(
        functools.partial(_attn_kernel, kb=min(256, seq)),
        grid=(t // seq, nq),
        in_specs=[pl.BlockSpec((tq, d_attn), lambda b, i: (b * nq + i, 0)),
                  pl.BlockSpec((seq, d_kv), lambda b, i: (b, 0)),
                  pl.BlockSpec((1, d_kv, seq), lambda b, i: (b, 0, 0))],
        out_specs=pl.BlockSpec((tq, d_attn), lambda b, i: (b * nq + i, 0)),
        out_shape=jax.ShapeDtypeStruct((t, d_attn), BF16),
        name="attention",
        compiler_params=_cparams(("arbitrary", "arbitrary"), 48),
    )(q, k, vt)


def _s5_weights(lam_re, lam_im, log_dt, b_re, b_im, c_re, c_im, d_skip):
    hp = lax.Precision.HIGHEST
    nl, _, ng, p = lam_re.shape
    cg = SSM_GROUP
    L = S5_CHUNK
    kd = L * cg
    dt = jnp.exp(log_dt)[..., None]
    zr = lam_re * dt
    zi = lam_im * dt

    def cpow(e):
        mag = jnp.exp(zr[..., None] * e)
        return mag * jnp.cos(zi[..., None] * e), mag * jnp.sin(zi[..., None] * e)

    lb_re, lb_im = jnp.exp(zr) * jnp.cos(zi), jnp.exp(zr) * jnp.sin(zi)
    nr, ni = lb_re - 1.0, lb_im
    den = lam_re * lam_re + lam_im * lam_im
    f_re = ((nr * lam_re + ni * lam_im) / den)[..., None]
    f_im = ((ni * lam_re - nr * lam_im) / den)[..., None]
    bb_re = f_re * b_re - f_im * b_im
    bb_im = f_re * b_im + f_im * b_re

    m_idx = jnp.arange(L + 1, dtype=F32)
    expo = jnp.stack([m_idx, L - m_idx])[None, :, None, None, :]
    pw_re, pw_im = cpow(expo)
    rep = jnp.repeat(jnp.eye(L + 1, dtype=F32), cg, axis=1)
    til = jnp.tile(jnp.eye(cg, dtype=F32), (1, L + 1))
    pwx_re = jnp.einsum('dxgpm,mn->dxgpn', pw_re, rep, precision=hp)
    pwx_im = jnp.einsum('dxgpm,mn->dxgpn', pw_im, rep, precision=hp)
    ct_re = jnp.einsum('dxgcp,cn->dxgpn', c_re, til, precision=hp)
    ct_im = jnp.einsum('dxgcp,cn->dxgpn', c_im, til, precision=hp)
    ca_re = ct_re * pwx_re - ct_im * pwx_im
    ca_im = ct_re * pwx_im + ct_im * pwx_re
    ef_re, ef_im = ca_re[:, 0, ..., :kd], ca_im[:, 0, ..., :kd]
    w2f_re, w2f_im = ca_re[:, 0, ..., cg:], ca_im[:, 0, ..., cg:]
    w2b_re, w2b_im = ca_re[:, 1, ..., :kd], ca_im[:, 1, ..., :kd]
    eb_re, eb_im = ca_re[:, 1, ..., cg:], ca_im[:, 1, ..., cg:]

    def lag_rows(x, e_re, e_im):
        return (jnp.einsum('dgpe,dgpn->dgen', bb_re[:, x], e_re, precision=hp)
                - jnp.einsum('dgpe,dgpn->dgen', bb_im[:, x], e_im, precision=hp))

    kf = lag_rows(0, ef_re, ef_im)
    kb = lag_rows(1, eb_re, eb_im)
    zeros = jnp.zeros_like(kf)
    pf = jnp.concatenate([zeros, kf], axis=-1)
    pb = jnp.concatenate([kb, zeros], axis=-1)
    tf = jnp.stack([pf[..., kd - cg * s:2 * kd - cg * s] for s in range(L)], axis=2)
    tb = jnp.stack([pb[..., cg * (L - 1 - s):cg * (L - 1 - s) + kd] for s in range(L)], axis=2)
    skip = jnp.eye(kd, dtype=F32) * jnp.tile(d_skip.reshape(nl, ng, 1, cg), (1, 1, 1, L))
    tmat = (tf + tb).reshape(nl, ng, kd, kd) + skip

    s_idx = jnp.arange(L, dtype=F32)
    sexp = jnp.stack([L - 1 - s_idx, s_idx])[None, :, None, None, :]
    ps_re, ps_im = cpow(sexp)
    w1_re = (jnp.einsum('dxgps,dxgpe->dxgsep', ps_re, bb_re)
             - jnp.einsum('dxgps,dxgpe->dxgsep', ps_im, bb_im)).reshape(nl, 2, ng, kd, p)
    w1_im = (jnp.einsum('dxgps,dxgpe->dxgsep', ps_re, bb_im)
             + jnp.einsum('dxgps,dxgpe->dxgsep', ps_im, bb_re)).reshape(nl, 2, ng, kd, p)

    wa = jnp.concatenate([tmat, w1_re[:, 0], w1_im[:, 0], w1_re[:, 1], w1_im[:, 1]], axis=-1)
    wb = jnp.concatenate([w2f_re, -w2f_im, w2b_re, -w2b_im], axis=2)
    al_re, al_im = cpow(jnp.float32(L))
    a_re = al_re.reshape(nl, 2, ng * p)
    a_im = al_im.reshape(nl, 2, ng * p)
    return wa.astype(BF16), wb.astype(BF16), a_re, a_im


GROUPS_PER_VREG = LANES // SSM_GROUP
PAIRS_PER_VREG = GROUPS_PER_VREG // S5_PAIR
TOKENS_PER_HALF = LANES // SSM_GROUP


def _granule_transpose(arrs):
    gran = lax.broadcasted_iota(jnp.int32, arrs[0].shape, 1) // SSM_GROUP
    cur = list(arrs)
    for dist in (4, 2, 1):
        keep = (gran & dist) == 0
        nxt = list(cur)
        for i in range(len(cur)):
            if i & dist:
                continue
            a, b = cur[i], cur[i + dist]
            nxt[i] = jnp.where(keep, a, pltpu.roll(b, dist * SSM_GROUP, 1))
            nxt[i + dist] = jnp.where(keep, pltpu.roll(a, LANES - dist * SSM_GROUP, 1), b)
        cur = nxt
    return cur


def _pair_lanes(a, b):
    low = lax.broadcasted_iota(jnp.int32, a.shape, 1) < LANES // 2
    return (jnp.where(low, a, pltpu.roll(b, LANES // 2, 1)),
            jnp.where(low, pltpu.roll(a, LANES // 2, 1), b))


def _s5_chunk_kernel(u_ref, wa_ref, yi_ref, sfr_ref, sfi_ref, sbr_ref, sbi_ref, *, nchunk):
    halves = []
    for jh in range(S5_CHUNK // TOKENS_PER_HALF):
        toks = [u_ref[pl.ds(jh * TOKENS_PER_HALF + jj, nchunk, stride=S5_CHUNK), :]
                for jj in range(TOKENS_PER_HALF)]
        halves.append([o.astype(BF16) for o in _granule_transpose(toks)])
    kd = S5_CHUNK * SSM_GROUP
    for qq in range(PAIRS_PER_VREG):
        res = []
        for gl in range(S5_PAIR):
            g = S5_PAIR * qq + gl
            z = jnp.concatenate([halves[jh][g] for jh in range(len(halves))], axis=-1)
            r = jnp.dot(z, wa_ref[g], preferred_element_type=F32)
            yi_ref[0, g] = r[:, :kd]
            res.append(r)
        lo = qq * LANES
        sfr_ref[:, lo:lo + LANES], sfi_ref[:, lo:lo + LANES] = _pair_lanes(
            res[0][:, kd:kd + LANES], res[1][:, kd:kd + LANES])
        sbr_ref[:, lo:lo + LANES], sbi_ref[:, lo:lo + LANES] = _pair_lanes(
            res[0][:, kd + LANES:kd + 2 * LANES], res[1][:, kd + LANES:kd + 2 * LANES])


def _s5_chunk_call(u, wa, *, batch, seq):
    t, d_ssm = u.shape
    nchunk = seq // S5_CHUNK
    nv = d_ssm // LANES
    ng, kd, ncol = wa.shape
    sw = PAIRS_PER_VREG * LANES
    st = jax.ShapeDtypeStruct((nchunk, batch * nv * sw), F32)
    sspec = pl.BlockSpec((nchunk, sw), lambda b, v: (0, b * nv + v))
    return pl.pallas_call(
        functools.partial(_s5_chunk_kernel, nchunk=nchunk),
        grid=(batch, nv),
        in_specs=[pl.BlockSpec((seq, LANES), lambda b, v: (b, v)),
                  pl.BlockSpec((GROUPS_PER_VREG, kd, ncol), lambda b, v: (v, 0, 0))],
        out_specs=[pl.BlockSpec((1, GROUPS_PER_VREG, nchunk, kd), lambda b, v: (b, v, 0, 0)),
                   sspec, sspec, sspec, sspec],
        out_shape=[jax.ShapeDtypeStruct((batch, ng, nchunk, kd), F32), st, st, st, st],
        name="s5_chunk",
        compiler_params=_cparams(("arbitrary", "arbitrary"), 40),
    )(u, wa)


def _s5scan_kernel(sfr_ref, sfi_ref, sbr_ref, sbi_ref, afr_ref, afi_ref, abr_ref, abi_ref,
                   xfr_ref, xfi_ref, xbr_ref, xbi_ref, *, nchunk):
    afr, afi = afr_ref[...], afi_ref[...]
    abr, abi = abr_ref[...], abi_ref[...]
    zero = jnp.zeros_like(afr)

    def body(k, carry):
        fr, fi, br, bi = carry
        kb = nchunk - 1 - k
        rowf = pl.ds(k, 1)
        rowb = pl.ds(kb, 1)
        xfr_ref[rowf, :] = fr
        xfi_ref[rowf, :] = fi
        xbr_ref[rowb, :] = br
        xbi_ref[rowb, :] = bi
        nfr = fr * afr - fi * afi + sfr_ref[rowf, :]
        nfi = fr * afi + fi * afr + sfi_ref[rowf, :]
        nbr = br * abr - bi * abi + sbr_ref[rowb, :]
        nbi = br * abi + bi * abr + sbi_ref[rowb, :]
        return nfr, nfi, nbr, nbi

    lax.fori_loop(0, nchunk, body, (zero, zero, zero, zero))


def _s5scan_call(s4, a4, *, nchunk, tl):
    w = s4[0].shape[1]
    sspec = pl.BlockSpec((nchunk, tl), lambda j: (0, j))
    aspec = pl.BlockSpec((1, tl), lambda j: (0, j))
    st = jax.ShapeDtypeStruct((nchunk, w), F32)
    return pl.pallas_call(
        functools.partial(_s5scan_kernel, nchunk=nchunk),
        grid=(w // tl,),
        in_specs=[sspec] * 4 + [aspec] * 4,
        out_specs=[sspec] * 4,
        out_shape=[st] * 4,
        name="s5_scan",
        compiler_params=_cparams(("arbitrary",), 32),
    )(*s4, *a4)


def _s5_readout_kernel(yi_ref, xfr_ref, xfi_ref, xbr_ref, xbi_ref, wb_ref, y_ref, *, nchunk):
    nhalf = S5_CHUNK // TOKENS_PER_HALF
    pieces = [[None] * GROUPS_PER_VREG for _ in range(nhalf)]
    for qq in range(PAIRS_PER_VREG):
        lo = qq * LANES
        xf = _pair_lanes(xfr_ref[:, lo:lo + LANES], xfi_ref[:, lo:lo + LANES])
        xb = _pair_lanes(xbr_ref[:, lo:lo + LANES], xbi_ref[:, lo:lo + LANES])
        for gl in range(S5_PAIR):
            g = S5_PAIR * qq + gl
            xs = jnp.concatenate([xf[gl], xb[gl]], axis=-1)
            y = yi_ref[0, g] + jnp.dot(xs.astype(BF16), wb_ref[g], preferred_element_type=F32)
            for jh in range(nhalf):
                pieces[jh][g] = y[:, jh * LANES:(jh + 1) * LANES]
    for jh in range(nhalf):
        toks = _granule_transpose(pieces[jh])
        for jj in range(TOKENS_PER_HALF):
            y_ref[pl.ds(jh * TOKENS_PER_HALF + jj, nchunk, stride=S5_CHUNK), :] = toks[jj]


def _s5_readout_call(yi, x4, wb, *, batch, seq):
    _, ng, nchunk, kd = yi.shape
    nv = ng // GROUPS_PER_VREG
    sw = PAIRS_PER_VREG * LANES
    xspec = pl.BlockSpec((nchunk, sw), lambda b, v: (0, b * nv + v))
    return pl.pallas_call(
        functools.partial(_s5_readout_kernel, nchunk=nchunk),
        grid=(batch, nv),
        in_specs=[pl.BlockSpec((1, GROUPS_PER_VREG, nchunk, kd), lambda b, v: (b, v, 0, 0)),
                  xspec, xspec, xspec, xspec,
                  pl.BlockSpec((GROUPS_PER_VREG, wb.shape[1], kd), lambda b, v: (v, 0, 0))],
        out_specs=pl.BlockSpec((seq, LANES), lambda b, v: (b, v)),
        out_shape=jax.ShapeDtypeStruct((batch * seq, nv * LANES), F32),
        name="s5_readout",
        compiler_params=_cparams(("arbitrary", "arbitrary"), 40),
    )(yi, *x4, wb)


def _s5_mixer(u, wa, wb, a_re, a_im, *, batch, seq):
    nchunk = seq // S5_CHUNK
    yi, sfr, sfi, sbr, sbi = _s5_chunk_call(u, wa, batch=batch, seq=seq)
    w = a_re.shape[1]
    a4 = [jnp.tile(a.reshape(1, w), (1, batch)) for a in (a_re[0], a_im[0], a_re[1], a_im[1])]
    tl = min(1024, batch * w)
    x4 = _s5scan_call([sfr, sfi, sbr, sbi], a4, nchunk=nchunk, tl=tl)
    return _s5_readout_call(yi, x4, wb, batch=batch, seq=seq)


POST_SUBTILES = 1


def _post_kernel(y_ref, at_ref, gs_ref, ga_ref, x_ref, mod_ref, ln_ref,
                 wg_ref, bg_ref, wsu_ref, wau_ref, wo_ref, wr_ref,
                 x1_ref, h2_ref, aff_ref, afft_ref, *, d_model):
    mod = mod_ref[0]
    g1 = mod[:, 2 * d_model:3 * d_model]
    sh2 = mod[:, 3 * d_model:4 * d_model]
    sc2 = mod[:, 4 * d_model:5 * d_model]
    tm = y_ref.shape[0]
    nsub = POST_SUBTILES
    rows = tm // nsub

    def glu_stage(sl, st):
        z = jax.nn.gelu(y_ref[sl, :], approximate=True)
        glu = jnp.dot(z.astype(BF16), wg_ref[...], preferred_element_type=F32) + bg_ref[...]
        return z, glu

    def up_stage(sl, st):
        z, glu = st
        ssm = z * jax.nn.sigmoid(glu)
        su = jnp.dot(ssm.astype(BF16), wsu_ref[...], preferred_element_type=F32)
        au = jnp.dot(at_ref[sl, :], wau_ref[...], preferred_element_type=F32)
        return su, au

    def out_stage(sl, st):
        su, au = st
        merged = gs_ref[sl, :].astype(F32) * su + ga_ref[sl, :].astype(F32) * au
        return jnp.dot(merged.astype(BF16), wo_ref[...], preferred_element_type=F32)

    def router_stage(sl, mix):
        x1 = x_ref[sl, :] + g1 * mix
        x1_ref[sl, :] = x1
        ms = jnp.mean(x1 * x1, axis=-1, keepdims=True)
        h2 = (x1 * lax.rsqrt(ms + EPS) * ln_ref[...]) * (1.0 + sc2) + sh2
        h2_ref[sl, :] = h2.astype(h2_ref.dtype)
        lg = lax.dot_general(wr_ref[...], h2, (((1,), (1,)), ((), ())), preferred_element_type=F32,
                             precision=lax.Precision.HIGHEST)
        lg = lg - jnp.max(lg, axis=0, keepdims=True)
        ex = jnp.exp(lg)
        aff = ex / jnp.sum(ex, axis=0, keepdims=True)
        aff_ref[0, :, sl] = aff
        pad = jnp.zeros((LANES - aff.shape[0], aff.shape[1]), F32)
        afft_ref[sl, :] = jnp.concatenate([aff, pad], axis=0).T
        return None

    stages = (glu_stage, up_stage, out_stage, router_stage)
    state = [None] * nsub
    for step in range(len(stages) + nsub - 1):
        for sub in range(nsub):
            k = step - sub
            if 0 <= k < len(stages):
                state[sub] = stages[k](slice(sub * rows, (sub + 1) * rows), state[sub])


def _post_call(y, attn, gs, ga, x2, mod3, ln_w, wg, bg, wsu, wau, wo, wr_t, *, seq, tm):
    t, d = x2.shape
    d_ssm = y.shape[1]
    d_attn = attn.shape[1]
    ne = wr_t.shape[0]
    tiles_per_seq = seq // tm
    row = lambda i: (i, 0)
    const = lambda i: (0, 0)
    return pl.pallas_call(
        functools.partial(_post_kernel, d_model=d),
        grid=(t // tm,),
        in_specs=[pl.BlockSpec((tm, d_ssm), row),
                  pl.BlockSpec((tm, d_attn), row),
                  pl.BlockSpec((tm, d), row),
                  pl.BlockSpec((tm, d), row),
                  pl.BlockSpec((tm, d), row),
                  pl.BlockSpec((1, 1, mod3.shape[2]), lambda i: (i // tiles_per_seq, 0, 0)),
                  pl.BlockSpec((1, d), const),
                  pl.BlockSpec(wg.shape, const),
                  pl.BlockSpec((1, d_ssm), const),
                  pl.BlockSpec(wsu.shape, const),
                  pl.BlockSpec(wau.shape, const),
                  pl.BlockSpec(wo.shape, const),
                  pl.BlockSpec(wr_t.shape, const)],
        out_specs=[pl.BlockSpec((tm, d), row),
                   pl.BlockSpec((tm, d), row),
                   pl.BlockSpec((1, ne, tm), lambda i: (i // tiles_per_seq, 0, i % tiles_per_seq)),
                   pl.BlockSpec((tm, LANES), row)],
        out_shape=[jax.ShapeDtypeStruct((t, d), F32),
                   jax.ShapeDtypeStruct((t, d), F32),
                   jax.ShapeDtypeStruct((t // seq, ne, seq), F32),
                   jax.ShapeDtypeStruct((t, LANES), F32)],
        name="post_mix",
        compiler_params=_cparams(("arbitrary",), 48),
    )(y, attn, gs, ga, x2, mod3, ln_w, wg, bg, wsu, wau, wo, wr_t)


def _lane_cumsum(mask_f, tri):
    ne, seq = mask_f.shape
    run = jnp.zeros((ne, 1), F32)
    parts = []
    for j in range(seq // LANES):
        blk = mask_f[:, j * LANES:(j + 1) * LANES]
        cs = jnp.dot(blk.astype(BF16), tri, preferred_element_type=F32) + run
        parts.append(cs)
        run = run + jnp.sum(blk, axis=1, keepdims=True)
    return jnp.concatenate(parts, axis=-1)


def _select_kernel(aff_ref, tri_ref, idx_ref, half_ref, *, cap, seq, ts):
    aff = aff_ref[0]
    ne = aff.shape[0]
    tri = tri_ref[...]

    def body(i, thr):
        cand = thr | (jnp.int32(1) << (30 - i))
        cnt = jnp.sum(jnp.where(aff >= pltpu.bitcast(cand, F32), 1.0, 0.0), axis=1, keepdims=True)
        return jnp.where(cnt >= cap, cand, thr)

    thr = lax.fori_loop(0, 31, body, jnp.zeros((ne, 1), jnp.int32))
    gt = aff >= pltpu.bitcast(thr + 1, F32)
    eq = (aff >= pltpu.bitcast(thr, F32)) & jnp.logical_not(gt)
    need = cap - jnp.sum(jnp.where(gt, 1.0, 0.0), axis=1, keepdims=True)
    eq_f = jnp.where(eq, 1.0, 0.0)
    eq_rank = _lane_cumsum(eq_f, tri) - eq_f
    sel = gt | (eq & (eq_rank < need))
    sel_f = jnp.where(sel, 1.0, 0.0)
    csum = _lane_cumsum(sel_f, tri)
    rank = jnp.where(sel, csum - sel_f, -1.0)
    half_ref[0] = csum[:, seq // 2 - 1:seq // 2].astype(jnp.int32)

    slot = lax.broadcasted_iota(jnp.int32, (cap, ts), 0).astype(F32)
    lane_e = lax.broadcasted_iota(jnp.int32, (cap, ne), 1)
    idx = jnp.zeros((cap, ne), F32)
    for e in range(ne):
        col = jnp.zeros((cap, 1), F32)
        for j in range(seq // ts):
            tok = (lax.broadcasted_iota(jnp.int32, (1, ts), 1) + j * ts).astype(F32)
            hit = rank[e:e + 1, j * ts:(j + 1) * ts] == slot
            col = col + jnp.sum(jnp.where(hit, tok, 0.0), axis=1, keepdims=True)
        idx = jnp.where(lane_e == e, col, idx)
    idx_ref[0] = idx.astype(jnp.int32)


def _select_call(aff_t, tri, *, cap):
    b, ne, seq = aff_t.shape
    return pl.pallas_call(
        functools.partial(_select_kernel, cap=cap, seq=seq, ts=min(1024, seq)),
        grid=(b,),
        in_specs=[pl.BlockSpec((1, ne, seq), lambda i: (i, 0, 0)),
                  pl.BlockSpec((LANES, LANES), lambda i: (0, 0))],
        out_specs=[pl.BlockSpec((1, cap, ne), lambda i: (i, 0, 0)),
                   pl.BlockSpec((1, ne, 1), lambda i: (i, 0, 0))],
        out_shape=[jax.ShapeDtypeStruct((b, cap, ne), jnp.int32),
                   jax.ShapeDtypeStruct((b, ne, 1), jnp.int32)],
        name="expert_select",
        compiler_params=_cparams(("arbitrary",), 32),
    )(aff_t, tri)


GATHER_UNROLL = 8


def _gather_kernel(idx_ref, h_ref, aff_ref, xg_ref, gc_ref, rows_ref, arow_ref, *, cap, ne):
    b = pl.program_id(0)
    e = pl.program_id(1)
    base = (b * ne + e) * cap

    def body(g, carry):
        for k in range(GATHER_UNROLL):
            c = g * GATHER_UNROLL + k
            r = idx_ref[base + c]
            rows_ref[pl.ds(c, 1), :] = h_ref[pl.ds(r, 1), :]
            arow_ref[pl.ds(c, 1), :] = aff_ref[pl.ds(r, 1), :]
        return carry

    lax.fori_loop(0, cap // GATHER_UNROLL, body, 0)
    xg_ref[0] = rows_ref[...].astype(xg_ref.dtype)
    lane = lax.broadcasted_iota(jnp.int32, arow_ref.shape, 1)
    gc_ref[0] = jnp.sum(jnp.where(lane == e, arow_ref[...], 0.0), axis=1, keepdims=True)


def _gather_call(idx_flat, h2, aff_tm, *, batch, seq, cap, ne):
    t, d = h2.shape
    grid_spec = pltpu.PrefetchScalarGridSpec(
        num_scalar_prefetch=1,
        grid=(batch, ne),
        in_specs=[pl.BlockSpec((seq, d), lambda b, e, idx: (b, 0)),
                  pl.BlockSpec((seq, LANES), lambda b, e, idx: (b, 0))],
        out_specs=[pl.BlockSpec((1, cap, d), lambda b, e, idx: (e, b, 0)),
                   pl.BlockSpec((1, cap, 1), lambda b, e, idx: (e, b, 0))],
        scratch_shapes=[pltpu.VMEM((cap, d), F32), pltpu.VMEM((cap, LANES), F32)],
    )
    return pl.pallas_call(
        functools.partial(_gather_kernel, cap=cap, ne=ne),
        grid_spec=grid_spec,
        out_shape=[jax.ShapeDtypeStruct((ne, batch * cap, d), BF16),
                   jax.ShapeDtypeStruct((ne, batch * cap, 1), F32)],
        name="expert_gather",
        compiler_params=_cparams(("arbitrary", "arbitrary"), 48),
    )(idx_flat, h2, aff_tm)


def _ffn_kernel(x_ref, gc_ref, wg_ref, wu_ref, wd_ref, ye_ref, *, rows):
    f = pl.program_id(1)

    @pl.when(f == 0)
    def _init():
        ye_ref[...] = jnp.zeros_like(ye_ref)

    wg = wg_ref[0, 0].astype(BF16)
    wu = wu_ref[0, 0].astype(BF16)
    wd = wd_ref[0, 0].astype(BF16)
    for r in range(x_ref.shape[1] // rows):
        sl = slice(r * rows, (r + 1) * rows)
        x = x_ref[0, sl, :]
        hg = jnp.dot(x, wg, preferred_element_type=F32)
        hu = jnp.dot(x, wu, preferred_element_type=F32)
        hid = (hg * jax.nn.sigmoid(hg) * hu).astype(BF16)
        ye_ref[0, sl, :] += jnp.dot(hid, wd, preferred_element_type=F32)

    @pl.when(f == pl.num_programs(1) - 1)
    def _gate():
        ye_ref[0] = ye_ref[0] * gc_ref[0]


def _ffn_call(xg, gc, wg, wu, wd, *, layer, tf, rows):
    ne, n, d = xg.shape
    dff = wg.shape[3]
    return pl.pallas_call(
        functools.partial(_ffn_kernel, rows=rows),
        grid=(ne, dff // tf),
        in_specs=[pl.BlockSpec((1, n, d), lambda e, f: (e, 0, 0)),
                  pl.BlockSpec((1, n, 1), lambda e, f: (e, 0, 0)),
                  pl.BlockSpec((1, 1, d, tf), lambda e, f: (layer, e, 0, f)),
                  pl.BlockSpec((1, 1, d, tf), lambda e, f: (layer, e, 0, f)),
                  pl.BlockSpec((1, 1, tf, d), lambda e, f: (layer, e, f, 0))],
        out_specs=pl.BlockSpec((1, n, d), lambda e, f: (e, 0, 0)),
        out_shape=jax.ShapeDtypeStruct((ne, n, d), F32),
        name="expert_ffn",
        compiler_params=_cparams(("arbitrary", "arbitrary"), 56),
    )(xg, gc, wg, wu, wd)


SCATTER_UNROLL = 8
SEQ_HALVES = 2


def _combine_kernel(idx_ref, half_ref, ye_ref, x1_ref, mod_ref, o_ref, acc_ref, *, cap, ne, rows,
                    d_model):
    i = pl.program_id(0)
    e = pl.program_id(1)
    b = i // SEQ_HALVES
    hh = i % SEQ_HALVES

    @pl.when(e == 0)
    def _init():
        o_ref[...] = jnp.zeros_like(o_ref)
        acc_ref[...] = jnp.zeros_like(acc_ref)

    mid = half_ref[b * ne + e]
    lo = jnp.where(hh == 0, 0, mid)
    hi = jnp.where(hh == 0, mid, cap)
    base = (b * ne + e) * cap
    off = hh * rows

    def add_row(c, dst_ref):
        r = idx_ref[base + c] - off
        dst_ref[pl.ds(r, 1), :] += ye_ref[0, pl.ds(c, 1), :]

    def group(g, carry):
        for k in range(SCATTER_UNROLL):
            add_row(lo + g * SCATTER_UNROLL + k, o_ref if k % 2 == 0 else acc_ref)
        return carry

    ngroup = (hi - lo) // SCATTER_UNROLL
    lax.fori_loop(0, ngroup, group, 0)

    def tail(c, carry):
        add_row(c, o_ref)
        return carry

    lax.fori_loop(lo + ngroup * SCATTER_UNROLL, hi, tail, 0)

    @pl.when(e == pl.num_programs(1) - 1)
    def _fin():
        g2 = mod_ref[0][:, 5 * d_model:6 * d_model]
        o_ref[...] = x1_ref[...] + g2 * (o_ref[...] + acc_ref[...])


def _combine_call(idx_flat, half_flat, ye, x1, mod3, *, batch, seq, cap):
    t, d = x1.shape
    ne = ye.shape[0]
    rows = seq // SEQ_HALVES
    grid_spec = pltpu.PrefetchScalarGridSpec(
        num_scalar_prefetch=2,
        grid=(batch * SEQ_HALVES, ne),
        in_specs=[pl.BlockSpec((1, cap, d), lambda i, e, idx, half: (e, i // SEQ_HALVES, 0)),
                  pl.BlockSpec((rows, d), lambda i, e, idx, half: (i, 0)),
                  pl.BlockSpec((1, 1, mod3.shape[2]), lambda i, e, idx, half: (i // SEQ_HALVES, 0, 0))],
        out_specs=pl.BlockSpec((rows, d), lambda i, e, idx, half: (i, 0)),
        scratch_shapes=[pltpu.VMEM((rows, d), F32)],
    )
    return pl.pallas_call(
        functools.partial(_combine_kernel, cap=cap, ne=ne, rows=rows, d_model=d),
        grid_spec=grid_spec,
        out_shape=jax.ShapeDtypeStruct((t, d), F32),
        name="moe_combine",
        compiler_params=_cparams(("arbitrary", "arbitrary"), 52),
    )(idx_flat, half_flat, ye, x1, mod3)


def kernel(x, c, w_mod, b_mod, ln1_w, ln2_w, w_in, ssm_lam_re, ssm_lam_im, ssm_log_dt, ssm_b_re,
           ssm_b_im, ssm_c_re, ssm_c_im, ssm_d, w_glu, b_glu, q_norm_w, k_norm_w, w_ssm_up,
           w_attn_up, w_out, w_router, w_exp_gate, w_exp_up, w_exp_down):
    batch, seq, d = x.shape
    depth = w_mod.shape[0]
    t = batch * seq
    d_attn = N_Q_HEADS * HEAD_DIM
    d_kv = N_KV_HEADS * HEAD_DIM
    cap = CAPACITY_FACTOR * seq // N_EXPERTS
    tm = min(512, seq)
    tq = min(256, seq)

    tabs = _rope_tables(seq)
    head_id = jnp.arange(d_attn) // HEAD_DIM
    hm = (head_id[:, None] == head_id[None, :]).astype(BF16)
    tri = (jnp.arange(LANES)[:, None] <= jnp.arange(LANES)[None, :]).astype(BF16)
    c_pad = jnp.zeros((8, d), F32).at[:batch].set(c)
    mod_all = _mod_call(c_pad, w_mod, b_mod)[:, :batch]
    wa_all, wb_all, a_re_all, a_im_all = _s5_weights(ssm_lam_re, ssm_lam_im, ssm_log_dt, ssm_b_re,
                                                     ssm_b_im, ssm_c_re, ssm_c_im, ssm_d)

    x2 = x.reshape(t, d)
    for l in range(depth):
        mod3 = mod_all[l].reshape(batch, 1, 6 * d)
        qw = jnp.tile(q_norm_w[l], N_Q_HEADS).reshape(1, d_attn)
        kw = jnp.tile(k_norm_w[l], N_KV_HEADS).reshape(1, d_kv)
        u, q, k, v, gs, ga = _in_call(x2, mod3, ln1_w[l].reshape(1, d), w_in[l].astype(BF16), tabs,
                                      qw, kw, hm, seq=seq, tm=tm)
        y = _s5_mixer(u, wa_all[l], wb_all[l], a_re_all[l], a_im_all[l], batch=batch, seq=seq)
        attn = _attn_call(q, k, v, seq=seq, tq=tq)
        x1, h2, aff_t, aff_tm = _post_call(
            y, attn, gs, ga, x2, mod3, ln2_w[l].reshape(1, d), w_glu[l].astype(BF16),
            b_glu[l].reshape(1, -1), w_ssm_up[l].astype(BF16), w_attn_up[l].astype(BF16),
            w_out[l].astype(BF16), w_router[l].T, seq=seq, tm=tm)
        idx_t, half = _select_call(aff_t, tri, cap=cap)
        idx_flat = jnp.swapaxes(idx_t, 1, 2).reshape(-1)
        half_flat = half.reshape(-1)
        xg, gc = _gather_call(idx_flat, h2, aff_tm, batch=batch, seq=seq, cap=cap, ne=N_EXPERTS)
        ye = _ffn_call(xg, gc, w_exp_gate, w_exp_up, w_exp_down, layer=l, tf=512,
                       rows=min(1024, batch * cap))
        x2 = _combine_call(idx_flat, half_flat, ye, x1, mod3, batch=batch, seq=seq, cap=cap)
    return x2.reshape(batch, seq, d)
```

```python
import functools
import math

import jax
import jax.numpy as jnp
from jax import lax
from jax.experimental import pallas as pl
from jax.experimental.pallas import tpu as pltpu

F32 = jnp.float32
BF16 = jnp.bfloat16

HEAD_DIM = 64
N_Q_HEADS = 8
N_KV_HEADS = 2
Q_PER_KV = N_Q_HEADS // N_KV_HEADS
ROT_HALF = HEAD_DIM // 2
ROPE_THETA = 10000.0
GRID_W = 64
SSM_GROUP = 16
SSM_STATE = 64
N_EXPERTS = 16
CAPACITY_FACTOR = 2
EPS = 1e-6

S5_CHUNK = 16
S5_PAIR = 2
LANES = 128
V7X_VMEM_BYTES = 64 * 1024 * 1024
LOG2E = 1.4426950408889634


def _cparams(semantics, vmem_mb):
    return pltpu.CompilerParams(dimension_semantics=semantics,
                                vmem_limit_bytes=int(vmem_mb * 1024 * 1024))


def _mod_kernel(c_ref, w_ref, b_ref, o_ref):
    c = c_ref[...]
    ca = c * jax.nn.sigmoid(c)
    o_ref[0] = jnp.dot(ca, w_ref[0], preferred_element_type=F32,
                       precision=lax.Precision.HIGHEST) + b_ref[0]


def _mod_call(c_pad, w_mod, b_mod):
    depth, d, n = w_mod.shape
    rows = c_pad.shape[0]
    tn = 1536
    return pl.pallas_call(
        _mod_kernel,
        grid=(depth, n // tn),
        in_specs=[pl.BlockSpec((rows, d), lambda l, j: (0, 0)),
                  pl.BlockSpec((1, d, tn), lambda l, j: (l, 0, j)),
                  pl.BlockSpec((1, 1, tn), lambda l, j: (l, 0, j))],
        out_specs=pl.BlockSpec((1, rows, tn), lambda l, j: (l, 0, j)),
        out_shape=jax.ShapeDtypeStruct((depth, rows, n), F32),
        name="adaln_mod",
        compiler_params=_cparams(("arbitrary", "arbitrary"), 32),
    )(c_pad, w_mod, b_mod.reshape(depth, 1, n))


def _head_mean_square(xsq, hm):
    hi = xsq.astype(BF16)
    lo = (xsq - hi.astype(F32)).astype(BF16)
    s = jnp.dot(hi, hm, preferred_element_type=F32) + jnp.dot(lo, hm, preferred_element_type=F32)
    return s * (1.0 / HEAD_DIM)


def _rope_block(blk, cos, sa, sb):
    return (blk * cos + pltpu.roll(blk, LANES - ROT_HALF // 2, 1) * sa
            + pltpu.roll(blk, ROT_HALF // 2, 1) * sb)


def _in_kernel(x_ref, mod_ref, ln_ref, w_ref, cos_ref, sa_ref, sb_ref, qw_ref, kw_ref, hm_ref,
               u_ref, q_ref, k_ref, v_ref, gs_ref, ga_ref, *, d_model, d_ssm, d_attn, d_kv):
    x = x_ref[...]
    mod = mod_ref[0]
    sh = mod[:, 0:d_model]
    sc = mod[:, d_model:2 * d_model]
    ms = jnp.mean(x * x, axis=-1, keepdims=True)
    h = (x * lax.rsqrt(ms + EPS) * ln_ref[...]) * (1.0 + sc) + sh
    hb = h.astype(BF16)
    hm = hm_ref[...]
    cos = cos_ref[...]
    sa = sa_ref[...]
    sb = sb_ref[...]
    qscale = (HEAD_DIM ** -0.5) * LOG2E

    def finish_u(p):
        u_ref[...] = p.astype(u_ref.dtype)

    def finish_q(q):
        qn = q * lax.rsqrt(_head_mean_square(q * q, hm) + EPS) * qw_ref[...]
        qr = [_rope_block(qn[:, j * LANES:(j + 1) * LANES], cos, sa, sb) for j in range(d_attn // LANES)]
        q_ref[...] = (jnp.concatenate(qr, axis=-1) * qscale).astype(q_ref.dtype)

    def finish_kv(p):
        k = p[:, :d_kv]
        v_ref[0] = p[:, d_kv:].T.astype(v_ref.dtype)
        kn = k * lax.rsqrt(_head_mean_square(k * k, hm[:d_kv, :d_kv]) + EPS) * kw_ref[...]
        kr = [_rope_block(kn[:, j * LANES:(j + 1) * LANES], cos, sa, sb) for j in range(d_kv // LANES)]
        k_ref[...] = jnp.concatenate(kr, axis=-1).astype(k_ref.dtype)

    def finish_gs(p):
        gs_ref[...] = jax.nn.sigmoid(p).astype(gs_ref.dtype)

    def finish_ga(p):
        ga_ref[...] = jax.nn.sigmoid(p).astype(ga_ref.dtype)

    groups = ((d_ssm, finish_u), (d_attn, finish_q), (2 * d_kv, finish_kv),
              (d_model, finish_gs), (d_model, finish_ga))
    starts = [sum(g[0] for g in groups[:i]) for i in range(len(groups))]

    def project(i):
        return jnp.dot(hb, w_ref[:, starts[i]:starts[i] + groups[i][0]], preferred_element_type=F32)

    nxt = project(0)
    for i, (_, finish) in enumerate(groups):
        cur = nxt
        if i + 1 < len(groups):
            nxt = project(i + 1)
        finish(cur)


def _in_call(x2, mod3, ln_w, w_in, tabs, qw, kw, hm, *, seq, tm):
    t, d = x2.shape
    n_in = w_in.shape[1]
    d_ssm = d // 2
    d_attn = N_Q_HEADS * HEAD_DIM
    d_kv = N_KV_HEADS * HEAD_DIM
    tiles_per_seq = seq // tm
    cos, sa, sb = tabs
    kern = functools.partial(_in_kernel, d_model=d, d_ssm=d_ssm, d_attn=d_attn, d_kv=d_kv)
    row = lambda i: (i, 0)
    const = lambda i: (0, 0)
    tab = lambda i: (i % tiles_per_seq, 0)
    return pl.pallas_call(
        kern,
        grid=(t // tm,),
        in_specs=[pl.BlockSpec((tm, d), row),
                  pl.BlockSpec((1, 1, mod3.shape[2]), lambda i: (i // tiles_per_seq, 0, 0)),
                  pl.BlockSpec((1, d), const),
                  pl.BlockSpec((d, n_in), const),
                  pl.BlockSpec((tm, LANES), tab),
                  pl.BlockSpec((tm, LANES), tab),
                  pl.BlockSpec((tm, LANES), tab),
                  pl.BlockSpec((1, d_attn), const),
                  pl.BlockSpec((1, d_kv), const),
                  pl.BlockSpec((d_attn, d_attn), const)],
        out_specs=[pl.BlockSpec((tm, d_ssm), row),
                   pl.BlockSpec((tm, d_attn), row),
                   pl.BlockSpec((tm, d_kv), row),
                   pl.BlockSpec((1, d_kv, tm), lambda i: (i // tiles_per_seq, 0, i % tiles_per_seq)),
                   pl.BlockSpec((tm, d), row),
                   pl.BlockSpec((tm, d), row)],
        out_shape=[jax.ShapeDtypeStruct((t, d_ssm), F32),
                   jax.ShapeDtypeStruct((t, d_attn), BF16),
                   jax.ShapeDtypeStruct((t, d_kv), BF16),
                   jax.ShapeDtypeStruct((t // seq, d_kv, seq), BF16),
                   jax.ShapeDtypeStruct((t, d), BF16),
                   jax.ShapeDtypeStruct((t, d), BF16)],
        name="in_proj",
        compiler_params=_cparams(("arbitrary",), 48),
    )(x2, mod3, ln_w, w_in, cos, sa, sb, qw, kw, hm)


def _rope_tables(seq):
    pos = jnp.arange(seq)
    row = (pos // GRID_W).astype(F32)
    col = (pos % GRID_W).astype(F32)
    inv = 1.0 / (ROPE_THETA ** (jnp.arange(0, ROT_HALF, 2, dtype=F32) / ROT_HALF))
    ang_r = row[:, None] * inv[None, :]
    ang_c = col[:, None] * inv[None, :]
    zeros = jnp.zeros_like(ang_r)
    cos_h = jnp.concatenate([jnp.cos(ang_r), jnp.cos(ang_r), jnp.cos(ang_c), jnp.cos(ang_c)], axis=-1)
    sa_h = jnp.concatenate([-jnp.sin(ang_r), zeros, -jnp.sin(ang_c), zeros], axis=-1)
    sb_h = jnp.concatenate([zeros, jnp.sin(ang_r), zeros, jnp.sin(ang_c)], axis=-1)
    rep = LANES // HEAD_DIM
    return tuple(jnp.tile(a, (1, rep)) for a in (cos_h, sa_h, sb_h))


def _attn_heads_out(o_t, tq):
    return [o_t[:, g * tq:(g + 1) * tq].T for g in range(Q_PER_KV)]


def _attn_kernel(q_ref, k_ref, vt_ref, o_ref, *, kb):
    tq = q_ref.shape[0]
    seq = k_ref.shape[0]
    nblk = seq // kb
    width = Q_PER_KV * HEAD_DIM
    q_ts = []
    for kvh in range(N_KV_HEADS):
        qt = q_ref[:, kvh * width:(kvh + 1) * width].astype(F32).T.astype(BF16)
        q_ts.append(jnp.concatenate([qt[g * HEAD_DIM:(g + 1) * HEAD_DIM] for g in range(Q_PER_KV)],
                                    axis=1))

    def scores(kvh, j):
        lo = kvh * HEAD_DIM
        return jnp.dot(k_ref[j * kb:(j + 1) * kb, lo:lo + HEAD_DIM], q_ts[kvh],
                       preferred_element_type=F32)

    def weighted_v(kvh, j, p):
        lo = kvh * HEAD_DIM
        return jnp.dot(vt_ref[0, lo:lo + HEAD_DIM, j * kb:(j + 1) * kb], p.astype(BF16),
                       preferred_element_type=F32)

    outs = []
    worst = jnp.zeros((1, 1), F32)
    for kvh in range(N_KV_HEADS):
        s0 = scores(kvh, 0)
        m0 = jnp.max(s0, axis=0, keepdims=True)
        l = jnp.zeros_like(m0)
        acc = jnp.zeros((HEAD_DIM, m0.shape[1]), F32)
        s_cur = s0
        for j in range(nblk):
            s_next = scores(kvh, j + 1) if j + 1 < nblk else None
            p = jnp.exp2(s_cur - m0)
            l = l + jnp.sum(p, axis=0, keepdims=True)
            acc = acc + weighted_v(kvh, j, p)
            s_cur = s_next
        worst = jnp.maximum(worst, jnp.max(l, axis=1, keepdims=True))
        outs.extend(_attn_heads_out(acc / l, tq))
    o_ref[...] = jnp.concatenate(outs, axis=-1).astype(o_ref.dtype)

    @pl.when(worst[0, 0] > jnp.finfo(F32).max)
    def _exact():
        outs = []
        for kvh in range(N_KV_HEADS):
            m = jnp.full((1, Q_PER_KV * tq), -jnp.inf, F32)
            l = jnp.zeros_like(m)
            acc = jnp.zeros((HEAD_DIM, Q_PER_KV * tq), F32)
            for j in range(nblk):
                s = scores(kvh, j)
                m_new = jnp.maximum(m, jnp.max(s, axis=0, keepdims=True))
                alpha = jnp.exp2(m - m_new)
                p = jnp.exp2(s - m_new)
                l = l * alpha + jnp.sum(p, axis=0, keepdims=True)
                acc = acc * alpha + weighted_v(kvh, j, p)
                m = m_new
            outs.extend(_attn_heads_out(acc / l, tq))
        o_ref[...] = jnp.concatenate(outs, axis=-1).astype(o_ref.dtype)


def _attn_call(q, k, vt, *, seq, tq):
    t, d_attn = q.shape
    d_kv = k.shape[1]
    nq = seq // tq
    return pl.pallas_call(
        functools.partial(_attn_kernel, kb=min(256, seq)),
        grid=(t // seq, nq),
        in_specs=[pl.BlockSpec((tq, d_attn), lambda b, i: (b * nq + i, 0)),
                  pl.BlockSpec((seq, d_kv), lambda b, i: (b, 0)),
                  pl.BlockSpec((1, d_kv, seq), lambda b, i: (b, 0, 0))],
        out_specs=pl.BlockSpec((tq, d_attn), lambda b, i: (b * nq + i, 0)),
        out_shape=jax.ShapeDtypeStruct((t, d_attn), BF16),
        name="attention",
        compiler_params=_cparams(("arbitrary", "arbitrary"), 48),
    )(q, k, vt)


def _s5_weights(lam_re, lam_im, log_dt, b_re, b_im, c_re, c_im, d_skip):
    hp = lax.Precision.HIGHEST
    nl, _, ng, p = lam_re.shape
    cg = SSM_GROUP
    L = S5_CHUNK
    kd = L * cg
    dt = jnp.exp(log_dt)[..., None]
    zr = lam_re * dt
    zi = lam_im * dt

    def cpow(e):
        mag = jnp.exp(zr[..., None] * e)
        return mag * jnp.cos(zi[..., None] * e), mag * jnp.sin(zi[..., None] * e)

    lb_re, lb_im = jnp.exp(zr) * jnp.cos(zi), jnp.exp(zr) * jnp.sin(zi)
    nr, ni = lb_re - 1.0, lb_im
    den = lam_re * lam_re + lam_im * lam_im
    f_re = ((nr * lam_re + ni * lam_im) / den)[..., None]
    f_im = ((ni * lam_re - nr * lam_im) / den)[..., None]
    bb_re = f_re * b_re - f_im * b_im
    bb_im = f_re * b_im + f_im * b_re

    m_idx = jnp.arange(L + 1, dtype=F32)
    expo = jnp.stack([m_idx, L - m_idx])[None, :, None, None, :]
    pw_re, pw_im = cpow(expo)
    rep = jnp.repeat(jnp.eye(L + 1, dtype=F32), cg, axis=1)
    til = jnp.tile(jnp.eye(cg, dtype=F32), (1, L + 1))
    pwx_re = jnp.einsum('dxgpm,mn->dxgpn', pw_re, rep, precision=hp)
    pwx_im = jnp.einsum('dxgpm,mn->dxgpn', pw_im, rep, precision=hp)
    ct_re = jnp.einsum('dxgcp,cn->dxgpn', c_re, til, precision=hp)
    ct_im = jnp.einsum('dxgcp,cn->dxgpn', c_im, til, precision=hp)
    ca_re = ct_re * pwx_re - ct_im * pwx_im
    ca_im = ct_re * pwx_im + ct_im * pwx_re
    ef_re, ef_im = ca_re[:, 0, ..., :kd], ca_im[:, 0, ..., :kd]
    w2f_re, w2f_im = ca_re[:, 0, ..., cg:], ca_im[:, 0, ..., cg:]
    w2b_re, w2b_im = ca_re[:, 1, ..., :kd], ca_im[:, 1, ..., :kd]
    eb_re, eb_im = ca_re[:, 1, ..., cg:], ca_im[:, 1, ..., cg:]

    def lag_rows(x, e_re, e_im):
        return (jnp.einsum('dgpe,dgpn->dgen', bb_re[:, x], e_re, precision=hp)
                - jnp.einsum('dgpe,dgpn->dgen', bb_im[:, x], e_im, precision=hp))

    kf = lag_rows(0, ef_re, ef_im)
    kb = lag_rows(1, eb_re, eb_im)
    zeros = jnp.zeros_like(kf)
    pf = jnp.concatenate([zeros, kf], axis=-1)
    pb = jnp.concatenate([kb, zeros], axis=-1)
    tf = jnp.stack([pf[..., kd - cg * s:2 * kd - cg * s] for s in range(L)], axis=2)
    tb = jnp.stack([pb[..., cg * (L - 1 - s):cg * (L - 1 - s) + kd] for s in range(L)], axis=2)
    skip = jnp.eye(kd, dtype=F32) * jnp.tile(d_skip.reshape(nl, ng, 1, cg), (1, 1, 1, L))
    tmat = (tf + tb).reshape(nl, ng, kd, kd) + skip

    s_idx = jnp.arange(L, dtype=F32)
    sexp = jnp.stack([L - 1 - s_idx, s_idx])[None, :, None, None, :]
    ps_re, ps_im = cpow(sexp)
    w1_re = (jnp.einsum('dxgps,dxgpe->dxgsep', ps_re, bb_re)
             - jnp.einsum('dxgps,dxgpe->dxgsep', ps_im, bb_im)).reshape(nl, 2, ng, kd, p)
    w1_im = (jnp.einsum('dxgps,dxgpe->dxgsep', ps_re, bb_im)
             + jnp.einsum('dxgps,dxgpe->dxgsep', ps_im, bb_re)).reshape(nl, 2, ng, kd, p)

    wa = jnp.concatenate([tmat, w1_re[:, 0], w1_im[:, 0], w1_re[:, 1], w1_im[:, 1]], axis=-1)
    wb = jnp.concatenate([w2f_re, -w2f_im, w2b_re, -w2b_im], axis=2)
    al_re, al_im = cpow(jnp.float32(L))
    a_re = al_re.reshape(nl, 2, ng * p)
    a_im = al_im.reshape(nl, 2, ng * p)
    return wa.astype(BF16), wb.astype(BF16), a_re, a_im


GROUPS_PER_VREG = LANES // SSM_GROUP
PAIRS_PER_VREG = GROUPS_PER_VREG // S5_PAIR
TOKENS_PER_HALF = LANES // SSM_GROUP


def _granule_transpose(arrs):
    gran = lax.broadcasted_iota(jnp.int32, arrs[0].shape, 1) // SSM_GROUP
    cur = list(arrs)
    for dist in (4, 2, 1):
        keep = (gran & dist) == 0
        nxt = list(cur)
        for i in range(len(cur)):
            if i & dist:
                continue
            a, b = cur[i], cur[i + dist]
            nxt[i] = jnp.where(keep, a, pltpu.roll(b, dist * SSM_GROUP, 1))
            nxt[i + dist] = jnp.where(keep, pltpu.roll(a, LANES - dist * SSM_GROUP, 1), b)
        cur = nxt
    return cur


def _pair_lanes(a, b):
    low = lax.broadcasted_iota(jnp.int32, a.shape, 1) < LANES // 2
    return (jnp.where(low, a, pltpu.roll(b, LANES // 2, 1)),
            jnp.where(low, pltpu.roll(a, LANES // 2, 1), b))


def _s5_chunk_kernel(u_ref, wa_ref, yi_ref, sfr_ref, sfi_ref, sbr_ref, sbi_ref, *, nchunk):
    halves = []
    for jh in range(S5_CHUNK // TOKENS_PER_HALF):
        toks = [u_ref[pl.ds(jh * TOKENS_PER_HALF + jj, nchunk, stride=S5_CHUNK), :]
                for jj in range(TOKENS_PER_HALF)]
        halves.append([o.astype(BF16) for o in _granule_transpose(toks)])
    kd = S5_CHUNK * SSM_GROUP
    for qq in range(PAIRS_PER_VREG):
        res = []
        for gl in range(S5_PAIR):
            g = S5_PAIR * qq + gl
            z = jnp.concatenate([halves[jh][g] for jh in range(len(halves))], axis=-1)
            r = jnp.dot(z, wa_ref[g], preferred_element_type=F32)
            yi_ref[0, g] = r[:, :kd]
            res.append(r)
        lo = qq * LANES
        sfr_ref[:, lo:lo + LANES], sfi_ref[:, lo:lo + LANES] = _pair_lanes(
            res[0][:, kd:kd + LANES], res[1][:, kd:kd + LANES])
        sbr_ref[:, lo:lo + LANES], sbi_ref[:, lo:lo + LANES] = _pair_lanes(
            res[0][:, kd + LANES:kd + 2 * LANES], res[1][:, kd + LANES:kd + 2 * LANES])


def _s5_chunk_call(u, wa, *, batch, seq):
    t, d_ssm = u.shape
    nchunk = seq // S5_CHUNK
    nv = d_ssm // LANES
    ng, kd, ncol = wa.shape
    sw = PAIRS_PER_VREG * LANES
    st = jax.ShapeDtypeStruct((nchunk, batch * nv * sw), F32)
    sspec = pl.BlockSpec((nchunk, sw), lambda b, v: (0, b * nv + v))
    return pl.pallas_call(
        functools.partial(_s5_chunk_kernel, nchunk=nchunk),
        grid=(batch, nv),
        in_specs=[pl.BlockSpec((seq, LANES), lambda b, v: (b, v)),
                  pl.BlockSpec((GROUPS_PER_VREG, kd, ncol), lambda b, v: (v, 0, 0))],
        out_specs=[pl.BlockSpec((1, GROUPS_PER_VREG, nchunk, kd), lambda b, v: (b, v, 0, 0)),
                   sspec, sspec, sspec, sspec],
        out_shape=[jax.ShapeDtypeStruct((batch, ng, nchunk, kd), F32), st, st, st, st],
        name="s5_chunk",
        compiler_params=_cparams(("arbitrary", "arbitrary"), 40),
    )(u, wa)


def _s5scan_kernel(sfr_ref, sfi_ref, sbr_ref, sbi_ref, afr_ref, afi_ref, abr_ref, abi_ref,
                   xfr_ref, xfi_ref, xbr_ref, xbi_ref, *, nchunk):
    afr, afi = afr_ref[...], afi_ref[...]
    abr, abi = abr_ref[...], abi_ref[...]
    zero = jnp.zeros_like(afr)

    def body(k, carry):
        fr, fi, br, bi = carry
        kb = nchunk - 1 - k
        rowf = pl.ds(k, 1)
        rowb = pl.ds(kb, 1)
        xfr_ref[rowf, :] = fr
        xfi_ref[rowf, :] = fi
        xbr_ref[rowb, :] = br
        xbi_ref[rowb, :] = bi
        nfr = fr * afr - fi * afi + sfr_ref[rowf, :]
        nfi = fr * afi + fi * afr + sfi_ref[rowf, :]
        nbr = br * abr - bi * abi + sbr_ref[rowb, :]
        nbi = br * abi + bi * abr + sbi_ref[rowb, :]
        return nfr, nfi, nbr, nbi

    lax.fori_loop(0, nchunk, body, (zero, zero, zero, zero))


def _s5scan_call(s4, a4, *, nchunk, tl):
    w = s4[0].shape[1]
    sspec = pl.BlockSpec((nchunk, tl), lambda j: (0, j))
    aspec = pl.BlockSpec((1, tl), lambda j: (0, j))
    st = jax.ShapeDtypeStruct((nchunk, w), F32)
    return pl.pallas_call(
        functools.partial(_s5scan_kernel, nchunk=nchunk),
        grid=(w // tl,),
        in_specs=[sspec] * 4 + [aspec] * 4,
        out_specs=[sspec] * 4,
        out_shape=[st] * 4,
        name="s5_scan",
        compiler_params=_cparams(("arbitrary",), 32),
    )(*s4, *a4)


def _s5_readout_kernel(yi_ref, xfr_ref, xfi_ref, xbr_ref, xbi_ref, wb_ref, y_ref, *, nchunk):
    nhalf = S5_CHUNK // TOKENS_PER_HALF
    pieces = [[None] * GROUPS_PER_VREG for _ in range(nhalf)]
    for qq in range(PAIRS_PER_VREG):
        lo = qq * LANES
        xf = _pair_lanes(xfr_ref[:, lo:lo + LANES], xfi_ref[:, lo:lo + LANES])
        xb = _pair_lanes(xbr_ref[:, lo:lo + LANES], xbi_ref[:, lo:lo + LANES])
        for gl in range(S5_PAIR):
            g = S5_PAIR * qq + gl
            xs = jnp.concatenate([xf[gl], xb[gl]], axis=-1)
            y = yi_ref[0, g] + jnp.dot(xs.astype(BF16), wb_ref[g], preferred_element_type=F32)
            for jh in range(nhalf):
                pieces[jh][g] = y[:, jh * LANES:(jh + 1) * LANES]
    for jh in range(nhalf):
        toks = _granule_transpose(pieces[jh])
        for jj in range(TOKENS_PER_HALF):
            y_ref[pl.ds(jh * TOKENS_PER_HALF + jj, nchunk, stride=S5_CHUNK), :] = toks[jj]


def _s5_readout_call(yi, x4, wb, *, batch, seq):
    _, ng, nchunk, kd = yi.shape
    nv = ng // GROUPS_PER_VREG
    sw = PAIRS_PER_VREG * LANES
    xspec = pl.BlockSpec((nchunk, sw), lambda b, v: (0, b * nv + v))
    return pl.pallas_call(
        functools.partial(_s5_readout_kernel, nchunk=nchunk),
        grid=(batch, nv),
        in_specs=[pl.BlockSpec((1, GROUPS_PER_VREG, nchunk, kd), lambda b, v: (b, v, 0, 0)),
                  xspec, xspec, xspec, xspec,
                  pl.BlockSpec((GROUPS_PER_VREG, wb.shape[1], kd), lambda b, v: (v, 0, 0))],
        out_specs=pl.BlockSpec((seq, LANES), lambda b, v: (b, v)),
        out_shape=jax.ShapeDtypeStruct((batch * seq, nv * LANES), F32),
        name="s5_readout",
        compiler_params=_cparams(("arbitrary", "arbitrary"), 40),
    )(yi, *x4, wb)


def _s5_mixer(u, wa, wb, a_re, a_im, *, batch, seq):
    nchunk = seq // S5_CHUNK
    yi, sfr, sfi, sbr, sbi = _s5_chunk_call(u, wa, batch=batch, seq=seq)
    w = a_re.shape[1]
    a4 = [jnp.tile(a.reshape(1, w), (1, batch)) for a in (a_re[0], a_im[0], a_re[1], a_im[1])]
    tl = min(1024, batch * w)
    x4 = _s5scan_call([sfr, sfi, sbr, sbi], a4, nchunk=nchunk, tl=tl)
    return _s5_readout_call(yi, x4, wb, batch=batch, seq=seq)


POST_SUBTILES = 1


def _post_kernel(y_ref, at_ref, gs_ref, ga_ref, x_ref, mod_ref, ln_ref,
                 wg_ref, bg_ref, wsu_ref, wau_ref, wo_ref, wr_ref,
                 x1_ref, h2_ref, aff_ref, afft_ref, *, d_model):
    mod = mod_ref[0]
    g1 = mod[:, 2 * d_model:3 * d_model]
    sh2 = mod[:, 3 * d_model:4 * d_model]
    sc2 = mod[:, 4 * d_model:5 * d_model]
    tm = y_ref.shape[0]
    nsub = POST_SUBTILES
    rows = tm // nsub

    def glu_stage(sl, st):
        au = jnp.dot(at_ref[sl, :], wau_ref[...], preferred_element_type=F32)
        z = jax.nn.gelu(y_ref[sl, :], approximate=True)
        glu = jnp.dot(z.astype(BF16), wg_ref[...], preferred_element_type=F32) + bg_ref[...]
        return z, glu, au

    def up_stage(sl, st):
        z, glu, au = st
        ssm = z * jax.nn.sigmoid(glu)
        su = jnp.dot(ssm.astype(BF16), wsu_ref[...], preferred_element_type=F32)
        return su, au

    def out_stage(sl, st):
        su, au = st
        merged = gs_ref[sl, :].astype(F32) * su + ga_ref[sl, :].astype(F32) * au
        return jnp.dot(merged.astype(BF16), wo_ref[...], preferred_element_type=F32)

    def router_stage(sl, mix):
        x1 = x_ref[sl, :] + g1 * mix
        x1_ref[sl, :] = x1
        ms = jnp.mean(x1 * x1, axis=-1, keepdims=True)
        h2 = (x1 * lax.rsqrt(ms + EPS) * ln_ref[...]) * (1.0 + sc2) + sh2
        h2_ref[sl, :] = h2.astype(h2_ref.dtype)
        lg = lax.dot_general(wr_ref[...], h2, (((1,), (1,)), ((), ())), preferred_element_type=F32,
                             precision=lax.Precision.HIGHEST)
        lg = lg - jnp.max(lg, axis=0, keepdims=True)
        ex = jnp.exp(lg)
        aff = ex / jnp.sum(ex, axis=0, keepdims=True)
        aff_ref[0, :, sl] = aff
        pad = jnp.zeros((LANES - aff.shape[0], aff.shape[1]), F32)
        afft_ref[sl, :] = jnp.concatenate([aff, pad], axis=0).T
        return None

    stages = (glu_stage, up_stage, out_stage, router_stage)
    state = [None] * nsub
    for step in range(len(stages) + nsub - 1):
        for sub in range(nsub):
            k = step - sub
            if 0 <= k < len(stages):
                state[sub] = stages[k](slice(sub * rows, (sub + 1) * rows), state[sub])


def _post_call(y, attn, gs, ga, x2, mod3, ln_w, wg, bg, wsu, wau, wo, wr_t, *, seq, tm):
    t, d = x2.shape
    d_ssm = y.shape[1]
    d_attn = attn.shape[1]
    ne = wr_t.shape[0]
    tiles_per_seq = seq // tm
    row = lambda i: (i, 0)
    const = lambda i: (0, 0)
    return pl.pallas_call(
        functools.partial(_post_kernel, d_model=d),
        grid=(t // tm,),
        in_specs=[pl.BlockSpec((tm, d_ssm), row),
                  pl.BlockSpec((tm, d_attn), row),
                  pl.BlockSpec((tm, d), row),
                  pl.BlockSpec((tm, d), row),
                  pl.BlockSpec((tm, d), row),
                  pl.BlockSpec((1, 1, mod3.shape[2]), lambda i: (i // tiles_per_seq, 0, 0)),
                  pl.BlockSpec((1, d), const),
                  pl.BlockSpec(wg.shape, const),
                  pl.BlockSpec((1, d_ssm), const),
                  pl.BlockSpec(wsu.shape, const),
                  pl.BlockSpec(wau.shape, const),
                  pl.BlockSpec(wo.shape, const),
                  pl.BlockSpec(wr_t.shape, const)],
        out_specs=[pl.BlockSpec((tm, d), row),
                   pl.BlockSpec((tm, d), row),
                   pl.BlockSpec((1, ne, tm), lambda i: (i // tiles_per_seq, 0, i % tiles_per_seq)),
                   pl.BlockSpec((tm, LANES), row)],
        out_shape=[jax.ShapeDtypeStruct((t, d), F32),
                   jax.ShapeDtypeStruct((t, d), F32),
                   jax.ShapeDtypeStruct((t // seq, ne, seq), F32),
                   jax.ShapeDtypeStruct((t, LANES), F32)],
        name="post_mix",
        compiler_params=_cparams(("arbitrary",), 48),
    )(y, attn, gs, ga, x2, mod3, ln_w, wg, bg, wsu, wau, wo, wr_t)


def _lane_cumsum(mask_f, tri):
    ne, seq = mask_f.shape
    run = jnp.zeros((ne, 1), F32)
    parts = []
    for j in range(seq // LANES):
        blk = mask_f[:, j * LANES:(j + 1) * LANES]
        cs = jnp.dot(blk.astype(BF16), tri, preferred_element_type=F32) + run
        parts.append(cs)
        run = run + jnp.sum(blk, axis=1, keepdims=True)
    return jnp.concatenate(parts, axis=-1)


def _select_kernel(aff_ref, tri_ref, idx_ref, half_ref, *, cap, seq, ts):
    aff = aff_ref[0]
    ne = aff.shape[0]
    tri = tri_ref[...]

    def body(i, thr):
        cand = thr | (jnp.int32(1) << (30 - i))
        cnt = jnp.sum(jnp.where(aff >= pltpu.bitcast(cand, F32), 1.0, 0.0), axis=1, keepdims=True)
        return jnp.where(cnt >= cap, cand, thr)

    thr = lax.fori_loop(0, 31, body, jnp.zeros((ne, 1), jnp.int32))
    gt = aff >= pltpu.bitcast(thr + 1, F32)
    eq = (aff >= pltpu.bitcast(thr, F32)) & jnp.logical_not(gt)
    need = cap - jnp.sum(jnp.where(gt, 1.0, 0.0), axis=1, keepdims=True)
    eq_f = jnp.where(eq, 1.0, 0.0)
    eq_rank = _lane_cumsum(eq_f, tri) - eq_f
    sel = gt | (eq & (eq_rank < need))
    sel_f = jnp.where(sel, 1.0, 0.0)
    csum = _lane_cumsum(sel_f, tri)
    rank = jnp.where(sel, csum - sel_f, -1.0)
    half_ref[0] = csum[:, seq // 2 - 1:seq // 2].astype(jnp.int32)

    slot = lax.broadcasted_iota(jnp.int32, (cap, ts), 0).astype(F32)
    lane_e = lax.broadcasted_iota(jnp.int32, (cap, ne), 1)
    idx = jnp.zeros((cap, ne), F32)
    for e in range(ne):
        col = jnp.zeros((cap, 1), F32)
        for j in range(seq // ts):
            tok = (lax.broadcasted_iota(jnp.int32, (1, ts), 1) + j * ts).astype(F32)
            hit = rank[e:e + 1, j * ts:(j + 1) * ts] == slot
            col = col + jnp.sum(jnp.where(hit, tok, 0.0), axis=1, keepdims=True)
        idx = jnp.where(lane_e == e, col, idx)
    idx_ref[0] = idx.astype(jnp.int32)


def _select_call(aff_t, tri, *, cap):
    b, ne, seq = aff_t.shape
    return pl.pallas_call(
        functools.partial(_select_kernel, cap=cap, seq=seq, ts=min(1024, seq)),
        grid=(b,),
        in_specs=[pl.BlockSpec((1, ne, seq), lambda i: (i, 0, 0)),
                  pl.BlockSpec((LANES, LANES), lambda i: (0, 0))],
        out_specs=[pl.BlockSpec((1, cap, ne), lambda i: (i, 0, 0)),
                   pl.BlockSpec((1, ne, 1), lambda i: (i, 0, 0))],
        out_shape=[jax.ShapeDtypeStruct((b, cap, ne), jnp.int32),
                   jax.ShapeDtypeStruct((b, ne, 1), jnp.int32)],
        name="expert_select",
        compiler_params=_cparams(("arbitrary",), 32),
    )(aff_t, tri)


GATHER_UNROLL = 8


def _gather_kernel(idx_ref, h_ref, aff_ref, xg_ref, gc_ref, rows_ref, arow_ref, *, cap, ne):
    b = pl.program_id(0)
    e = pl.program_id(1)
    base = (b * ne + e) * cap

    def body(g, carry):
        for k in range(GATHER_UNROLL):
            c = g * GATHER_UNROLL + k
            r = idx_ref[base + c]
            rows_ref[pl.ds(c, 1), :] = h_ref[pl.ds(r, 1), :]
            arow_ref[pl.ds(c, 1), :] = aff_ref[pl.ds(r, 1), :]
        return carry

    lax.fori_loop(0, cap // GATHER_UNROLL, body, 0)
    xg_ref[0] = rows_ref[...].astype(xg_ref.dtype)
    lane = lax.broadcasted_iota(jnp.int32, arow_ref.shape, 1)
    gc_ref[0] = jnp.sum(jnp.where(lane == e, arow_ref[...], 0.0), axis=1, keepdims=True)


def _gather_call(idx_flat, h2, aff_tm, *, batch, seq, cap, ne):
    t, d = h2.shape
    grid_spec = pltpu.PrefetchScalarGridSpec(
        num_scalar_prefetch=1,
        grid=(batch, ne),
        in_specs=[pl.BlockSpec((seq, d), lambda b, e, idx: (b, 0)),
                  pl.BlockSpec((seq, LANES), lambda b, e, idx: (b, 0))],
        out_specs=[pl.BlockSpec((1, cap, d), lambda b, e, idx: (e, b, 0)),
                   pl.BlockSpec((1, cap, 1), lambda b, e, idx: (e, b, 0))],
        scratch_shapes=[pltpu.VMEM((cap, d), F32), pltpu.VMEM((cap, LANES), F32)],
    )
    return pl.pallas_call(
        functools.partial(_gather_kernel, cap=cap, ne=ne),
        grid_spec=grid_spec,
        out_shape=[jax.ShapeDtypeStruct((ne, batch * cap, d), BF16),
                   jax.ShapeDtypeStruct((ne, batch * cap, 1), F32)],
        name="expert_gather",
        compiler_params=_cparams(("arbitrary", "arbitrary"), 48),
    )(idx_flat, h2, aff_tm)


def _ffn_kernel(x_ref, gc_ref, wg_ref, wu_ref, wd_ref, ye_ref, *, rows):
    f = pl.program_id(1)

    @pl.when(f == 0)
    def _init():
        ye_ref[...] = jnp.zeros_like(ye_ref)

    wg = wg_ref[0, 0].astype(BF16)
    wu = wu_ref[0, 0].astype(BF16)
    wd = wd_ref[0, 0].astype(BF16)
    nblk = x_ref.shape[1] // rows

    def up(r):
        x = x_ref[0, r * rows:(r + 1) * rows, :]
        return (jnp.dot(x, wg, preferred_element_type=F32), jnp.dot(x, wu, preferred_element_type=F32))

    nxt = up(0)
    for r in range(nblk):
        hg, hu = nxt
        if r + 1 < nblk:
            nxt = up(r + 1)
        hid = (hg * jax.nn.sigmoid(hg) * hu).astype(BF16)
        ye_ref[0, r * rows:(r + 1) * rows, :] += jnp.dot(hid, wd, preferred_element_type=F32)

    @pl.when(f == pl.num_programs(1) - 1)
    def _gate():
        ye_ref[0] = ye_ref[0] * gc_ref[0]


def _ffn_call(xg, gc, wg, wu, wd, *, layer, tf, rows):
    ne, n, d = xg.shape
    dff = wg.shape[3]
    return pl.pallas_call(
        functools.partial(_ffn_kernel, rows=rows),
        grid=(ne, dff // tf),
        in_specs=[pl.BlockSpec((1, n, d), lambda e, f: (e, 0, 0)),
                  pl.BlockSpec((1, n, 1), lambda e, f: (e, 0, 0)),
                  pl.BlockSpec((1, 1, d, tf), lambda e, f: (layer, e, 0, f)),
                  pl.BlockSpec((1, 1, d, tf), lambda e, f: (layer, e, 0, f)),
                  pl.BlockSpec((1, 1, tf, d), lambda e, f: (layer, e, f, 0))],
        out_specs=pl.BlockSpec((1, n, d), lambda e, f: (e, 0, 0)),
        out_shape=jax.ShapeDtypeStruct((ne, n, d), F32),
        name="expert_ffn",
        compiler_params=_cparams(("arbitrary", "arbitrary"), 56),
    )(xg, gc, wg, wu, wd)


SCATTER_UNROLL = 8
SEQ_HALVES = 2


def _combine_kernel(idx_ref, half_ref, ye_ref, x1_ref, mod_ref, o_ref, acc_ref, *, cap, ne, rows,
                    d_model):
    i = pl.program_id(0)
    e = pl.program_id(1)
    b = i // SEQ_HALVES
    hh = i % SEQ_HALVES

    @pl.when(e == 0)
    def _init():
        o_ref[...] = jnp.zeros_like(o_ref)
        acc_ref[...] = jnp.zeros_like(acc_ref)

    mid = half_ref[b * ne + e]
    lo = jnp.where(hh == 0, 0, mid)
    hi = jnp.where(hh == 0, mid, cap)
    base = (b * ne + e) * cap
    off = hh * rows

    def add_row(c, dst_ref):
        r = idx_ref[base + c] - off
        dst_ref[pl.ds(r, 1), :] += ye_ref[0, pl.ds(c, 1), :]

    def group(g, carry):
        for k in range(SCATTER_UNROLL):
            add_row(lo + g * SCATTER_UNROLL + k, o_ref if k % 2 == 0 else acc_ref)
        return carry

    ngroup = (hi - lo) // SCATTER_UNROLL
    lax.fori_loop(0, ngroup, group, 0)

    def tail(c, carry):
        add_row(c, o_ref)
        return carry

    lax.fori_loop(lo + ngroup * SCATTER_UNROLL, hi, tail, 0)

    @pl.when(e == pl.num_programs(1) - 1)
    def _fin():
        g2 = mod_ref[0][:, 5 * d_model:6 * d_model]
        o_ref[...] = x1_ref[...] + g2 * (o_ref[...] + acc_ref[...])


def _combine_call(idx_flat, half_flat, ye, x1, mod3, *, batch, seq, cap):
    t, d = x1.shape
    ne = ye.shape[0]
    rows = seq // SEQ_HALVES
    grid_spec = pltpu.PrefetchScalarGridSpec(
        num_scalar_prefetch=2,
        grid=(batch * SEQ_HALVES, ne),
        in_specs=[pl.BlockSpec((1, cap, d), lambda i, e, idx, half: (e, i // SEQ_HALVES, 0)),
                  pl.BlockSpec((rows, d), lambda i, e, idx, half: (i, 0)),
                  pl.BlockSpec((1, 1, mod3.shape[2]), lambda i, e, idx, half: (i // SEQ_HALVES, 0, 0))],
        out_specs=pl.BlockSpec((rows, d), lambda i, e, idx, half: (i, 0)),
        scratch_shapes=[pltpu.VMEM((rows, d), F32)],
    )
    return pl.pallas_call(
        functools.partial(_combine_kernel, cap=cap, ne=ne, rows=rows, d_model=d),
        grid_spec=grid_spec,
        out_shape=jax.ShapeDtypeStruct((t, d), F32),
        name="moe_combine",
        compiler_params=_cparams(("arbitrary", "arbitrary"), 52),
    )(idx_flat, half_flat, ye, x1, mod3)


def kernel(x, c, w_mod, b_mod, ln1_w, ln2_w, w_in, ssm_lam_re, ssm_lam_im, ssm_log_dt, ssm_b_re,
           ssm_b_im, ssm_c_re, ssm_c_im, ssm_d, w_glu, b_glu, q_norm_w, k_norm_w, w_ssm_up,
           w_attn_up, w_out, w_router, w_exp_gate, w_exp_up, w_exp_down):
    batch, seq, d = x.shape
    depth = w_mod.shape[0]
    t = batch * seq
    d_attn = N_Q_HEADS * HEAD_DIM
    d_kv = N_KV_HEADS * HEAD_DIM
    cap = CAPACITY_FACTOR * seq // N_EXPERTS
    tm = min(512, seq)
    tq = min(256, seq)

    tabs = _rope_tables(seq)
    head_id = jnp.arange(d_attn) // HEAD_DIM
    hm = (head_id[:, None] == head_id[None, :]).astype(BF16)
    tri = (jnp.arange(LANES)[:, None] <= jnp.arange(LANES)[None, :]).astype(BF16)
    c_pad = jnp.zeros((8, d), F32).at[:batch].set(c)
    mod_all = _mod_call(c_pad, w_mod, b_mod)[:, :batch]
    wa_all, wb_all, a_re_all, a_im_all = _s5_weights(ssm_lam_re, ssm_lam_im, ssm_log_dt, ssm_b_re,
                                                     ssm_b_im, ssm_c_re, ssm_c_im, ssm_d)

    x2 = x.reshape(t, d)
    for l in range(depth):
        mod3 = mod_all[l].reshape(batch, 1, 6 * d)
        qw = jnp.tile(q_norm_w[l], N_Q_HEADS).reshape(1, d_attn)
        kw = jnp.tile(k_norm_w[l], N_KV_HEADS).reshape(1, d_kv)
        u, q, k, v, gs, ga = _in_call(x2, mod3, ln1_w[l].reshape(1, d), w_in[l].astype(BF16), tabs,
                                      qw, kw, hm, seq=seq, tm=tm)
        y = _s5_mixer(u, wa_all[l], wb_all[l], a_re_all[l], a_im_all[l], batch=batch, seq=seq)
        attn = _attn_call(q, k, v, seq=seq, tq=tq)
        x1, h2, aff_t, aff_tm = _post_call(
            y, attn, gs, ga, x2, mod3, ln2_w[l].reshape(1, d), w_glu[l].astype(BF16),
            b_glu[l].reshape(1, -1), w_ssm_up[l].astype(BF16), w_attn_up[l].astype(BF16),
            w_out[l].astype(BF16), w_router[l].T, seq=seq, tm=tm)
        idx_t, half = _select_call(aff_t, tri, cap=cap)
        idx_flat = jnp.swapaxes(idx_t, 1, 2).reshape(-1)
        half_flat = half.reshape(-1)
        xg, gc = _gather_call(idx_flat, h2, aff_tm, batch=batch, seq=seq, cap=cap, ne=N_EXPERTS)
        ye = _ffn_call(xg, gc, w_exp_gate, w_exp_up, w_exp_down, layer=l, tf=512,
                       rows=min(1024, batch * cap))
        x2 = _combine_call(idx_flat, half_flat, ye, x1, mod3, batch=batch, seq=seq, cap=cap)
    return x2.reshape(batch, seq, d)
```

```python
import functools
import math

import jax
import jax.numpy as jnp
from jax import lax
from jax.experimental import pallas as pl
from jax.experimental.pallas import tpu as pltpu

F32 = jnp.float32
BF16 = jnp.bfloat16

HEAD_DIM = 64
N_Q_HEADS = 8
N_KV_HEADS = 2
Q_PER_KV = N_Q_HEADS // N_KV_HEADS
ROT_HALF = HEAD_DIM // 2
ROPE_THETA = 10000.0
GRID_W = 64
SSM_GROUP = 16
SSM_STATE = 64
N_EXPERTS = 16
CAPACITY_FACTOR = 2
EPS = 1e-6

S5_CHUNK = 16
S5_PAIR = 2
LANES = 128
V7X_VMEM_BYTES = 64 * 1024 * 1024
LOG2E = 1.4426950408889634


def _cparams(semantics, vmem_mb):
    return pltpu.CompilerParams(dimension_semantics=semantics,
                                vmem_limit_bytes=int(vmem_mb * 1024 * 1024))


def _mod_kernel(c_ref, w_ref, b_ref, o_ref):
    c = c_ref[...]
    ca = c * jax.nn.sigmoid(c)
    o_ref[0] = jnp.dot(ca, w_ref[0], preferred_element_type=F32,
                       precision=lax.Precision.HIGHEST) + b_ref[0]


def _mod_call(c_pad, w_mod, b_mod):
    depth, d, n = w_mod.shape
    rows = c_pad.shape[0]
    tn = 1536
    return pl.pallas_call(
        _mod_kernel,
        grid=(depth, n // tn),
        in_specs=[pl.BlockSpec((rows, d), lambda l, j: (0, 0)),
                  pl.BlockSpec((1, d, tn), lambda l, j: (l, 0, j)),
                  pl.BlockSpec((1, 1, tn), lambda l, j: (l, 0, j))],
        out_specs=pl.BlockSpec((1, rows, tn), lambda l, j: (l, 0, j)),
        out_shape=jax.ShapeDtypeStruct((depth, rows, n), F32),
        name="adaln_mod",
        compiler_params=_cparams(("arbitrary", "arbitrary"), 32),
    )(c_pad, w_mod, b_mod.reshape(depth, 1, n))


def _head_mean_square(xsq, hm):
    hi = xsq.astype(BF16)
    lo = (xsq - hi.astype(F32)).astype(BF16)
    s = jnp.dot(hi, hm, preferred_element_type=F32) + jnp.dot(lo, hm, preferred_element_type=F32)
    return s * (1.0 / HEAD_DIM)


def _rope_block(blk, cos, sa, sb):
    return (blk * cos + pltpu.roll(blk, LANES - ROT_HALF // 2, 1) * sa
            + pltpu.roll(blk, ROT_HALF // 2, 1) * sb)


def _in_kernel(x_ref, mod_ref, ln_ref, w_ref, cos_ref, sa_ref, sb_ref, qw_ref, kw_ref, hm_ref,
               u_ref, q_ref, k_ref, v_ref, gs_ref, ga_ref, *, d_model, d_ssm, d_attn, d_kv):
    x = x_ref[...]
    mod = mod_ref[0]
    sh = mod[:, 0:d_model]
    sc = mod[:, d_model:2 * d_model]
    ms = jnp.mean(x * x, axis=-1, keepdims=True)
    h = (x * lax.rsqrt(ms + EPS) * ln_ref[...]) * (1.0 + sc) + sh
    hb = h.astype(BF16)
    hm = hm_ref[...]
    cos = cos_ref[...]
    sa = sa_ref[...]
    sb = sb_ref[...]
    qscale = (HEAD_DIM ** -0.5) * LOG2E

    def finish_u(p):
        u_ref[...] = p.astype(u_ref.dtype)

    def finish_q(q):
        qn = q * lax.rsqrt(_head_mean_square(q * q, hm) + EPS) * qw_ref[...]
        qr = [_rope_block(qn[:, j * LANES:(j + 1) * LANES], cos, sa, sb) for j in range(d_attn // LANES)]
        q_ref[...] = (jnp.concatenate(qr, axis=-1) * qscale).astype(q_ref.dtype)

    def finish_kv(p):
        k = p[:, :d_kv]
        v_ref[0] = p[:, d_kv:].T.astype(v_ref.dtype)
        kn = k * lax.rsqrt(_head_mean_square(k * k, hm[:d_kv, :d_kv]) + EPS) * kw_ref[...]
        kr = [_rope_block(kn[:, j * LANES:(j + 1) * LANES], cos, sa, sb) for j in range(d_kv // LANES)]
        k_ref[...] = jnp.concatenate(kr, axis=-1).astype(k_ref.dtype)

    def finish_gs(p):
        gs_ref[...] = jax.nn.sigmoid(p).astype(gs_ref.dtype)

    def finish_ga(p):
        ga_ref[...] = jax.nn.sigmoid(p).astype(ga_ref.dtype)

    groups = ((d_ssm, finish_u), (d_attn, finish_q), (2 * d_kv, finish_kv),
              (d_model, finish_gs), (d_model, finish_ga))
    starts = [sum(g[0] for g in groups[:i]) for i in range(len(groups))]

    def project(i):
        return jnp.dot(hb, w_ref[:, starts[i]:starts[i] + groups[i][0]], preferred_element_type=F32)

    nxt = project(0)
    for i, (_, finish) in enumerate(groups):
        cur = nxt
        if i + 1 < len(groups):
            nxt = project(i + 1)
        finish(cur)


def _in_call(x2, mod3, ln_w, w_in, tabs, qw, kw, hm, *, seq, tm):
    t, d = x2.shape
    n_in = w_in.shape[1]
    d_ssm = d // 2
    d_attn = N_Q_HEADS * HEAD_DIM
    d_kv = N_KV_HEADS * HEAD_DIM
    tiles_per_seq = seq // tm
    cos, sa, sb = tabs
    kern = functools.partial(_in_kernel, d_model=d, d_ssm=d_ssm, d_attn=d_attn, d_kv=d_kv)
    row = lambda i: (i, 0)
    const = lambda i: (0, 0)
    tab = lambda i: (i % tiles_per_seq, 0)
    return pl.pallas_call(
        kern,
        grid=(t // tm,),
        in_specs=[pl.BlockSpec((tm, d), row),
                  pl.BlockSpec((1, 1, mod3.shape[2]), lambda i: (i // tiles_per_seq, 0, 0)),
                  pl.BlockSpec((1, d), const),
                  pl.BlockSpec((d, n_in), const),
                  pl.BlockSpec((tm, LANES), tab),
                  pl.BlockSpec((tm, LANES), tab),
                  pl.BlockSpec((tm, LANES), tab),
                  pl.BlockSpec((1, d_attn), const),
                  pl.BlockSpec((1, d_kv), const),
                  pl.BlockSpec((d_attn, d_attn), const)],
        out_specs=[pl.BlockSpec((tm, d_ssm), row),
                   pl.BlockSpec((tm, d_attn), row),
                   pl.BlockSpec((tm, d_kv), row),
                   pl.BlockSpec((1, d_kv, tm), lambda i: (i // tiles_per_seq, 0, i % tiles_per_seq)),
                   pl.BlockSpec((tm, d), row),
                   pl.BlockSpec((tm, d), row)],
        out_shape=[jax.ShapeDtypeStruct((t, d_ssm), F32),
                   jax.ShapeDtypeStruct((t, d_attn), BF16),
                   jax.ShapeDtypeStruct((t, d_kv), BF16),
                   jax.ShapeDtypeStruct((t // seq, d_kv, seq), BF16),
                   jax.ShapeDtypeStruct((t, d), BF16),
                   jax.ShapeDtypeStruct((t, d), BF16)],
        name="in_proj",
        compiler_params=_cparams(("arbitrary",), 48),
    )(x2, mod3, ln_w, w_in, cos, sa, sb, qw, kw, hm)


def _rope_tables(seq):
    pos = jnp.arange(seq)
    row = (pos // GRID_W).astype(F32)
    col = (pos % GRID_W).astype(F32)
    inv = 1.0 / (ROPE_THETA ** (jnp.arange(0, ROT_HALF, 2, dtype=F32) / ROT_HALF))
    ang_r = row[:, None] * inv[None, :]
    ang_c = col[:, None] * inv[None, :]
    zeros = jnp.zeros_like(ang_r)
    cos_h = jnp.concatenate([jnp.cos(ang_r), jnp.cos(ang_r), jnp.cos(ang_c), jnp.cos(ang_c)], axis=-1)
    sa_h = jnp.concatenate([-jnp.sin(ang_r), zeros, -jnp.sin(ang_c), zeros], axis=-1)
    sb_h = jnp.concatenate([zeros, jnp.sin(ang_r), zeros, jnp.sin(ang_c)], axis=-1)
    rep = LANES // HEAD_DIM
    return tuple(jnp.tile(a, (1, rep)) for a in (cos_h, sa_h, sb_h))


def _attn_heads_out(o_t, tq):
    return [o_t[:, g * tq:(g + 1) * tq].T for g in range(Q_PER_KV)]


def _attn_kernel(q_ref, k_ref, vt_ref, o_ref, *, kb):
    tq = q_ref.shape[0]
    seq = k_ref.shape[0]
    nblk = seq // kb
    width = Q_PER_KV * HEAD_DIM
    q_ts = []
    for kvh in range(N_KV_HEADS):
        qt = q_ref[:, kvh * width:(kvh + 1) * width].astype(F32).T.astype(BF16)
        q_ts.append(jnp.concatenate([qt[g * HEAD_DIM:(g + 1) * HEAD_DIM] for g in range(Q_PER_KV)],
                                    axis=1))

    def scores(kvh, j):
        lo = kvh * HEAD_DIM
        return jnp.dot(k_ref[j * kb:(j + 1) * kb, lo:lo + HEAD_DIM], q_ts[kvh],
                       preferred_element_type=F32)

    def weighted_v(kvh, j, p):
        lo = kvh * HEAD_DIM
        return jnp.dot(vt_ref[0, lo:lo + HEAD_DIM, j * kb:(j + 1) * kb], p.astype(BF16),
                       preferred_element_type=F32)

    outs = []
    worst = jnp.zeros((1, 1), F32)
    for kvh in range(N_KV_HEADS):
        s0 = scores(kvh, 0)
        m0 = jnp.max(s0, axis=0, keepdims=True)
        l = jnp.zeros_like(m0)
        acc = jnp.zeros((HEAD_DIM, m0.shape[1]), F32)
        s_cur = s0
        for j in range(nblk):
            s_next = scores(kvh, j + 1) if j + 1 < nblk else None
            p = jnp.exp2(s_cur - m0)
            l = l + jnp.sum(p, axis=0, keepdims=True)
            acc = acc + weighted_v(kvh, j, p)
            s_cur = s_next
        worst = jnp.maximum(worst, jnp.max(l, axis=1, keepdims=True))
        outs.extend(_attn_heads_out(acc / l, tq))
    o_ref[...] = jnp.concatenate(outs, axis=-1).astype(o_ref.dtype)

    @pl.when(worst[0, 0] > jnp.finfo(F32).max)
    def _exact():
        outs = []
        for kvh in range(N_KV_HEADS):
            m = jnp.full((1, Q_PER_KV * tq), -jnp.inf, F32)
            l = jnp.zeros_like(m)
            acc = jnp.zeros((HEAD_DIM, Q_PER_KV * tq), F32)
            for j in range(nblk):
                s = scores(kvh, j)
                m_new = jnp.maximum(m, jnp.max(s, axis=0, keepdims=True))
                alpha = jnp.exp2(m - m_new)
                p = jnp.exp2(s - m_new)
                l = l * alpha + jnp.sum(p, axis=0, keepdims=True)
                acc = acc * alpha + weighted_v(kvh, j, p)
                m = m_new
            outs.extend(_attn_heads_out(acc / l, tq))
        o_ref[...] = jnp.concatenate(outs, axis=-1).astype(o_ref.dtype)


def _attn_call(q, k, vt, *, seq, tq):
    t, d_attn = q.shape
    d_kv = k.shape[1]
    nq = seq // tq
    return pl.pallas_call(
        functools.partial(_attn_kernel, kb=min(256, seq)),
        grid=(t // seq, nq),
        in_specs=[pl.BlockSpec((tq, d_attn), lambda b, i: (b * nq + i, 0)),
                  pl.BlockSpec((seq, d_kv), lambda b, i: (b, 0)),
                  pl.BlockSpec((1, d_kv, seq), lambda b, i: (b, 0, 0))],
        out_specs=pl.BlockSpec((tq, d_attn), lambda b, i: (b * nq + i, 0)),
        out_shape=jax.ShapeDtypeStruct((t, d_attn), BF16),
        name="attention",
        compiler_params=_cparams(("arbitrary", "arbitrary"), 48),
    )(q, k, vt)


def _s5_weights(lam_re, lam_im, log_dt, b_re, b_im, c_re, c_im, d_skip):
    hp = lax.Precision.HIGHEST
    nl, _, ng, p = lam_re.shape
    cg = SSM_GROUP
    L = S5_CHUNK
    kd = L * cg
    dt = jnp.exp(log_dt)[..., None]
    zr = lam_re * dt
    zi = lam_im * dt

    def cpow(e):
        mag = jnp.exp(zr[..., None] * e)
        return mag * jnp.cos(zi[..., None] * e), mag * jnp.sin(zi[..., None] * e)

    lb_re, lb_im = jnp.exp(zr) * jnp.cos(zi), jnp.exp(zr) * jnp.sin(zi)
    nr, ni = lb_re - 1.0, lb_im
    den = lam_re * lam_re + lam_im * lam_im
    f_re = ((nr * lam_re + ni * lam_im) / den)[..., None]
    f_im = ((ni * lam_re - nr * lam_im) / den)[..., None]
    bb_re = f_re * b_re - f_im * b_im
    bb_im = f_re * b_im + f_im * b_re

    m_idx = jnp.arange(L + 1, dtype=F32)
    expo = jnp.stack([m_idx, L - m_idx])[None, :, None, None, :]
    pw_re, pw_im = cpow(expo)
    rep = jnp.repeat(jnp.eye(L + 1, dtype=F32), cg, axis=1)
    til = jnp.tile(jnp.eye(cg, dtype=F32), (1, L + 1))
    pwx_re = jnp.einsum('dxgpm,mn->dxgpn', pw_re, rep, precision=hp)
    pwx_im = jnp.einsum('dxgpm,mn->dxgpn', pw_im, rep, precision=hp)
    ct_re = jnp.einsum('dxgcp,cn->dxgpn', c_re, til, precision=hp)
    ct_im = jnp.einsum('dxgcp,cn->dxgpn', c_im, til, precision=hp)
    ca_re = ct_re * pwx_re - ct_im * pwx_im
    ca_im = ct_re * pwx_im + ct_im * pwx_re
    ef_re, ef_im = ca_re[:, 0, ..., :kd], ca_im[:, 0, ..., :kd]
    w2f_re, w2f_im = ca_re[:, 0, ..., cg:], ca_im[:, 0, ..., cg:]
    w2b_re, w2b_im = ca_re[:, 1, ..., :kd], ca_im[:, 1, ..., :kd]
    eb_re, eb_im = ca_re[:, 1, ..., cg:], ca_im[:, 1, ..., cg:]

    def lag_rows(x, e_re, e_im):
        return (jnp.einsum('dgpe,dgpn->dgen', bb_re[:, x], e_re, precision=hp)
                - jnp.einsum('dgpe,dgpn->dgen', bb_im[:, x], e_im, precision=hp))

    kf = lag_rows(0, ef_re, ef_im)
    kb = lag_rows(1, eb_re, eb_im)
    zeros = jnp.zeros_like(kf)
    pf = jnp.concatenate([zeros, kf], axis=-1)
    pb = jnp.concatenate([kb, zeros], axis=-1)
    tf = jnp.stack([pf[..., kd - cg * s:2 * kd - cg * s] for s in range(L)], axis=2)
    tb = jnp.stack([pb[..., cg * (L - 1 - s):cg * (L - 1 - s) + kd] for s in range(L)], axis=2)
    skip = jnp.eye(kd, dtype=F32) * jnp.tile(d_skip.reshape(nl, ng, 1, cg), (1, 1, 1, L))
    tmat = (tf + tb).reshape(nl, ng, kd, kd) + skip

    s_idx = jnp.arange(L, dtype=F32)
    sexp = jnp.stack([L - 1 - s_idx, s_idx])[None, :, None, None, :]
    ps_re, ps_im = cpow(sexp)
    w1_re = (jnp.einsum('dxgps,dxgpe->dxgsep', ps_re, bb_re)
             - jnp.einsum('dxgps,dxgpe->dxgsep', ps_im, bb_im)).reshape(nl, 2, ng, kd, p)
    w1_im = (jnp.einsum('dxgps,dxgpe->dxgsep', ps_re, bb_im)
             + jnp.einsum('dxgps,dxgpe->dxgsep', ps_im, bb_re)).reshape(nl, 2, ng, kd, p)

    wa = jnp.concatenate([tmat, w1_re[:, 0], w1_im[:, 0], w1_re[:, 1], w1_im[:, 1]], axis=-1)
    wb = jnp.concatenate([w2f_re, -w2f_im, w2b_re, -w2b_im], axis=2)
    al_re, al_im = cpow(jnp.float32(L))
    a_re = al_re.reshape(nl, 2, ng * p)
    a_im = al_im.reshape(nl, 2, ng * p)
    return wa.astype(BF16), wb.astype(BF16), a_re, a_im


GROUPS_PER_VREG = LANES // SSM_GROUP
PAIRS_PER_VREG = GROUPS_PER_VREG // S5_PAIR
TOKENS_PER_HALF = LANES // SSM_GROUP


def _granule_transpose(arrs):
    gran = lax.broadcasted_iota(jnp.int32, arrs[0].shape, 1) // SSM_GROUP
    cur = list(arrs)
    for dist in (4, 2, 1):
        keep = (gran & dist) == 0
        nxt = list(cur)
        for i in range(len(cur)):
            if i & dist:
                continue
            a, b = cur[i], cur[i + dist]
            nxt[i] = jnp.where(keep, a, pltpu.roll(b, dist * SSM_GROUP, 1))
            nxt[i + dist] = jnp.where(keep, pltpu.roll(a, LANES - dist * SSM_GROUP, 1), b)
        cur = nxt
    return cur


def _pair_lanes(a, b):
    low = lax.broadcasted_iota(jnp.int32, a.shape, 1) < LANES // 2
    return (jnp.where(low, a, pltpu.roll(b, LANES // 2, 1)),
            jnp.where(low, pltpu.roll(a, LANES // 2, 1), b))


def _s5_chunk_kernel(u_ref, wa_ref, yi_ref, sfr_ref, sfi_ref, sbr_ref, sbi_ref, *, nchunk):
    halves = []
    for jh in range(S5_CHUNK // TOKENS_PER_HALF):
        toks = [u_ref[pl.ds(jh * TOKENS_PER_HALF + jj, nchunk, stride=S5_CHUNK), :]
                for jj in range(TOKENS_PER_HALF)]
        halves.append([o.astype(BF16) for o in _granule_transpose(toks)])
    kd = S5_CHUNK * SSM_GROUP
    for qq in range(PAIRS_PER_VREG):
        res = []
        for gl in range(S5_PAIR):
            g = S5_PAIR * qq + gl
            z = jnp.concatenate([halves[jh][g] for jh in range(len(halves))], axis=-1)
            r = jnp.dot(z, wa_ref[g], preferred_element_type=F32)
            yi_ref[0, g] = r[:, :kd]
            res.append(r)
        lo = qq * LANES
        sfr_ref[:, lo:lo + LANES], sfi_ref[:, lo:lo + LANES] = _pair_lanes(
            res[0][:, kd:kd + LANES], res[1][:, kd:kd + LANES])
        sbr_ref[:, lo:lo + LANES], sbi_ref[:, lo:lo + LANES] = _pair_lanes(
            res[0][:, kd + LANES:kd + 2 * LANES], res[1][:, kd + LANES:kd + 2 * LANES])


def _s5_chunk_call(u, wa, *, batch, seq):
    t, d_ssm = u.shape
    nchunk = seq // S5_CHUNK
    nv = d_ssm // LANES
    ng, kd, ncol = wa.shape
    sw = PAIRS_PER_VREG * LANES
    st = jax.ShapeDtypeStruct((nchunk, batch * nv * sw), F32)
    sspec = pl.BlockSpec((nchunk, sw), lambda b, v: (0, b * nv + v))
    return pl.pallas_call(
        functools.partial(_s5_chunk_kernel, nchunk=nchunk),
        grid=(batch, nv),
        in_specs=[pl.BlockSpec((seq, LANES), lambda b, v: (b, v)),
                  pl.BlockSpec((GROUPS_PER_VREG, kd, ncol), lambda b, v: (v, 0, 0))],
        out_specs=[pl.BlockSpec((1, GROUPS_PER_VREG, nchunk, kd), lambda b, v: (b, v, 0, 0)),
                   sspec, sspec, sspec, sspec],
        out_shape=[jax.ShapeDtypeStruct((batch, ng, nchunk, kd), F32), st, st, st, st],
        name="s5_chunk",
        compiler_params=_cparams(("arbitrary", "arbitrary"), 40),
    )(u, wa)


def _s5scan_kernel(sfr_ref, sfi_ref, sbr_ref, sbi_ref, afr_ref, afi_ref, abr_ref, abi_ref,
                   xfr_ref, xfi_ref, xbr_ref, xbi_ref, *, nchunk):
    afr, afi = afr_ref[...], afi_ref[...]
    abr, abi = abr_ref[...], abi_ref[...]
    zero = jnp.zeros_like(afr)

    def body(k, carry):
        fr, fi, br, bi = carry
        kb = nchunk - 1 - k
        rowf = pl.ds(k, 1)
        rowb = pl.ds(kb, 1)
        xfr_ref[rowf, :] = fr
        xfi_ref[rowf, :] = fi
        xbr_ref[rowb, :] = br
        xbi_ref[rowb, :] = bi
        nfr = fr * afr - fi * afi + sfr_ref[rowf, :]
        nfi = fr * afi + fi * afr + sfi_ref[rowf, :]
        nbr = br * abr - bi * abi + sbr_ref[rowb, :]
        nbi = br * abi + bi * abr + sbi_ref[rowb, :]
        return nfr, nfi, nbr, nbi

    lax.fori_loop(0, nchunk, body, (zero, zero, zero, zero))


def _s5scan_call(s4, a4, *, nchunk, tl):
    w = s4[0].shape[1]
    sspec = pl.BlockSpec((nchunk, tl), lambda j: (0, j))
    aspec = pl.BlockSpec((1, tl), lambda j: (0, j))
    st = jax.ShapeDtypeStruct((nchunk, w), F32)
    return pl.pallas_call(
        functools.partial(_s5scan_kernel, nchunk=nchunk),
        grid=(w // tl,),
        in_specs=[sspec] * 4 + [aspec] * 4,
        out_specs=[sspec] * 4,
        out_shape=[st] * 4,
        name="s5_scan",
        compiler_params=_cparams(("arbitrary",), 32),
    )(*s4, *a4)


def _s5_readout_kernel(yi_ref, xfr_ref, xfi_ref, xbr_ref, xbi_ref, wb_ref, y_ref, *, nchunk):
    nhalf = S5_CHUNK // TOKENS_PER_HALF
    pieces = [[None] * GROUPS_PER_VREG for _ in range(nhalf)]
    for qq in range(PAIRS_PER_VREG):
        lo = qq * LANES
        xf = _pair_lanes(xfr_ref[:, lo:lo + LANES], xfi_ref[:, lo:lo + LANES])
        xb = _pair_lanes(xbr_ref[:, lo:lo + LANES], xbi_ref[:, lo:lo + LANES])
        for gl in range(S5_PAIR):
            g = S5_PAIR * qq + gl
            xs = jnp.concatenate([xf[gl], xb[gl]], axis=-1)
            y = yi_ref[0, g] + jnp.dot(xs.astype(BF16), wb_ref[g], preferred_element_type=F32)
            for jh in range(nhalf):
                pieces[jh][g] = y[:, jh * LANES:(jh + 1) * LANES]
    for jh in range(nhalf):
        toks = _granule_transpose(pieces[jh])
        for jj in range(TOKENS_PER_HALF):
            y_ref[pl.ds(jh * TOKENS_PER_HALF + jj, nchunk, stride=S5_CHUNK), :] = toks[jj]


def _s5_readout_call(yi, x4, wb, *, batch, seq):
    _, ng, nchunk, kd = yi.shape
    nv = ng // GROUPS_PER_VREG
    sw = PAIRS_PER_VREG * LANES
    xspec = pl.BlockSpec((nchunk, sw), lambda b, v: (0, b * nv + v))
    return pl.pallas_call(
        functools.partial(_s5_readout_kernel, nchunk=nchunk),
        grid=(batch, nv),
        in_specs=[pl.BlockSpec((1, GROUPS_PER_VREG, nchunk, kd), lambda b, v: (b, v, 0, 0)),
                  xspec, xspec, xspec, xspec,
                  pl.BlockSpec((GROUPS_PER_VREG, wb.shape[1], kd), lambda b, v: (v, 0, 0))],
        out_specs=pl.BlockSpec((seq, LANES), lambda b, v: (b, v)),
        out_shape=jax.ShapeDtypeStruct((batch * seq, nv * LANES), F32),
        name="s5_readout",
        compiler_params=_cparams(("arbitrary", "arbitrary"), 40),
    )(yi, *x4, wb)


def _s5_mixer(u, wa, wb, a_re, a_im, *, batch, seq):
    nchunk = seq // S5_CHUNK
    yi, sfr, sfi, sbr, sbi = _s5_chunk_call(u, wa, batch=batch, seq=seq)
    w = a_re.shape[1]
    a4 = [jnp.tile(a.reshape(1, w), (1, batch)) for a in (a_re[0], a_im[0], a_re[1], a_im[1])]
    tl = min(1024, batch * w)
    x4 = _s5scan_call([sfr, sfi, sbr, sbi], a4, nchunk=nchunk, tl=tl)
    return _s5_readout_call(yi, x4, wb, batch=batch, seq=seq)


POST_SUBTILES = 1
ROW_TILE = 8


def _post_kernel(y_ref, at_ref, gs_ref, ga_ref, x_ref, mod_ref, ln_ref,
                 wg_ref, bg_ref, wsu_ref, wau_ref, wo_ref, wr_ref,
                 x1_ref, h2_ref, aff_ref, afft_ref, *, d_model):
    mod = mod_ref[0]
    g1 = mod[:, 2 * d_model:3 * d_model]
    sh2 = mod[:, 3 * d_model:4 * d_model]
    sc2 = mod[:, 4 * d_model:5 * d_model]
    tm = y_ref.shape[0]
    nsub = POST_SUBTILES
    rows = tm // nsub

    def glu_stage(sl, st):
        au = jnp.dot(at_ref[sl, :], wau_ref[...], preferred_element_type=F32)
        z = jax.nn.gelu(y_ref[sl, :], approximate=True)
        glu = jnp.dot(z.astype(BF16), wg_ref[...], preferred_element_type=F32) + bg_ref[...]
        return z, glu, au

    def up_stage(sl, st):
        z, glu, au = st
        ssm = z * jax.nn.sigmoid(glu)
        su = jnp.dot(ssm.astype(BF16), wsu_ref[...], preferred_element_type=F32)
        return su, au

    def out_stage(sl, st):
        su, au = st
        merged = gs_ref[sl, :].astype(F32) * su + ga_ref[sl, :].astype(F32) * au
        return jnp.dot(merged.astype(BF16), wo_ref[...], preferred_element_type=F32)

    def router_stage(sl, mix):
        x1 = x_ref[sl, :] + g1 * mix
        x1_ref[sl, :] = x1
        ms = jnp.mean(x1 * x1, axis=-1, keepdims=True)
        h2 = (x1 * lax.rsqrt(ms + EPS) * ln_ref[...]) * (1.0 + sc2) + sh2
        for i in range(d_model // LANES):
            h2_ref[pl.ds(sl.start * ROW_TILE + i, sl.stop - sl.start, stride=ROW_TILE), :] = (
                h2[:, i * LANES:(i + 1) * LANES])
        lg = lax.dot_general(wr_ref[...], h2, (((1,), (1,)), ((), ())), preferred_element_type=F32,
                             precision=lax.Precision.HIGHEST)
        lg = lg - jnp.max(lg, axis=0, keepdims=True)
        ex = jnp.exp(lg)
        aff = ex / jnp.sum(ex, axis=0, keepdims=True)
        aff_ref[0, :, sl] = aff
        pad = jnp.zeros((LANES - aff.shape[0], aff.shape[1]), F32)
        afft_ref[sl, :] = jnp.concatenate([aff, pad], axis=0).T
        return None

    stages = (glu_stage, up_stage, out_stage, router_stage)
    state = [None] * nsub
    for step in range(len(stages) + nsub - 1):
        for sub in range(nsub):
            k = step - sub
            if 0 <= k < len(stages):
                state[sub] = stages[k](slice(sub * rows, (sub + 1) * rows), state[sub])


def _post_call(y, attn, gs, ga, x2, mod3, ln_w, wg, bg, wsu, wau, wo, wr_t, *, seq, tm):
    t, d = x2.shape
    d_ssm = y.shape[1]
    d_attn = attn.shape[1]
    ne = wr_t.shape[0]
    tiles_per_seq = seq // tm
    row = lambda i: (i, 0)
    const = lambda i: (0, 0)
    return pl.pallas_call(
        functools.partial(_post_kernel, d_model=d),
        grid=(t // tm,),
        in_specs=[pl.BlockSpec((tm, d_ssm), row),
                  pl.BlockSpec((tm, d_attn), row),
                  pl.BlockSpec((tm, d), row),
                  pl.BlockSpec((tm, d), row),
                  pl.BlockSpec((tm, d), row),
                  pl.BlockSpec((1, 1, mod3.shape[2]), lambda i: (i // tiles_per_seq, 0, 0)),
                  pl.BlockSpec((1, d), const),
                  pl.BlockSpec(wg.shape, const),
                  pl.BlockSpec((1, d_ssm), const),
                  pl.BlockSpec(wsu.shape, const),
                  pl.BlockSpec(wau.shape, const),
                  pl.BlockSpec(wo.shape, const),
                  pl.BlockSpec(wr_t.shape, const)],
        out_specs=[pl.BlockSpec((tm, d), row),
                   pl.BlockSpec((tm * ROW_TILE, LANES), row),
                   pl.BlockSpec((1, ne, tm), lambda i: (i // tiles_per_seq, 0, i % tiles_per_seq)),
                   pl.BlockSpec((tm, LANES), row)],
        out_shape=[jax.ShapeDtypeStruct((t, d), F32),
                   jax.ShapeDtypeStruct((t * ROW_TILE, LANES), F32),
                   jax.ShapeDtypeStruct((t // seq, ne, seq), F32),
                   jax.ShapeDtypeStruct((t, LANES), F32)],
        name="post_mix",
        compiler_params=_cparams(("arbitrary",), 48),
    )(y, attn, gs, ga, x2, mod3, ln_w, wg, bg, wsu, wau, wo, wr_t)


def _lane_cumsum(mask_f, tri):
    ne, seq = mask_f.shape
    run = jnp.zeros((ne, 1), F32)
    parts = []
    for j in range(seq // LANES):
        blk = mask_f[:, j * LANES:(j + 1) * LANES]
        cs = jnp.dot(blk.astype(BF16), tri, preferred_element_type=F32) + run
        parts.append(cs)
        run = run + jnp.sum(blk, axis=1, keepdims=True)
    return jnp.concatenate(parts, axis=-1)


def _select_kernel(aff_ref, tri_ref, idx_ref, half_ref, *, cap, seq, ts):
    aff = aff_ref[0]
    ne = aff.shape[0]
    tri = tri_ref[...]

    def body(i, thr):
        cand = thr | (jnp.int32(1) << (30 - i))
        cnt = jnp.sum(jnp.where(aff >= pltpu.bitcast(cand, F32), 1.0, 0.0), axis=1, keepdims=True)
        return jnp.where(cnt >= cap, cand, thr)

    thr = lax.fori_loop(0, 31, body, jnp.zeros((ne, 1), jnp.int32))
    gt = aff >= pltpu.bitcast(thr + 1, F32)
    eq = (aff >= pltpu.bitcast(thr, F32)) & jnp.logical_not(gt)
    need = cap - jnp.sum(jnp.where(gt, 1.0, 0.0), axis=1, keepdims=True)
    eq_f = jnp.where(eq, 1.0, 0.0)
    eq_rank = _lane_cumsum(eq_f, tri) - eq_f
    sel = gt | (eq & (eq_rank < need))
    sel_f = jnp.where(sel, 1.0, 0.0)
    csum = _lane_cumsum(sel_f, tri)
    rank = jnp.where(sel, csum - sel_f, -1.0)
    half_ref[0] = csum[:, seq // 2 - 1:seq // 2].astype(jnp.int32)

    slot = lax.broadcasted_iota(jnp.int32, (cap, ts), 0).astype(F32)
    lane_e = lax.broadcasted_iota(jnp.int32, (cap, ne), 1)
    idx = jnp.zeros((cap, ne), F32)
    for e in range(ne):
        col = jnp.zeros((cap, 1), F32)
        for j in range(seq // ts):
            tok = (lax.broadcasted_iota(jnp.int32, (1, ts), 1) + j * ts).astype(F32)
            hit = rank[e:e + 1, j * ts:(j + 1) * ts] == slot
            col = col + jnp.sum(jnp.where(hit, tok, 0.0), axis=1, keepdims=True)
        idx = jnp.where(lane_e == e, col, idx)
    idx_ref[0] = idx.astype(jnp.int32)


def _select_call(aff_t, tri, *, cap):
    b, ne, seq = aff_t.shape
    return pl.pallas_call(
        functools.partial(_select_kernel, cap=cap, seq=seq, ts=min(1024, seq)),
        grid=(b,),
        in_specs=[pl.BlockSpec((1, ne, seq), lambda i: (i, 0, 0)),
                  pl.BlockSpec((LANES, LANES), lambda i: (0, 0))],
        out_specs=[pl.BlockSpec((1, cap, ne), lambda i: (i, 0, 0)),
                   pl.BlockSpec((1, ne, 1), lambda i: (i, 0, 0))],
        out_shape=[jax.ShapeDtypeStruct((b, cap, ne), jnp.int32),
                   jax.ShapeDtypeStruct((b, ne, 1), jnp.int32)],
        name="expert_select",
        compiler_params=_cparams(("arbitrary",), 32),
    )(aff_t, tri)


GATHER_UNROLL = 8


def _gather_kernel(idx_ref, h_ref, aff_ref, xg_ref, gc_ref, rows_ref, arow_ref, *, cap, ne):
    b = pl.program_id(0)
    e = pl.program_id(1)
    base = (b * ne + e) * cap

    def body(g, carry):
        for k in range(GATHER_UNROLL):
            c = g * GATHER_UNROLL + k
            r = idx_ref[base + c]
            src = pl.ds(pl.multiple_of(r * ROW_TILE, ROW_TILE), ROW_TILE)
            dst = pl.ds(pl.multiple_of(c * ROW_TILE, ROW_TILE), ROW_TILE)
            rows_ref[dst, :] = h_ref[src, :]
            arow_ref[pl.ds(c, 1), :] = aff_ref[pl.ds(r, 1), :]
        return carry

    lax.fori_loop(0, cap // GATHER_UNROLL, body, 0)
    xg_ref[0] = jnp.concatenate([rows_ref[pl.ds(i, cap, stride=ROW_TILE), :] for i in range(ROW_TILE)],
                                axis=1).astype(xg_ref.dtype)
    lane = lax.broadcasted_iota(jnp.int32, arow_ref.shape, 1)
    gc_ref[0] = jnp.sum(jnp.where(lane == e, arow_ref[...], 0.0), axis=1, keepdims=True)


def _gather_call(idx_flat, h2t, aff_tm, *, batch, seq, cap, ne):
    d = ROW_TILE * LANES
    assert h2t.shape == (batch * seq * ROW_TILE, LANES)
    grid_spec = pltpu.PrefetchScalarGridSpec(
        num_scalar_prefetch=1,
        grid=(batch, ne),
        in_specs=[pl.BlockSpec((seq * ROW_TILE, LANES), lambda b, e, idx: (b, 0)),
                  pl.BlockSpec((seq, LANES), lambda b, e, idx: (b, 0))],
        out_specs=[pl.BlockSpec((1, cap, d), lambda b, e, idx: (e, b, 0)),
                   pl.BlockSpec((1, cap, 1), lambda b, e, idx: (e, b, 0))],
        scratch_shapes=[pltpu.VMEM((cap * ROW_TILE, LANES), F32), pltpu.VMEM((cap, LANES), F32)],
    )
    return pl.pallas_call(
        functools.partial(_gather_kernel, cap=cap, ne=ne),
        grid_spec=grid_spec,
        out_shape=[jax.ShapeDtypeStruct((ne, batch * cap, d), BF16),
                   jax.ShapeDtypeStruct((ne, batch * cap, 1), F32)],
        name="expert_gather",
        compiler_params=_cparams(("arbitrary", "arbitrary"), 48),
    )(idx_flat, h2t, aff_tm)


def _ffn_kernel(x_ref, gc_ref, wg_ref, wu_ref, wd_ref, ye_ref, *, rows):
    f = pl.program_id(1)

    @pl.when(f == 0)
    def _init():
        ye_ref[...] = jnp.zeros_like(ye_ref)

    wg = wg_ref[0, 0].astype(BF16)
    wu = wu_ref[0, 0].astype(BF16)
    wd = wd_ref[0, 0].astype(BF16)
    nblk = x_ref.shape[1] // rows

    def up(r):
        x = x_ref[0, r * rows:(r + 1) * rows, :]
        return (jnp.dot(x, wg, preferred_element_type=F32), jnp.dot(x, wu, preferred_element_type=F32))

    nxt = up(0)
    for r in range(nblk):
        hg, hu = nxt
        if r + 1 < nblk:
            nxt = up(r + 1)
        hid = (hg * jax.nn.sigmoid(hg) * hu).astype(BF16)
        ye_ref[0, r * rows:(r + 1) * rows, :] += jnp.dot(hid, wd, preferred_element_type=F32)

    @pl.when(f == pl.num_programs(1) - 1)
    def _gate():
        ye_ref[0] = ye_ref[0] * gc_ref[0]


def _ffn_call(xg, gc, wg, wu, wd, *, layer, tf, rows):
    ne, n, d = xg.shape
    dff = wg.shape[3]
    return pl.pallas_call(
        functools.partial(_ffn_kernel, rows=rows),
        grid=(ne, dff // tf),
        in_specs=[pl.BlockSpec((1, n, d), lambda e, f: (e, 0, 0)),
                  pl.BlockSpec((1, n, 1), lambda e, f: (e, 0, 0)),
                  pl.BlockSpec((1, 1, d, tf), lambda e, f: (layer, e, 0, f)),
                  pl.BlockSpec((1, 1, d, tf), lambda e, f: (layer, e, 0, f)),
                  pl.BlockSpec((1, 1, tf, d), lambda e, f: (layer, e, f, 0))],
        out_specs=pl.BlockSpec((1, n, d), lambda e, f: (e, 0, 0)),
        out_shape=jax.ShapeDtypeStruct((ne, n, d), F32),
        name="expert_ffn",
        compiler_params=_cparams(("arbitrary", "arbitrary"), 56),
    )(xg, gc, wg, wu, wd)


SCATTER_UNROLL = 8
SEQ_HALVES = 2


def _combine_kernel(idx_ref, half_ref, ye_ref, x1_ref, mod_ref, o_ref, acc_ref, *, cap, ne, rows,
                    d_model):
    i = pl.program_id(0)
    e = pl.program_id(1)
    b = i // SEQ_HALVES
    hh = i % SEQ_HALVES

    @pl.when(e == 0)
    def _init():
        o_ref[...] = jnp.zeros_like(o_ref)
        acc_ref[...] = jnp.zeros_like(acc_ref)

    mid = half_ref[b * ne + e]
    lo = jnp.where(hh == 0, 0, mid)
    hi = jnp.where(hh == 0, mid, cap)
    base = (b * ne + e) * cap
    off = hh * rows

    def add_row(c, dst_ref):
        r = idx_ref[base + c] - off
        dst_ref[pl.ds(r, 1), :] += ye_ref[0, pl.ds(c, 1), :]

    def group(g, carry):
        for k in range(SCATTER_UNROLL):
            add_row(lo + g * SCATTER_UNROLL + k, o_ref if k % 2 == 0 else acc_ref)
        return carry

    ngroup = (hi - lo) // SCATTER_UNROLL
    lax.fori_loop(0, ngroup, group, 0)

    def tail(c, carry):
        add_row(c, o_ref)
        return carry

    lax.fori_loop(lo + ngroup * SCATTER_UNROLL, hi, tail, 0)

    @pl.when(e == pl.num_programs(1) - 1)
    def _fin():
        g2 = mod_ref[0][:, 5 * d_model:6 * d_model]
        o_ref[...] = x1_ref[...] + g2 * (o_ref[...] + acc_ref[...])


def _combine_call(idx_flat, half_flat, ye, x1, mod3, *, batch, seq, cap):
    t, d = x1.shape
    ne = ye.shape[0]
    rows = seq // SEQ_HALVES
    grid_spec = pltpu.PrefetchScalarGridSpec(
        num_scalar_prefetch=2,
        grid=(batch * SEQ_HALVES, ne),
        in_specs=[pl.BlockSpec((1, cap, d), lambda i, e, idx, half: (e, i // SEQ_HALVES, 0)),
                  pl.BlockSpec((rows, d), lambda i, e, idx, half: (i, 0)),
                  pl.BlockSpec((1, 1, mod3.shape[2]), lambda i, e, idx, half: (i // SEQ_HALVES, 0, 0))],
        out_specs=pl.BlockSpec((rows, d), lambda i, e, idx, half: (i, 0)),
        scratch_shapes=[pltpu.VMEM((rows, d), F32)],
    )
    return pl.pallas_call(
        functools.partial(_combine_kernel, cap=cap, ne=ne, rows=rows, d_model=d),
        grid_spec=grid_spec,
        out_shape=jax.ShapeDtypeStruct((t, d), F32),
        name="moe_combine",
        compiler_params=_cparams(("arbitrary", "arbitrary"), 52),
    )(idx_flat, half_flat, ye, x1, mod3)


def kernel(x, c, w_mod, b_mod, ln1_w, ln2_w, w_in, ssm_lam_re, ssm_lam_im, ssm_log_dt, ssm_b_re,
           ssm_b_im, ssm_c_re, ssm_c_im, ssm_d, w_glu, b_glu, q_norm_w, k_norm_w, w_ssm_up,
           w_attn_up, w_out, w_router, w_exp_gate, w_exp_up, w_exp_down):
    batch, seq, d = x.shape
    depth = w_mod.shape[0]
    t = batch * seq
    d_attn = N_Q_HEADS * HEAD_DIM
    d_kv = N_KV_HEADS * HEAD_DIM
    cap = CAPACITY_FACTOR * seq // N_EXPERTS
    tm = min(512, seq)
    tq = min(256, seq)

    tabs = _rope_tables(seq)
    head_id = jnp.arange(d_attn) // HEAD_DIM
    hm = (head_id[:, None] == head_id[None, :]).astype(BF16)
    tri = (jnp.arange(LANES)[:, None] <= jnp.arange(LANES)[None, :]).astype(BF16)
    c_pad = jnp.zeros((8, d), F32).at[:batch].set(c)
    mod_all = _mod_call(c_pad, w_mod, b_mod)[:, :batch]
    wa_all, wb_all, a_re_all, a_im_all = _s5_weights(ssm_lam_re, ssm_lam_im, ssm_log_dt, ssm_b_re,
                                                     ssm_b_im, ssm_c_re, ssm_c_im, ssm_d)

    x2 = x.reshape(t, d)
    for l in range(depth):
        mod3 = mod_all[l].reshape(batch, 1, 6 * d)
        qw = jnp.tile(q_norm_w[l], N_Q_HEADS).reshape(1, d_attn)
        kw = jnp.tile(k_norm_w[l], N_KV_HEADS).reshape(1, d_kv)
        u, q, k, v, gs, ga = _in_call(x2, mod3, ln1_w[l].reshape(1, d), w_in[l].astype(BF16), tabs,
                                      qw, kw, hm, seq=seq, tm=tm)
        y = _s5_mixer(u, wa_all[l], wb_all[l], a_re_all[l], a_im_all[l], batch=batch, seq=seq)
        attn = _attn_call(q, k, v, seq=seq, tq=tq)
        x1, h2, aff_t, aff_tm = _post_call(
            y, attn, gs, ga, x2, mod3, ln2_w[l].reshape(1, d), w_glu[l].astype(BF16),
            b_glu[l].reshape(1, -1), w_ssm_up[l].astype(BF16), w_attn_up[l].astype(BF16),
            w_out[l].astype(BF16), w_router[l].T, seq=seq, tm=tm)
        idx_t, half = _select_call(aff_t, tri, cap=cap)
        idx_flat = jnp.swapaxes(idx_t, 1, 2).reshape(-1)
        half_flat = half.reshape(-1)
        xg, gc = _gather_call(idx_flat, h2, aff_tm, batch=batch, seq=seq, cap=cap, ne=N_EXPERTS)
        ye = _ffn_call(xg, gc, w_exp_gate, w_exp_up, w_exp_down, layer=l, tf=512,
                       rows=min(1024, batch * cap))
        x2 = _combine_call(idx_flat, half_flat, ye, x1, mod3, batch=batch, seq=seq, cap=cap)
    return x2.reshape(batch, seq, d)
```

```python
import functools
import math

import jax
import jax.numpy as jnp
from jax import lax
from jax.experimental import pallas as pl
from jax.experimental.pallas import tpu as pltpu

F32 = jnp.float32
BF16 = jnp.bfloat16

HEAD_DIM = 64
N_Q_HEADS = 8
N_KV_HEADS = 2
Q_PER_KV = N_Q_HEADS // N_KV_HEADS
ROT_HALF = HEAD_DIM // 2
ROPE_THETA = 10000.0
GRID_W = 64
SSM_GROUP = 16
SSM_STATE = 64
N_EXPERTS = 16
CAPACITY_FACTOR = 2
EPS = 1e-6

S5_CHUNK = 16
S5_PAIR = 2
LANES = 128
V7X_VMEM_BYTES = 64 * 1024 * 1024
LOG2E = 1.4426950408889634


PIPELINE_BUFFERS = 2
SUBLANES_32BIT = 8


def _vmem_bytes(shape, dtype):
    itemsize = jnp.dtype(dtype).itemsize
    sub = SUBLANES_32BIT * 4 // itemsize
    lead = math.prod(shape[:-2]) if len(shape) > 2 else 1
    rows = shape[-2] if len(shape) > 1 else 1
    return lead * (-(-rows // sub) * sub) * (-(-shape[-1] // LANES) * LANES) * itemsize


def _cparams(semantics, blocks, scratch=(), temps=()):
    need = (PIPELINE_BUFFERS * sum(_vmem_bytes(s, d) for s, d in blocks)
            + sum(_vmem_bytes(s, d) for s, d in scratch) + sum(_vmem_bytes(s, d) for s, d in temps))
    return pltpu.CompilerParams(dimension_semantics=semantics,
                                vmem_limit_bytes=min(need, V7X_VMEM_BYTES))


def _mod_kernel(c_ref, w_ref, b_ref, o_ref):
    c = c_ref[...]
    ca = c * jax.nn.sigmoid(c)
    o_ref[0] = jnp.dot(ca, w_ref[0], preferred_element_type=F32,
                       precision=lax.Precision.HIGHEST) + b_ref[0]


def _mod_call(c_pad, w_mod, b_mod):
    depth, d, n = w_mod.shape
    rows = c_pad.shape[0]
    tn = 1536
    return pl.pallas_call(
        _mod_kernel,
        grid=(depth, n // tn),
        in_specs=[pl.BlockSpec((rows, d), lambda l, j: (0, 0)),
                  pl.BlockSpec((1, d, tn), lambda l, j: (l, 0, j)),
                  pl.BlockSpec((1, 1, tn), lambda l, j: (l, 0, j))],
        out_specs=pl.BlockSpec((1, rows, tn), lambda l, j: (l, 0, j)),
        out_shape=jax.ShapeDtypeStruct((depth, rows, n), F32),
        name="adaln_mod",
        compiler_params=_cparams(
            ("arbitrary", "arbitrary"),
            blocks=[((rows, d), F32), ((1, d, tn), F32), ((1, 1, tn), F32), ((1, rows, tn), F32)],
            temps=[((d, tn), F32)] * 2),
    )(c_pad, w_mod, b_mod.reshape(depth, 1, n))


def _head_mean_square(xsq, hm):
    hi = xsq.astype(BF16)
    lo = (xsq - hi.astype(F32)).astype(BF16)
    s = jnp.dot(hi, hm, preferred_element_type=F32) + jnp.dot(lo, hm, preferred_element_type=F32)
    return s * (1.0 / HEAD_DIM)


def _rope_block(blk, cos, sa, sb):
    return (blk * cos + pltpu.roll(blk, LANES - ROT_HALF // 2, 1) * sa
            + pltpu.roll(blk, ROT_HALF // 2, 1) * sb)


def _in_kernel(x_ref, mod_ref, ln_ref, w_ref, cos_ref, sa_ref, sb_ref, qw_ref, kw_ref, hm_ref,
               u_ref, q_ref, k_ref, v_ref, gs_ref, ga_ref, *, d_model, d_ssm, d_attn, d_kv):
    x = x_ref[...]
    mod = mod_ref[0]
    sh = mod[:, 0:d_model]
    sc = mod[:, d_model:2 * d_model]
    ms = jnp.mean(x * x, axis=-1, keepdims=True)
    h = (x * lax.rsqrt(ms + EPS) * ln_ref[...]) * (1.0 + sc) + sh
    hb = h.astype(BF16)
    hm = hm_ref[...]
    cos = cos_ref[...]
    sa = sa_ref[...]
    sb = sb_ref[...]
    qscale = (HEAD_DIM ** -0.5) * LOG2E

    def finish_u(p):
        u_ref[...] = p.astype(u_ref.dtype)

    def finish_q(q):
        qn = q * lax.rsqrt(_head_mean_square(q * q, hm) + EPS) * qw_ref[...]
        qr = [_rope_block(qn[:, j * LANES:(j + 1) * LANES], cos, sa, sb) for j in range(d_attn // LANES)]
        q_ref[...] = (jnp.concatenate(qr, axis=-1) * qscale).astype(q_ref.dtype)

    def finish_kv(p):
        k = p[:, :d_kv]
        v_ref[0] = p[:, d_kv:].T.astype(v_ref.dtype)
        kn = k * lax.rsqrt(_head_mean_square(k * k, hm[:d_kv, :d_kv]) + EPS) * kw_ref[...]
        kr = [_rope_block(kn[:, j * LANES:(j + 1) * LANES], cos, sa, sb) for j in range(d_kv // LANES)]
        k_ref[...] = jnp.concatenate(kr, axis=-1).astype(k_ref.dtype)

    def finish_gs(p):
        gs_ref[...] = jax.nn.sigmoid(p).astype(gs_ref.dtype)

    def finish_ga(p):
        ga_ref[...] = jax.nn.sigmoid(p).astype(ga_ref.dtype)

    groups = ((d_ssm, finish_u), (d_attn, finish_q), (2 * d_kv, finish_kv),
              (d_model, finish_gs), (d_model, finish_ga))
    starts = [sum(g[0] for g in groups[:i]) for i in range(len(groups))]

    def project(i):
        return jnp.dot(hb, w_ref[:, starts[i]:starts[i] + groups[i][0]], preferred_element_type=F32)

    nxt = project(0)
    for i, (_, finish) in enumerate(groups):
        cur = nxt
        if i + 1 < len(groups):
            nxt = project(i + 1)
        finish(cur)


def _in_call(x2, mod3, ln_w, w_in, tabs, qw, kw, hm, *, seq, tm):
    t, d = x2.shape
    n_in = w_in.shape[1]
    d_ssm = d // 2
    d_attn = N_Q_HEADS * HEAD_DIM
    d_kv = N_KV_HEADS * HEAD_DIM
    tiles_per_seq = seq // tm
    cos, sa, sb = tabs
    kern = functools.partial(_in_kernel, d_model=d, d_ssm=d_ssm, d_attn=d_attn, d_kv=d_kv)
    row = lambda i: (i, 0)
    const = lambda i: (0, 0)
    tab = lambda i: (i % tiles_per_seq, 0)
    return pl.pallas_call(
        kern,
        grid=(t // tm,),
        in_specs=[pl.BlockSpec((tm, d), row),
                  pl.BlockSpec((1, 1, mod3.shape[2]), lambda i: (i // tiles_per_seq, 0, 0)),
                  pl.BlockSpec((1, d), const),
                  pl.BlockSpec((d, n_in), const),
                  pl.BlockSpec((tm, LANES), tab),
                  pl.BlockSpec((tm, LANES), tab),
                  pl.BlockSpec((tm, LANES), tab),
                  pl.BlockSpec((1, d_attn), const),
                  pl.BlockSpec((1, d_kv), const),
                  pl.BlockSpec((d_attn, d_attn), const)],
        out_specs=[pl.BlockSpec((tm, d_ssm), row),
                   pl.BlockSpec((tm, d_attn), row),
                   pl.BlockSpec((tm, d_kv), row),
                   pl.BlockSpec((1, d_kv, tm), lambda i: (i // tiles_per_seq, 0, i % tiles_per_seq)),
                   pl.BlockSpec((tm, d), row),
                   pl.BlockSpec((tm, d), row)],
        out_shape=[jax.ShapeDtypeStruct((t, d_ssm), F32),
                   jax.ShapeDtypeStruct((t, d_attn), BF16),
                   jax.ShapeDtypeStruct((t, d_kv), BF16),
                   jax.ShapeDtypeStruct((t // seq, d_kv, seq), BF16),
                   jax.ShapeDtypeStruct((t, d), BF16),
                   jax.ShapeDtypeStruct((t, d), BF16)],
        name="in_proj",
        compiler_params=_cparams(
            ("arbitrary",),
            blocks=[((tm, d), F32), ((1, 1, mod3.shape[2]), F32), ((1, d), F32), ((d, n_in), BF16),
                    ((tm, LANES), F32), ((tm, LANES), F32), ((tm, LANES), F32),
                    ((1, d_attn), F32), ((1, d_kv), F32), ((d_attn, d_attn), BF16),
                    ((tm, d_ssm), F32), ((tm, d_attn), BF16), ((tm, d_kv), BF16), ((1, d_kv, tm), BF16),
                    ((tm, d), BF16), ((tm, d), BF16)],
            temps=[((tm, d), F32)] * 6),
    )(x2, mod3, ln_w, w_in, cos, sa, sb, qw, kw, hm)


def _rope_tables(seq):
    pos = jnp.arange(seq)
    row = (pos // GRID_W).astype(F32)
    col = (pos % GRID_W).astype(F32)
    inv = 1.0 / (ROPE_THETA ** (jnp.arange(0, ROT_HALF, 2, dtype=F32) / ROT_HALF))
    ang_r = row[:, None] * inv[None, :]
    ang_c = col[:, None] * inv[None, :]
    zeros = jnp.zeros_like(ang_r)
    cos_h = jnp.concatenate([jnp.cos(ang_r), jnp.cos(ang_r), jnp.cos(ang_c), jnp.cos(ang_c)], axis=-1)
    sa_h = jnp.concatenate([-jnp.sin(ang_r), zeros, -jnp.sin(ang_c), zeros], axis=-1)
    sb_h = jnp.concatenate([zeros, jnp.sin(ang_r), zeros, jnp.sin(ang_c)], axis=-1)
    rep = LANES // HEAD_DIM
    return tuple(jnp.tile(a, (1, rep)) for a in (cos_h, sa_h, sb_h))


def _attn_heads_out(o_t, tq):
    return [o_t[:, g * tq:(g + 1) * tq].T for g in range(Q_PER_KV)]


def _attn_kernel(q_ref, k_ref, vt_ref, o_ref, *, kb):
    tq = q_ref.shape[0]
    seq = k_ref.shape[0]
    nblk = seq // kb
    width = Q_PER_KV * HEAD_DIM
    q_ts = []
    for kvh in range(N_KV_HEADS):
        qt = q_ref[:, kvh * width:(kvh + 1) * width].astype(F32).T.astype(BF16)
        q_ts.append(jnp.concatenate([qt[g * HEAD_DIM:(g + 1) * HEAD_DIM] for g in range(Q_PER_KV)],
                                    axis=1))

    def scores(kvh, j):
        lo = kvh * HEAD_DIM
        return jnp.dot(k_ref[j * kb:(j + 1) * kb, lo:lo + HEAD_DIM], q_ts[kvh],
                       preferred_element_type=F32)

    def weighted_v(kvh, j, p):
        lo = kvh * HEAD_DIM
        return jnp.dot(vt_ref[0, lo:lo + HEAD_DIM, j * kb:(j + 1) * kb], p.astype(BF16),
                       preferred_element_type=F32)

    outs = []
    worst = jnp.zeros((1, 1), F32)
    for kvh in range(N_KV_HEADS):
        s0 = scores(kvh, 0)
        m0 = jnp.max(s0, axis=0, keepdims=True)
        l = jnp.zeros_like(m0)
        acc = jnp.zeros((HEAD_DIM, m0.shape[1]), F32)
        s_cur = s0
        for j in range(nblk):
            s_next = scores(kvh, j + 1) if j + 1 < nblk else None
            p = jnp.exp2(s_cur - m0)
            l = l + jnp.sum(p, axis=0, keepdims=True)
            acc = acc + weighted_v(kvh, j, p)
            s_cur = s_next
        worst = jnp.maximum(worst, jnp.max(l, axis=1, keepdims=True))
        outs.extend(_attn_heads_out(acc / l, tq))
    o_ref[...] = jnp.concatenate(outs, axis=-1).astype(o_ref.dtype)

    @pl.when(worst[0, 0] > jnp.finfo(F32).max)
    def _exact():
        outs = []
        for kvh in range(N_KV_HEADS):
            m = jnp.full((1, Q_PER_KV * tq), -jnp.inf, F32)
            l = jnp.zeros_like(m)
            acc = jnp.zeros((HEAD_DIM, Q_PER_KV * tq), F32)
            for j in range(nblk):
                s = scores(kvh, j)
                m_new = jnp.maximum(m, jnp.max(s, axis=0, keepdims=True))
                alpha = jnp.exp2(m - m_new)
                p = jnp.exp2(s - m_new)
                l = l * alpha + jnp.sum(p, axis=0, keepdims=True)
                acc = acc * alpha + weighted_v(kvh, j, p)
                m = m_new
            outs.extend(_attn_heads_out(acc / l, tq))
        o_ref[...] = jnp.concatenate(outs, axis=-1).astype(o_ref.dtype)


def _attn_call(q, k, vt, *, seq, tq):
    t, d_attn = q.shape
    d_kv = k.shape[1]
    nq = seq // tq
    kb = min(256, seq)
    return pl.pallas_call(
        functools.partial(_attn_kernel, kb=kb),
        grid=(t // seq, nq),
        in_specs=[pl.BlockSpec((tq, d_attn), lambda b, i: (b * nq + i, 0)),
                  pl.BlockSpec((seq, d_kv), lambda b, i: (b, 0)),
                  pl.BlockSpec((1, d_kv, seq), lambda b, i: (b, 0, 0))],
        out_specs=pl.BlockSpec((tq, d_attn), lambda b, i: (b * nq + i, 0)),
        out_shape=jax.ShapeDtypeStruct((t, d_attn), BF16),
        name="attention",
        compiler_params=_cparams(
            ("arbitrary", "arbitrary"),
            blocks=[((tq, d_attn), BF16), ((seq, d_kv), BF16), ((1, d_kv, seq), BF16), ((tq, d_attn), BF16)],
            temps=[((kb, Q_PER_KV * tq), F32)] * 16),
    )(q, k, vt)


def _s5_weights(lam_re, lam_im, log_dt, b_re, b_im, c_re, c_im, d_skip):
    hp = lax.Precision.HIGHEST
    nl, _, ng, p = lam_re.shape
    cg = SSM_GROUP
    L = S5_CHUNK
    kd = L * cg
    dt = jnp.exp(log_dt)[..., None]
    zr = lam_re * dt
    zi = lam_im * dt

    def cpow(e):
        mag = jnp.exp(zr[..., None] * e)
        return mag * jnp.cos(zi[..., None] * e), mag * jnp.sin(zi[..., None] * e)

    lb_re, lb_im = jnp.exp(zr) * jnp.cos(zi), jnp.exp(zr) * jnp.sin(zi)
    nr, ni = lb_re - 1.0, lb_im
    den = lam_re * lam_re + lam_im * lam_im
    f_re = ((nr * lam_re + ni * lam_im) / den)[..., None]
    f_im = ((ni * lam_re - nr * lam_im) / den)[..., None]
    bb_re = f_re * b_re - f_im * b_im
    bb_im = f_re * b_im + f_im * b_re

    m_idx = jnp.arange(L + 1, dtype=F32)
    expo = jnp.stack([m_idx, L - m_idx])[None, :, None, None, :]
    pw_re, pw_im = cpow(expo)
    rep = jnp.repeat(jnp.eye(L + 1, dtype=F32), cg, axis=1)
    til = jnp.tile(jnp.eye(cg, dtype=F32), (1, L + 1))
    pwx_re = jnp.einsum('dxgpm,mn->dxgpn', pw_re, rep, precision=hp)
    pwx_im = jnp.einsum('dxgpm,mn->dxgpn', pw_im, rep, precision=hp)
    ct_re = jnp.einsum('dxgcp,cn->dxgpn', c_re, til, precision=hp)
    ct_im = jnp.einsum('dxgcp,cn->dxgpn', c_im, til, precision=hp)
    ca_re = ct_re * pwx_re - ct_im * pwx_im
    ca_im = ct_re * pwx_im + ct_im * pwx_re
    ef_re, ef_im = ca_re[:, 0, ..., :kd], ca_im[:, 0, ..., :kd]
    w2f_re, w2f_im = ca_re[:, 0, ..., cg:], ca_im[:, 0, ..., cg:]
    w2b_re, w2b_im = ca_re[:, 1, ..., :kd], ca_im[:, 1, ..., :kd]
    eb_re, eb_im = ca_re[:, 1, ..., cg:], ca_im[:, 1, ..., cg:]

    def lag_rows(x, e_re, e_im):
        return (jnp.einsum('dgpe,dgpn->dgen', bb_re[:, x], e_re, precision=hp)
                - jnp.einsum('dgpe,dgpn->dgen', bb_im[:, x], e_im, precision=hp))

    kf = lag_rows(0, ef_re, ef_im)
    kb = lag_rows(1, eb_re, eb_im)
    zeros = jnp.zeros_like(kf)
    pf = jnp.concatenate([zeros, kf], axis=-1)
    pb = jnp.concatenate([kb, zeros], axis=-1)
    tf = jnp.stack([pf[..., kd - cg * s:2 * kd - cg * s] for s in range(L)], axis=2)
    tb = jnp.stack([pb[..., cg * (L - 1 - s):cg * (L - 1 - s) + kd] for s in range(L)], axis=2)
    skip = jnp.eye(kd, dtype=F32) * jnp.tile(d_skip.reshape(nl, ng, 1, cg), (1, 1, 1, L))
    tmat = (tf + tb).reshape(nl, ng, kd, kd) + skip

    s_idx = jnp.arange(L, dtype=F32)
    sexp = jnp.stack([L - 1 - s_idx, s_idx])[None, :, None, None, :]
    ps_re, ps_im = cpow(sexp)
    w1_re = (jnp.einsum('dxgps,dxgpe->dxgsep', ps_re, bb_re)
             - jnp.einsum('dxgps,dxgpe->dxgsep', ps_im, bb_im)).reshape(nl, 2, ng, kd, p)
    w1_im = (jnp.einsum('dxgps,dxgpe->dxgsep', ps_re, bb_im)
             + jnp.einsum('dxgps,dxgpe->dxgsep', ps_im, bb_re)).reshape(nl, 2, ng, kd, p)

    wa = jnp.concatenate([tmat, w1_re[:, 0], w1_im[:, 0], w1_re[:, 1], w1_im[:, 1]], axis=-1)
    wb = jnp.concatenate([w2f_re, -w2f_im, w2b_re, -w2b_im], axis=2)
    al_re, al_im = cpow(jnp.float32(L))
    a_re = al_re.reshape(nl, 2, ng * p)
    a_im = al_im.reshape(nl, 2, ng * p)
    return wa.astype(BF16), wb.astype(BF16), a_re, a_im


GROUPS_PER_VREG = LANES // SSM_GROUP
PAIRS_PER_VREG = GROUPS_PER_VREG // S5_PAIR
TOKENS_PER_HALF = LANES // SSM_GROUP


def _granule_transpose(arrs):
    gran = lax.broadcasted_iota(jnp.int32, arrs[0].shape, 1) // SSM_GROUP
    cur = list(arrs)
    for dist in (4, 2, 1):
        keep = (gran & dist) == 0
        nxt = list(cur)
        for i in range(len(cur)):
            if i & dist:
                continue
            a, b = cur[i], cur[i + dist]
            nxt[i] = jnp.where(keep, a, pltpu.roll(b, dist * SSM_GROUP, 1))
            nxt[i + dist] = jnp.where(keep, pltpu.roll(a, LANES - dist * SSM_GROUP, 1), b)
        cur = nxt
    return cur


def _pair_lanes(a, b):
    low = lax.broadcasted_iota(jnp.int32, a.shape, 1) < LANES // 2
    return (jnp.where(low, a, pltpu.roll(b, LANES // 2, 1)),
            jnp.where(low, pltpu.roll(a, LANES // 2, 1), b))


def _s5_chunk_kernel(u_ref, wa_ref, yi_ref, sfr_ref, sfi_ref, sbr_ref, sbi_ref, *, nchunk):
    halves = []
    for jh in range(S5_CHUNK // TOKENS_PER_HALF):
        toks = [u_ref[pl.ds(jh * TOKENS_PER_HALF + jj, nchunk, stride=S5_CHUNK), :]
                for jj in range(TOKENS_PER_HALF)]
        halves.append([o.astype(BF16) for o in _granule_transpose(toks)])
    kd = S5_CHUNK * SSM_GROUP
    for qq in range(PAIRS_PER_VREG):
        res = []
        for gl in range(S5_PAIR):
            g = S5_PAIR * qq + gl
            z = jnp.concatenate([halves[jh][g] for jh in range(len(halves))], axis=-1)
            r = jnp.dot(z, wa_ref[g], preferred_element_type=F32)
            yi_ref[0, g] = r[:, :kd]
            res.append(r)
        lo = qq * LANES
        sfr_ref[:, lo:lo + LANES], sfi_ref[:, lo:lo + LANES] = _pair_lanes(
            res[0][:, kd:kd + LANES], res[1][:, kd:kd + LANES])
        sbr_ref[:, lo:lo + LANES], sbi_ref[:, lo:lo + LANES] = _pair_lanes(
            res[0][:, kd + LANES:kd + 2 * LANES], res[1][:, kd + LANES:kd + 2 * LANES])


def _s5_chunk_call(u, wa, *, batch, seq):
    t, d_ssm = u.shape
    nchunk = seq // S5_CHUNK
    nv = d_ssm // LANES
    ng, kd, ncol = wa.shape
    sw = PAIRS_PER_VREG * LANES
    st = jax.ShapeDtypeStruct((nchunk, batch * nv * sw), F32)
    sspec = pl.BlockSpec((nchunk, sw), lambda b, v: (0, b * nv + v))
    return pl.pallas_call(
        functools.partial(_s5_chunk_kernel, nchunk=nchunk),
        grid=(batch, nv),
        in_specs=[pl.BlockSpec((seq, LANES), lambda b, v: (b, v)),
                  pl.BlockSpec((GROUPS_PER_VREG, kd, ncol), lambda b, v: (v, 0, 0))],
        out_specs=[pl.BlockSpec((1, GROUPS_PER_VREG, nchunk, kd), lambda b, v: (b, v, 0, 0)),
                   sspec, sspec, sspec, sspec],
        out_shape=[jax.ShapeDtypeStruct((batch, ng, nchunk, kd), F32), st, st, st, st],
        name="s5_chunk",
        compiler_params=_cparams(
            ("arbitrary", "arbitrary"),
            blocks=[((seq, LANES), F32), ((GROUPS_PER_VREG, kd, ncol), BF16),
                    ((1, GROUPS_PER_VREG, nchunk, kd), F32)] + [((nchunk, sw), F32)] * 4,
            temps=[((seq, LANES), F32)] * 3 + [((nchunk, ncol), F32)] * 4),
    )(u, wa)


def _s5scan_kernel(sfr_ref, sfi_ref, sbr_ref, sbi_ref, afr_ref, afi_ref, abr_ref, abi_ref,
                   xfr_ref, xfi_ref, xbr_ref, xbi_ref, *, nchunk):
    afr, afi = afr_ref[...], afi_ref[...]
    abr, abi = abr_ref[...], abi_ref[...]
    zero = jnp.zeros_like(afr)

    def body(k, carry):
        fr, fi, br, bi = carry
        kb = nchunk - 1 - k
        rowf = pl.ds(k, 1)
        rowb = pl.ds(kb, 1)
        xfr_ref[rowf, :] = fr
        xfi_ref[rowf, :] = fi
        xbr_ref[rowb, :] = br
        xbi_ref[rowb, :] = bi
        nfr = fr * afr - fi * afi + sfr_ref[rowf, :]
        nfi = fr * afi + fi * afr + sfi_ref[rowf, :]
        nbr = br * abr - bi * abi + sbr_ref[rowb, :]
        nbi = br * abi + bi * abr + sbi_ref[rowb, :]
        return nfr, nfi, nbr, nbi

    lax.fori_loop(0, nchunk, body, (zero, zero, zero, zero))


def _s5scan_call(s4, a4, *, nchunk, tl):
    w = s4[0].shape[1]
    sspec = pl.BlockSpec((nchunk, tl), lambda j: (0, j))
    aspec = pl.BlockSpec((1, tl), lambda j: (0, j))
    st = jax.ShapeDtypeStruct((nchunk, w), F32)
    return pl.pallas_call(
        functools.partial(_s5scan_kernel, nchunk=nchunk),
        grid=(w // tl,),
        in_specs=[sspec] * 4 + [aspec] * 4,
        out_specs=[sspec] * 4,
        out_shape=[st] * 4,
        name="s5_scan",
        compiler_params=_cparams(("arbitrary",),
                                 blocks=[((nchunk, tl), F32)] * 8 + [((1, tl), F32)] * 4,
                                 temps=[((SUBLANES_32BIT, tl), F32)] * 16),
    )(*s4, *a4)


def _s5_readout_kernel(yi_ref, xfr_ref, xfi_ref, xbr_ref, xbi_ref, wb_ref, y_ref, *, nchunk):
    nhalf = S5_CHUNK // TOKENS_PER_HALF
    pieces = [[None] * GROUPS_PER_VREG for _ in range(nhalf)]
    for qq in range(PAIRS_PER_VREG):
        lo = qq * LANES
        xf = _pair_lanes(xfr_ref[:, lo:lo + LANES], xfi_ref[:, lo:lo + LANES])
        xb = _pair_lanes(xbr_ref[:, lo:lo + LANES], xbi_ref[:, lo:lo + LANES])
        for gl in range(S5_PAIR):
            g = S5_PAIR * qq + gl
            xs = jnp.concatenate([xf[gl], xb[gl]], axis=-1)
            y = yi_ref[0, g] + jnp.dot(xs.astype(BF16), wb_ref[g], preferred_element_type=F32)
            for jh in range(nhalf):
                pieces[jh][g] = y[:, jh * LANES:(jh + 1) * LANES]
    for jh in range(nhalf):
        toks = _granule_transpose(pieces[jh])
        for jj in range(TOKENS_PER_HALF):
            y_ref[pl.ds(jh * TOKENS_PER_HALF + jj, nchunk, stride=S5_CHUNK), :] = toks[jj]


def _s5_readout_call(yi, x4, wb, *, batch, seq):
    _, ng, nchunk, kd = yi.shape
    nv = ng // GROUPS_PER_VREG
    sw = PAIRS_PER_VREG * LANES
    xspec = pl.BlockSpec((nchunk, sw), lambda b, v: (0, b * nv + v))
    return pl.pallas_call(
        functools.partial(_s5_readout_kernel, nchunk=nchunk),
        grid=(batch, nv),
        in_specs=[pl.BlockSpec((1, GROUPS_PER_VREG, nchunk, kd), lambda b, v: (b, v, 0, 0)),
                  xspec, xspec, xspec, xspec,
                  pl.BlockSpec((GROUPS_PER_VREG, wb.shape[1], kd), lambda b, v: (v, 0, 0))],
        out_specs=pl.BlockSpec((seq, LANES), lambda b, v: (b, v)),
        out_shape=jax.ShapeDtypeStruct((batch * seq, nv * LANES), F32),
        name="s5_readout",
        compiler_params=_cparams(
            ("arbitrary", "arbitrary"),
            blocks=[((1, GROUPS_PER_VREG, nchunk, kd), F32), ((GROUPS_PER_VREG, wb.shape[1], kd), BF16),
                    ((seq, LANES), F32)] + [((nchunk, sw), F32)] * 4,
            temps=[((seq, LANES), F32)] * 4),
    )(yi, *x4, wb)


def _s5_mixer(u, wa, wb, a_re, a_im, *, batch, seq):
    nchunk = seq // S5_CHUNK
    yi, sfr, sfi, sbr, sbi = _s5_chunk_call(u, wa, batch=batch, seq=seq)
    w = a_re.shape[1]
    a4 = [jnp.tile(a.reshape(1, w), (1, batch)) for a in (a_re[0], a_im[0], a_re[1], a_im[1])]
    tl = min(1024, batch * w)
    x4 = _s5scan_call([sfr, sfi, sbr, sbi], a4, nchunk=nchunk, tl=tl)
    return _s5_readout_call(yi, x4, wb, batch=batch, seq=seq)


ROW_TILE = 8


def _post_kernel(y_ref, at_ref, gs_ref, ga_ref, x_ref, mod_ref, ln_ref,
                 wg_ref, bg_ref, wsu_ref, wau_ref, wo_ref, wr_ref,
                 x1_ref, h2_ref, aff_ref, afft_ref, *, d_model):
    mod = mod_ref[0]
    g1 = mod[:, 2 * d_model:3 * d_model]
    sh2 = mod[:, 3 * d_model:4 * d_model]
    sc2 = mod[:, 4 * d_model:5 * d_model]
    tm = y_ref.shape[0]

    au = jnp.dot(at_ref[...], wau_ref[...], preferred_element_type=F32)
    z = jax.nn.gelu(y_ref[...], approximate=True)
    glu = jnp.dot(z.astype(BF16), wg_ref[...], preferred_element_type=F32) + bg_ref[...]
    ssm = z * jax.nn.sigmoid(glu)
    su = jnp.dot(ssm.astype(BF16), wsu_ref[...], preferred_element_type=F32)
    merged = gs_ref[...].astype(F32) * su + ga_ref[...].astype(F32) * au
    mix = jnp.dot(merged.astype(BF16), wo_ref[...], preferred_element_type=F32)
    x1 = x_ref[...] + g1 * mix
    x1_ref[...] = x1
    ms = jnp.mean(x1 * x1, axis=-1, keepdims=True)
    h2 = (x1 * lax.rsqrt(ms + EPS) * ln_ref[...]) * (1.0 + sc2) + sh2
    for i in range(d_model // LANES):
        h2_ref[pl.ds(i, tm, stride=ROW_TILE), :] = h2[:, i * LANES:(i + 1) * LANES]
    lg = lax.dot_general(wr_ref[...], h2, (((1,), (1,)), ((), ())), preferred_element_type=F32,
                         precision=lax.Precision.HIGHEST)
    lg = lg - jnp.max(lg, axis=0, keepdims=True)
    ex = jnp.exp(lg)
    aff = ex / jnp.sum(ex, axis=0, keepdims=True)
    aff_ref[0] = aff
    pad = jnp.zeros((LANES - aff.shape[0], aff.shape[1]), F32)
    afft_ref[...] = jnp.concatenate([aff, pad], axis=0).T


def _post_call(y, attn, gs, ga, x2, mod3, ln_w, wg, bg, wsu, wau, wo, wr_t, *, seq, tm):
    t, d = x2.shape
    d_ssm = y.shape[1]
    d_attn = attn.shape[1]
    ne = wr_t.shape[0]
    tiles_per_seq = seq // tm
    row = lambda i: (i, 0)
    const = lambda i: (0, 0)
    return pl.pallas_call(
        functools.partial(_post_kernel, d_model=d),
        grid=(t // tm,),
        in_specs=[pl.BlockSpec((tm, d_ssm), row),
                  pl.BlockSpec((tm, d_attn), row),
                  pl.BlockSpec((tm, d), row),
                  pl.BlockSpec((tm, d), row),
                  pl.BlockSpec((tm, d), row),
                  pl.BlockSpec((1, 1, mod3.shape[2]), lambda i: (i // tiles_per_seq, 0, 0)),
                  pl.BlockSpec((1, d), const),
                  pl.BlockSpec(wg.shape, const),
                  pl.BlockSpec((1, d_ssm), const),
                  pl.BlockSpec(wsu.shape, const),
                  pl.BlockSpec(wau.shape, const),
                  pl.BlockSpec(wo.shape, const),
                  pl.BlockSpec(wr_t.shape, const)],
        out_specs=[pl.BlockSpec((tm, d), row),
                   pl.BlockSpec((tm * ROW_TILE, LANES), row),
                   pl.BlockSpec((1, ne, tm), lambda i: (i // tiles_per_seq, 0, i % tiles_per_seq)),
                   pl.BlockSpec((tm, LANES), row)],
        out_shape=[jax.ShapeDtypeStruct((t, d), F32),
                   jax.ShapeDtypeStruct((t * ROW_TILE, LANES), F32),
                   jax.ShapeDtypeStruct((t // seq, ne, seq), F32),
                   jax.ShapeDtypeStruct((t, LANES), F32)],
        name="post_mix",
        compiler_params=_cparams(
            ("arbitrary",),
            blocks=[((tm, d_ssm), F32), ((tm, d_attn), BF16), ((tm, d), BF16), ((tm, d), BF16),
                    ((tm, d), F32), ((1, 1, mod3.shape[2]), F32), ((1, d), F32), (wg.shape, BF16),
                    ((1, d_ssm), F32), (wsu.shape, BF16), (wau.shape, BF16), (wo.shape, BF16),
                    (wr_t.shape, F32), ((tm, d), F32), ((tm * ROW_TILE, LANES), F32),
                    ((1, ne, tm), F32), ((tm, LANES), F32)],
            temps=[((tm, d), F32)] * 6),
    )(y, attn, gs, ga, x2, mod3, ln_w, wg, bg, wsu, wau, wo, wr_t)


def _lane_cumsum(mask_f, tri):
    ne, seq = mask_f.shape
    run = jnp.zeros((ne, 1), F32)
    parts = []
    for j in range(seq // LANES):
        blk = mask_f[:, j * LANES:(j + 1) * LANES]
        cs = jnp.dot(blk.astype(BF16), tri, preferred_element_type=F32) + run
        parts.append(cs)
        run = run + jnp.sum(blk, axis=1, keepdims=True)
    return jnp.concatenate(parts, axis=-1)


def _select_kernel(aff_ref, tri_ref, idx_ref, half_ref, *, cap, seq, ts):
    aff = aff_ref[0]
    ne = aff.shape[0]
    tri = tri_ref[...]

    def body(i, thr):
        cand = thr | (jnp.int32(1) << (30 - i))
        cnt = jnp.sum(jnp.where(aff >= pltpu.bitcast(cand, F32), 1.0, 0.0), axis=1, keepdims=True)
        return jnp.where(cnt >= cap, cand, thr)

    thr = lax.fori_loop(0, 31, body, jnp.zeros((ne, 1), jnp.int32))
    gt = aff >= pltpu.bitcast(thr + 1, F32)
    eq = (aff >= pltpu.bitcast(thr, F32)) & jnp.logical_not(gt)
    need = cap - jnp.sum(jnp.where(gt, 1.0, 0.0), axis=1, keepdims=True)
    eq_f = jnp.where(eq, 1.0, 0.0)
    eq_rank = _lane_cumsum(eq_f, tri) - eq_f
    sel = gt | (eq & (eq_rank < need))
    sel_f = jnp.where(sel, 1.0, 0.0)
    csum = _lane_cumsum(sel_f, tri)
    rank = jnp.where(sel, csum - sel_f, -1.0)
    half_ref[0] = csum[:, seq // 2 - 1:seq // 2].astype(jnp.int32)

    slot = lax.broadcasted_iota(jnp.int32, (cap, ts), 0).astype(F32)
    lane_e = lax.broadcasted_iota(jnp.int32, (cap, ne), 1)
    idx = jnp.zeros((cap, ne), F32)
    for e in range(ne):
        col = jnp.zeros((cap, 1), F32)
        for j in range(seq // ts):
            tok = (lax.broadcasted_iota(jnp.int32, (1, ts), 1) + j * ts).astype(F32)
            hit = rank[e:e + 1, j * ts:(j + 1) * ts] == slot
            col = col + jnp.sum(jnp.where(hit, tok, 0.0), axis=1, keepdims=True)
        idx = jnp.where(lane_e == e, col, idx)
    idx_ref[0] = idx.astype(jnp.int32)


def _select_call(aff_t, tri, *, cap):
    b, ne, seq = aff_t.shape
    ts = min(1024, seq)
    return pl.pallas_call(
        functools.partial(_select_kernel, cap=cap, seq=seq, ts=ts),
        grid=(b,),
        in_specs=[pl.BlockSpec((1, ne, seq), lambda i: (i, 0, 0)),
                  pl.BlockSpec((LANES, LANES), lambda i: (0, 0))],
        out_specs=[pl.BlockSpec((1, cap, ne), lambda i: (i, 0, 0)),
                   pl.BlockSpec((1, ne, 1), lambda i: (i, 0, 0))],
        out_shape=[jax.ShapeDtypeStruct((b, cap, ne), jnp.int32),
                   jax.ShapeDtypeStruct((b, ne, 1), jnp.int32)],
        name="expert_select",
        compiler_params=_cparams(
            ("arbitrary",),
            blocks=[((1, ne, seq), F32), ((LANES, LANES), BF16), ((1, cap, ne), jnp.int32),
                    ((1, ne, 1), jnp.int32)],
            temps=[((ne, seq), F32)] * 8 + [((cap, ts), F32)] * 4),
    )(aff_t, tri)


GATHER_UNROLL = 8


def _gather_kernel(idx_ref, h_ref, aff_ref, xg_ref, gc_ref, rows_ref, arow_ref, *, cap, ne):
    b = pl.program_id(0)
    e = pl.program_id(1)
    base = (b * ne + e) * cap

    def body(g, carry):
        for k in range(GATHER_UNROLL):
            c = g * GATHER_UNROLL + k
            r = idx_ref[base + c]
            src = pl.ds(pl.multiple_of(r * ROW_TILE, ROW_TILE), ROW_TILE)
            dst = pl.ds(pl.multiple_of(c * ROW_TILE, ROW_TILE), ROW_TILE)
            rows_ref[dst, :] = h_ref[src, :]
            arow_ref[pl.ds(c, 1), :] = aff_ref[pl.ds(r, 1), :]
        return carry

    lax.fori_loop(0, cap // GATHER_UNROLL, body, 0)
    xg_ref[0] = jnp.concatenate([rows_ref[pl.ds(i, cap, stride=ROW_TILE), :] for i in range(ROW_TILE)],
                                axis=1).astype(xg_ref.dtype)
    lane = lax.broadcasted_iota(jnp.int32, arow_ref.shape, 1)
    gc_ref[0] = jnp.sum(jnp.where(lane == e, arow_ref[...], 0.0), axis=1, keepdims=True)


def _gather_call(idx_flat, h2t, aff_tm, *, batch, seq, cap, ne):
    d = ROW_TILE * LANES
    assert h2t.shape == (batch * seq * ROW_TILE, LANES)
    grid_spec = pltpu.PrefetchScalarGridSpec(
        num_scalar_prefetch=1,
        grid=(batch, ne),
        in_specs=[pl.BlockSpec((seq * ROW_TILE, LANES), lambda b, e, idx: (b, 0)),
                  pl.BlockSpec((seq, LANES), lambda b, e, idx: (b, 0))],
        out_specs=[pl.BlockSpec((1, cap, d), lambda b, e, idx: (e, b, 0)),
                   pl.BlockSpec((1, cap, 1), lambda b, e, idx: (e, b, 0))],
        scratch_shapes=[pltpu.VMEM((cap * ROW_TILE, LANES), F32), pltpu.VMEM((cap, LANES), F32)],
    )
    return pl.pallas_call(
        functools.partial(_gather_kernel, cap=cap, ne=ne),
        grid_spec=grid_spec,
        out_shape=[jax.ShapeDtypeStruct((ne, batch * cap, d), BF16),
                   jax.ShapeDtypeStruct((ne, batch * cap, 1), F32)],
        name="expert_gather",
        compiler_params=_cparams(
            ("arbitrary", "arbitrary"),
            blocks=[((seq * ROW_TILE, LANES), F32), ((seq, LANES), F32), ((1, cap, d), BF16),
                    ((1, cap, 1), F32)],
            scratch=[((cap * ROW_TILE, LANES), F32), ((cap, LANES), F32)],
            temps=[((cap, d), F32)] * 2),
    )(idx_flat, h2t, aff_tm)


def _ffn_kernel(x_ref, gc_ref, wg_ref, wu_ref, wd_ref, ye_ref, *, rows):
    f = pl.program_id(1)

    @pl.when(f == 0)
    def _init():
        ye_ref[...] = jnp.zeros_like(ye_ref)

    wg = wg_ref[0, 0].astype(BF16)
    wu = wu_ref[0, 0].astype(BF16)
    wd = wd_ref[0, 0].astype(BF16)
    nblk = x_ref.shape[1] // rows

    def up(r):
        x = x_ref[0, r * rows:(r + 1) * rows, :]
        return (jnp.dot(x, wg, preferred_element_type=F32), jnp.dot(x, wu, preferred_element_type=F32))

    nxt = up(0)
    for r in range(nblk):
        hg, hu = nxt
        if r + 1 < nblk:
            nxt = up(r + 1)
        hid = (hg * jax.nn.sigmoid(hg) * hu).astype(BF16)
        ye_ref[0, r * rows:(r + 1) * rows, :] += jnp.dot(hid, wd, preferred_element_type=F32)

    @pl.when(f == pl.num_programs(1) - 1)
    def _gate():
        ye_ref[0] = ye_ref[0] * gc_ref[0]


def _ffn_call(xg, gc, wg, wu, wd, *, layer, tf, rows):
    ne, n, d = xg.shape
    dff = wg.shape[3]
    return pl.pallas_call(
        functools.partial(_ffn_kernel, rows=rows),
        grid=(ne, dff // tf),
        in_specs=[pl.BlockSpec((1, n, d), lambda e, f: (e, 0, 0)),
                  pl.BlockSpec((1, n, 1), lambda e, f: (e, 0, 0)),
                  pl.BlockSpec((1, 1, d, tf), lambda e, f: (layer, e, 0, f)),
                  pl.BlockSpec((1, 1, d, tf), lambda e, f: (layer, e, 0, f)),
                  pl.BlockSpec((1, 1, tf, d), lambda e, f: (layer, e, f, 0))],
        out_specs=pl.BlockSpec((1, n, d), lambda e, f: (e, 0, 0)),
        out_shape=jax.ShapeDtypeStruct((ne, n, d), F32),
        name="expert_ffn",
        compiler_params=_cparams(
            ("arbitrary", "arbitrary"),
            blocks=[((1, n, d), BF16), ((1, n, 1), F32), ((1, 1, d, tf), F32), ((1, 1, d, tf), F32),
                    ((1, 1, tf, d), F32), ((1, n, d), F32)],
            temps=[((d, tf), BF16)] * 3 + [((rows, tf), F32)] * 5 + [((rows, d), F32)]),
    )(xg, gc, wg, wu, wd)


SCATTER_UNROLL = 8
SEQ_HALVES = 2


def _combine_kernel(idx_ref, half_ref, ye_ref, x1_ref, mod_ref, o_ref, acc_ref, *, cap, ne, rows,
                    d_model):
    i = pl.program_id(0)
    e = pl.program_id(1)
    b = i // SEQ_HALVES
    hh = i % SEQ_HALVES

    @pl.when(e == 0)
    def _init():
        o_ref[...] = jnp.zeros_like(o_ref)
        acc_ref[...] = jnp.zeros_like(acc_ref)

    mid = half_ref[b * ne + e]
    lo = jnp.where(hh == 0, 0, mid)
    hi = jnp.where(hh == 0, mid, cap)
    base = (b * ne + e) * cap
    off = hh * rows

    def add_row(c, dst_ref):
        r = idx_ref[base + c] - off
        dst_ref[pl.ds(r, 1), :] += ye_ref[0, pl.ds(c, 1), :]

    def group(g, carry):
        for k in range(SCATTER_UNROLL):
            add_row(lo + g * SCATTER_UNROLL + k, o_ref if k % 2 == 0 else acc_ref)
        return carry

    ngroup = (hi - lo) // SCATTER_UNROLL
    lax.fori_loop(0, ngroup, group, 0)

    def tail(c, carry):
        add_row(c, o_ref)
        return carry

    lax.fori_loop(lo + ngroup * SCATTER_UNROLL, hi, tail, 0)

    @pl.when(e == pl.num_programs(1) - 1)
    def _fin():
        g2 = mod_ref[0][:, 5 * d_model:6 * d_model]
        o_ref[...] = x1_ref[...] + g2 * (o_ref[...] + acc_ref[...])


def _combine_call(idx_flat, half_flat, ye, x1, mod3, *, batch, seq, cap):
    t, d = x1.shape
    ne = ye.shape[0]
    rows = seq // SEQ_HALVES
    grid_spec = pltpu.PrefetchScalarGridSpec(
        num_scalar_prefetch=2,
        grid=(batch * SEQ_HALVES, ne),
        in_specs=[pl.BlockSpec((1, cap, d), lambda i, e, idx, half: (e, i // SEQ_HALVES, 0)),
                  pl.BlockSpec((rows, d), lambda i, e, idx, half: (i, 0)),
                  pl.BlockSpec((1, 1, mod3.shape[2]), lambda i, e, idx, half: (i // SEQ_HALVES, 0, 0))],
        out_specs=pl.BlockSpec((rows, d), lambda i, e, idx, half: (i, 0)),
        scratch_shapes=[pltpu.VMEM((rows, d), F32)],
    )
    return pl.pallas_call(
        functools.partial(_combine_kernel, cap=cap, ne=ne, rows=rows, d_model=d),
        grid_spec=grid_spec,
        out_shape=jax.ShapeDtypeStruct((t, d), F32),
        name="moe_combine",
        compiler_params=_cparams(
            ("arbitrary", "arbitrary"),
            blocks=[((1, cap, d), F32), ((rows, d), F32), ((1, 1, mod3.shape[2]), F32), ((rows, d), F32)],
            scratch=[((rows, d), F32)],
            temps=[((SUBLANES_32BIT * SCATTER_UNROLL, d), F32)] * 4),
    )(idx_flat, half_flat, ye, x1, mod3)


def kernel(x, c, w_mod, b_mod, ln1_w, ln2_w, w_in, ssm_lam_re, ssm_lam_im, ssm_log_dt, ssm_b_re,
           ssm_b_im, ssm_c_re, ssm_c_im, ssm_d, w_glu, b_glu, q_norm_w, k_norm_w, w_ssm_up,
           w_attn_up, w_out, w_router, w_exp_gate, w_exp_up, w_exp_down):
    batch, seq, d = x.shape
    depth = w_mod.shape[0]
    t = batch * seq
    d_attn = N_Q_HEADS * HEAD_DIM
    d_kv = N_KV_HEADS * HEAD_DIM
    cap = CAPACITY_FACTOR * seq // N_EXPERTS
    tm = min(512, seq)
    tq = min(256, seq)

    tabs = _rope_tables(seq)
    head_id = jnp.arange(d_attn) // HEAD_DIM
    hm = (head_id[:, None] == head_id[None, :]).astype(BF16)
    tri = (jnp.arange(LANES)[:, None] <= jnp.arange(LANES)[None, :]).astype(BF16)
    c_pad = jnp.zeros((8, d), F32).at[:batch].set(c)
    mod_all = _mod_call(c_pad, w_mod, b_mod)[:, :batch]
    wa_all, wb_all, a_re_all, a_im_all = _s5_weights(ssm_lam_re, ssm_lam_im, ssm_log_dt, ssm_b_re,
                                                     ssm_b_im, ssm_c_re, ssm_c_im, ssm_d)

    x2 = x.reshape(t, d)
    for l in range(depth):
        mod3 = mod_all[l].reshape(batch, 1, 6 * d)
        qw = jnp.tile(q_norm_w[l], N_Q_HEADS).reshape(1, d_attn)
        kw = jnp.tile(k_norm_w[l], N_KV_HEADS).reshape(1, d_kv)
        u, q, k, v, gs, ga = _in_call(x2, mod3, ln1_w[l].reshape(1, d), w_in[l].astype(BF16), tabs,
                                      qw, kw, hm, seq=seq, tm=tm)
        y = _s5_mixer(u, wa_all[l], wb_all[l], a_re_all[l], a_im_all[l], batch=batch, seq=seq)
        attn = _attn_call(q, k, v, seq=seq, tq=tq)
        x1, h2, aff_t, aff_tm = _post_call(
            y, attn, gs, ga, x2, mod3, ln2_w[l].reshape(1, d), w_glu[l].astype(BF16),
            b_glu[l].reshape(1, -1), w_ssm_up[l].astype(BF16), w_attn_up[l].astype(BF16),
            w_out[l].astype(BF16), w_router[l].T, seq=seq, tm=tm)
        idx_t, half = _select_call(aff_t, tri, cap=cap)
        idx_flat = jnp.swapaxes(idx_t, 1, 2).reshape(-1)
        half_flat = half.reshape(-1)
        xg, gc = _gather_call(idx_flat, h2, aff_tm, batch=batch, seq=seq, cap=cap, ne=N_EXPERTS)
        ye = _ffn_call(xg, gc, w_exp_gate, w_exp_up, w_exp_down, layer=l, tf=512,
                       rows=min(1024, batch * cap))
        x2 = _combine_call(idx_flat, half_flat, ye, x1, mod3, batch=batch, seq=seq, cap=cap)
    return x2.reshape(batch, seq, d)
```

```python
import functools
import math

import jax
import jax.numpy as jnp
from jax import lax
from jax.experimental import pallas as pl
from jax.experimental.pallas import tpu as pltpu

F32 = jnp.float32
BF16 = jnp.bfloat16

HEAD_DIM = 64
N_Q_HEADS = 8
N_KV_HEADS = 2
Q_PER_KV = N_Q_HEADS // N_KV_HEADS
ROT_HALF = HEAD_DIM // 2
ROPE_THETA = 10000.0
GRID_W = 64
SSM_GROUP = 16
SSM_STATE = 64
N_EXPERTS = 16
CAPACITY_FACTOR = 2
EPS = 1e-6

S5_CHUNK = 16
S5_PAIR = 2
LANES = 128
V7X_VMEM_BYTES = 64 * 1024 * 1024
LOG2E = 1.4426950408889634


PIPELINE_BUFFERS = 2
SUBLANES_32BIT = 8


def _vmem_bytes(shape, dtype):
    itemsize = jnp.dtype(dtype).itemsize
    sub = SUBLANES_32BIT * 4 // itemsize
    lead = math.prod(shape[:-2]) if len(shape) > 2 else 1
    rows = shape[-2] if len(shape) > 1 else 1
    return lead * (-(-rows // sub) * sub) * (-(-shape[-1] // LANES) * LANES) * itemsize


def _cparams(semantics, blocks, scratch=(), temps=()):
    need = (PIPELINE_BUFFERS * sum(_vmem_bytes(s, d) for s, d in blocks)
            + sum(_vmem_bytes(s, d) for s, d in scratch) + sum(_vmem_bytes(s, d) for s, d in temps))
    return pltpu.CompilerParams(dimension_semantics=semantics,
                                vmem_limit_bytes=min(need, V7X_VMEM_BYTES))


def _mod_kernel(c_ref, w_ref, b_ref, o_ref):
    c = c_ref[...]
    ca = c * jax.nn.sigmoid(c)
    o_ref[0] = jnp.dot(ca, w_ref[0], preferred_element_type=F32,
                       precision=lax.Precision.HIGHEST) + b_ref[0]


def _mod_call(c_pad, w_mod, b_mod):
    depth, d, n = w_mod.shape
    rows = c_pad.shape[0]
    tn = 1536
    return pl.pallas_call(
        _mod_kernel,
        grid=(depth, n // tn),
        in_specs=[pl.BlockSpec((rows, d), lambda l, j: (0, 0)),
                  pl.BlockSpec((1, d, tn), lambda l, j: (l, 0, j)),
                  pl.BlockSpec((1, 1, tn), lambda l, j: (l, 0, j))],
        out_specs=pl.BlockSpec((1, rows, tn), lambda l, j: (l, 0, j)),
        out_shape=jax.ShapeDtypeStruct((depth, rows, n), F32),
        name="adaln_mod",
        compiler_params=_cparams(
            ("arbitrary", "arbitrary"),
            blocks=[((rows, d), F32), ((1, d, tn), F32), ((1, 1, tn), F32), ((1, rows, tn), F32)],
            temps=[((d, tn), F32)] * 2),
    )(c_pad, w_mod, b_mod.reshape(depth, 1, n))


def _head_mean_square(xsq, hm):
    hi = xsq.astype(BF16)
    lo = (xsq - hi.astype(F32)).astype(BF16)
    s = jnp.dot(hi, hm, preferred_element_type=F32) + jnp.dot(lo, hm, preferred_element_type=F32)
    return s * (1.0 / HEAD_DIM)


def _rope_block(blk, cos, sa, sb):
    return (blk * cos + pltpu.roll(blk, LANES - ROT_HALF // 2, 1) * sa
            + pltpu.roll(blk, ROT_HALF // 2, 1) * sb)


def _in_kernel(x_ref, mod_ref, ln_ref, w_ref, cos_ref, sa_ref, sb_ref, qw_ref, kw_ref, hm_ref,
               u_ref, q_ref, k_ref, v_ref, gs_ref, ga_ref, *, d_model, d_ssm, d_attn, d_kv):
    x = x_ref[...]
    mod = mod_ref[0]
    sh = mod[:, 0:d_model]
    sc = mod[:, d_model:2 * d_model]
    ms = jnp.mean(x * x, axis=-1, keepdims=True)
    h = (x * lax.rsqrt(ms + EPS) * ln_ref[...]) * (1.0 + sc) + sh
    hb = h.astype(BF16)
    hm = hm_ref[...]
    cos = cos_ref[...]
    sa = sa_ref[...]
    sb = sb_ref[...]
    qscale = (HEAD_DIM ** -0.5) * LOG2E

    def finish_u(p):
        u_ref[...] = p.astype(u_ref.dtype)

    def finish_q(q):
        qn = q * lax.rsqrt(_head_mean_square(q * q, hm) + EPS) * qw_ref[...]
        qr = [_rope_block(qn[:, j * LANES:(j + 1) * LANES], cos, sa, sb) for j in range(d_attn // LANES)]
        q_ref[...] = (jnp.concatenate(qr, axis=-1) * qscale).astype(q_ref.dtype)

    def finish_kv(p):
        k = p[:, :d_kv]
        v_ref[0] = p[:, d_kv:].T.astype(v_ref.dtype)
        kn = k * lax.rsqrt(_head_mean_square(k * k, hm[:d_kv, :d_kv]) + EPS) * kw_ref[...]
        kr = [_rope_block(kn[:, j * LANES:(j + 1) * LANES], cos, sa, sb) for j in range(d_kv // LANES)]
        k_ref[...] = jnp.concatenate(kr, axis=-1).astype(k_ref.dtype)

    def finish_gs(p):
        gs_ref[...] = jax.nn.sigmoid(p).astype(gs_ref.dtype)

    def finish_ga(p):
        ga_ref[...] = jax.nn.sigmoid(p).astype(ga_ref.dtype)

    groups = ((d_ssm, finish_u), (d_attn, finish_q), (2 * d_kv, finish_kv),
              (d_model, finish_gs), (d_model, finish_ga))
    starts = [sum(g[0] for g in groups[:i]) for i in range(len(groups))]

    def project(i):
        return jnp.dot(hb, w_ref[:, starts[i]:starts[i] + groups[i][0]], preferred_element_type=F32)

    nxt = project(0)
    for i, (_, finish) in enumerate(groups):
        cur = nxt
        if i + 1 < len(groups):
            nxt = project(i + 1)
        finish(cur)


def _in_call(x2, mod3, ln_w, w_in, tabs, qw, kw, hm, *, seq, tm):
    t, d = x2.shape
    n_in = w_in.shape[1]
    d_ssm = d // 2
    d_attn = N_Q_HEADS * HEAD_DIM
    d_kv = N_KV_HEADS * HEAD_DIM
    tiles_per_seq = seq // tm
    cos, sa, sb = tabs
    kern = functools.partial(_in_kernel, d_model=d, d_ssm=d_ssm, d_attn=d_attn, d_kv=d_kv)
    row = lambda i: (i, 0)
    const = lambda i: (0, 0)
    tab = lambda i: (i % tiles_per_seq, 0)
    return pl.pallas_call(
        kern,
        grid=(t // tm,),
        in_specs=[pl.BlockSpec((tm, d), row),
                  pl.BlockSpec((1, 1, mod3.shape[2]), lambda i: (i // tiles_per_seq, 0, 0)),
                  pl.BlockSpec((1, d), const),
                  pl.BlockSpec((d, n_in), const),
                  pl.BlockSpec((tm, LANES), tab),
                  pl.BlockSpec((tm, LANES), tab),
                  pl.BlockSpec((tm, LANES), tab),
                  pl.BlockSpec((1, d_attn), const),
                  pl.BlockSpec((1, d_kv), const),
                  pl.BlockSpec((d_attn, d_attn), const)],
        out_specs=[pl.BlockSpec((tm, d_ssm), row),
                   pl.BlockSpec((tm, d_attn), row),
                   pl.BlockSpec((tm, d_kv), row),
                   pl.BlockSpec((1, d_kv, tm), lambda i: (i // tiles_per_seq, 0, i % tiles_per_seq)),
                   pl.BlockSpec((tm, d), row),
                   pl.BlockSpec((tm, d), row)],
        out_shape=[jax.ShapeDtypeStruct((t, d_ssm), F32),
                   jax.ShapeDtypeStruct((t, d_attn), BF16),
                   jax.ShapeDtypeStruct((t, d_kv), BF16),
                   jax.ShapeDtypeStruct((t // seq, d_kv, seq), BF16),
                   jax.ShapeDtypeStruct((t, d), BF16),
                   jax.ShapeDtypeStruct((t, d), BF16)],
        name="in_proj",
        compiler_params=_cparams(
            ("arbitrary",),
            blocks=[((tm, d), F32), ((1, 1, mod3.shape[2]), F32), ((1, d), F32), ((d, n_in), BF16),
                    ((tm, LANES), F32), ((tm, LANES), F32), ((tm, LANES), F32),
                    ((1, d_attn), F32), ((1, d_kv), F32), ((d_attn, d_attn), BF16),
                    ((tm, d_ssm), F32), ((tm, d_attn), BF16), ((tm, d_kv), BF16), ((1, d_kv, tm), BF16),
                    ((tm, d), BF16), ((tm, d), BF16)],
            temps=[((tm, d), F32)] * 6),
    )(x2, mod3, ln_w, w_in, cos, sa, sb, qw, kw, hm)


def _rope_tables(seq):
    pos = jnp.arange(seq)
    row = (pos // GRID_W).astype(F32)
    col = (pos % GRID_W).astype(F32)
    inv = 1.0 / (ROPE_THETA ** (jnp.arange(0, ROT_HALF, 2, dtype=F32) / ROT_HALF))
    ang_r = row[:, None] * inv[None, :]
    ang_c = col[:, None] * inv[None, :]
    zeros = jnp.zeros_like(ang_r)
    cos_h = jnp.concatenate([jnp.cos(ang_r), jnp.cos(ang_r), jnp.cos(ang_c), jnp.cos(ang_c)], axis=-1)
    sa_h = jnp.concatenate([-jnp.sin(ang_r), zeros, -jnp.sin(ang_c), zeros], axis=-1)
    sb_h = jnp.concatenate([zeros, jnp.sin(ang_r), zeros, jnp.sin(ang_c)], axis=-1)
    rep = LANES // HEAD_DIM
    return tuple(jnp.tile(a, (1, rep)) for a in (cos_h, sa_h, sb_h))


def _attn_heads_out(o_t, tq):
    return [o_t[:, g * tq:(g + 1) * tq].T for g in range(Q_PER_KV)]


def _attn_kernel(q_ref, k_ref, vt_ref, o_ref, *, kb):
    tq = q_ref.shape[0]
    seq = k_ref.shape[0]
    nblk = seq // kb
    width = Q_PER_KV * HEAD_DIM
    q_ts = []
    for kvh in range(N_KV_HEADS):
        qt = q_ref[:, kvh * width:(kvh + 1) * width].astype(F32).T.astype(BF16)
        q_ts.append(jnp.concatenate([qt[g * HEAD_DIM:(g + 1) * HEAD_DIM] for g in range(Q_PER_KV)],
                                    axis=1))

    def scores(kvh, j):
        lo = kvh * HEAD_DIM
        return jnp.dot(k_ref[j * kb:(j + 1) * kb, lo:lo + HEAD_DIM], q_ts[kvh],
                       preferred_element_type=F32)

    def weighted_v(kvh, j, p):
        lo = kvh * HEAD_DIM
        return jnp.dot(vt_ref[0, lo:lo + HEAD_DIM, j * kb:(j + 1) * kb], p.astype(BF16),
                       preferred_element_type=F32)

    outs = []
    worst = jnp.zeros((1, 1), F32)
    for kvh in range(N_KV_HEADS):
        s0 = scores(kvh, 0)
        m0 = jnp.max(s0, axis=0, keepdims=True)
        l = jnp.zeros_like(m0)
        acc = jnp.zeros((HEAD_DIM, m0.shape[1]), F32)
        s_cur = s0
        for j in range(nblk):
            s_next = scores(kvh, j + 1) if j + 1 < nblk else None
            p = jnp.exp2(s_cur - m0)
            l = l + jnp.sum(p, axis=0, keepdims=True)
            acc = acc + weighted_v(kvh, j, p)
            s_cur = s_next
        worst = jnp.maximum(worst, jnp.max(l, axis=1, keepdims=True))
        outs.extend(_attn_heads_out(acc / l, tq))
    o_ref[...] = jnp.concatenate(outs, axis=-1).astype(o_ref.dtype)

    @pl.when(worst[0, 0] > jnp.finfo(F32).max)
    def _exact():
        outs = []
        for kvh in range(N_KV_HEADS):
            m = jnp.full((1, Q_PER_KV * tq), -jnp.inf, F32)
            l = jnp.zeros_like(m)
            acc = jnp.zeros((HEAD_DIM, Q_PER_KV * tq), F32)
            for j in range(nblk):
                s = scores(kvh, j)
                m_new = jnp.maximum(m, jnp.max(s, axis=0, keepdims=True))
                alpha = jnp.exp2(m - m_new)
                p = jnp.exp2(s - m_new)
                l = l * alpha + jnp.sum(p, axis=0, keepdims=True)
                acc = acc * alpha + weighted_v(kvh, j, p)
                m = m_new
            outs.extend(_attn_heads_out(acc / l, tq))
        o_ref[...] = jnp.concatenate(outs, axis=-1).astype(o_ref.dtype)


def _attn_call(q, k, vt, *, seq, tq):
    t, d_attn = q.shape
    d_kv = k.shape[1]
    nq = seq // tq
    kb = min(256, seq)
    return pl.pallas_call(
        functools.partial(_attn_kernel, kb=kb),
        grid=(t // seq, nq),
        in_specs=[pl.BlockSpec((tq, d_attn), lambda b, i: (b * nq + i, 0)),
                  pl.BlockSpec((seq, d_kv), lambda b, i: (b, 0)),
                  pl.BlockSpec((1, d_kv, seq), lambda b, i: (b, 0, 0))],
        out_specs=pl.BlockSpec((tq, d_attn), lambda b, i: (b * nq + i, 0)),
        out_shape=jax.ShapeDtypeStruct((t, d_attn), BF16),
        name="attention",
        compiler_params=_cparams(
            ("arbitrary", "arbitrary"),
            blocks=[((tq, d_attn), BF16), ((seq, d_kv), BF16), ((1, d_kv, seq), BF16), ((tq, d_attn), BF16)],
            temps=[((kb, Q_PER_KV * tq), F32)] * 16),
    )(q, k, vt)


def _s5_weights(lam_re, lam_im, log_dt, b_re, b_im, c_re, c_im, d_skip):
    hp = lax.Precision.HIGHEST
    nl, _, ng, p = lam_re.shape
    cg = SSM_GROUP
    L = S5_CHUNK
    kd = L * cg
    dt = jnp.exp(log_dt)[..., None]
    zr = lam_re * dt
    zi = lam_im * dt

    def cpow(e):
        mag = jnp.exp(zr[..., None] * e)
        return mag * jnp.cos(zi[..., None] * e), mag * jnp.sin(zi[..., None] * e)

    lb_re, lb_im = jnp.exp(zr) * jnp.cos(zi), jnp.exp(zr) * jnp.sin(zi)
    nr, ni = lb_re - 1.0, lb_im
    den = lam_re * lam_re + lam_im * lam_im
    f_re = ((nr * lam_re + ni * lam_im) / den)[..., None]
    f_im = ((ni * lam_re - nr * lam_im) / den)[..., None]
    bb_re = f_re * b_re - f_im * b_im
    bb_im = f_re * b_im + f_im * b_re

    m_idx = jnp.arange(L + 1, dtype=F32)
    expo = jnp.stack([m_idx, L - m_idx])[None, :, None, None, :]
    pw_re, pw_im = cpow(expo)
    rep = jnp.repeat(jnp.eye(L + 1, dtype=F32), cg, axis=1)
    til = jnp.tile(jnp.eye(cg, dtype=F32), (1, L + 1))
    pwx_re = jnp.einsum('dxgpm,mn->dxgpn', pw_re, rep, precision=hp)
    pwx_im = jnp.einsum('dxgpm,mn->dxgpn', pw_im, rep, precision=hp)
    ct_re = jnp.einsum('dxgcp,cn->dxgpn', c_re, til, precision=hp)
    ct_im = jnp.einsum('dxgcp,cn->dxgpn', c_im, til, precision=hp)
    ca_re = ct_re * pwx_re - ct_im * pwx_im
    ca_im = ct_re * pwx_im + ct_im * pwx_re
    ef_re, ef_im = ca_re[:, 0, ..., :kd], ca_im[:, 0, ..., :kd]
    w2f_re, w2f_im = ca_re[:, 0, ..., cg:], ca_im[:, 0, ..., cg:]
    w2b_re, w2b_im = ca_re[:, 1, ..., :kd], ca_im[:, 1, ..., :kd]
    eb_re, eb_im = ca_re[:, 1, ..., cg:], ca_im[:, 1, ..., cg:]

    def lag_rows(x, e_re, e_im):
        return (jnp.einsum('dgpe,dgpn->dgen', bb_re[:, x], e_re, precision=hp)
                - jnp.einsum('dgpe,dgpn->dgen', bb_im[:, x], e_im, precision=hp))

    kf = lag_rows(0, ef_re, ef_im)
    kb = lag_rows(1, eb_re, eb_im)
    zeros = jnp.zeros_like(kf)
    pf = jnp.concatenate([zeros, kf], axis=-1)
    pb = jnp.concatenate([kb, zeros], axis=-1)
    tf = jnp.stack([pf[..., kd - cg * s:2 * kd - cg * s] for s in range(L)], axis=2)
    tb = jnp.stack([pb[..., cg * (L - 1 - s):cg * (L - 1 - s) + kd] for s in range(L)], axis=2)
    skip = jnp.eye(kd, dtype=F32) * jnp.tile(d_skip.reshape(nl, ng, 1, cg), (1, 1, 1, L))
    tmat = (tf + tb).reshape(nl, ng, kd, kd) + skip

    s_idx = jnp.arange(L, dtype=F32)
    sexp = jnp.stack([L - 1 - s_idx, s_idx])[None, :, None, None, :]
    ps_re, ps_im = cpow(sexp)
    w1_re = (jnp.einsum('dxgps,dxgpe->dxgsep', ps_re, bb_re)
             - jnp.einsum('dxgps,dxgpe->dxgsep', ps_im, bb_im)).reshape(nl, 2, ng, kd, p)
    w1_im = (jnp.einsum('dxgps,dxgpe->dxgsep', ps_re, bb_im)
             + jnp.einsum('dxgps,dxgpe->dxgsep', ps_im, bb_re)).reshape(nl, 2, ng, kd, p)

    wa = jnp.concatenate([tmat, w1_re[:, 0], w1_im[:, 0], w1_re[:, 1], w1_im[:, 1]], axis=-1)
    wb = jnp.concatenate([w2f_re, -w2f_im, w2b_re, -w2b_im], axis=2)
    al_re, al_im = cpow(jnp.float32(L))
    a_re = al_re.reshape(nl, 2, ng * p)
    a_im = al_im.reshape(nl, 2, ng * p)
    return wa.astype(BF16), wb.astype(BF16), a_re, a_im


GROUPS_PER_VREG = LANES // SSM_GROUP
PAIRS_PER_VREG = GROUPS_PER_VREG // S5_PAIR
TOKENS_PER_HALF = LANES // SSM_GROUP


def _granule_transpose(arrs):
    gran = lax.broadcasted_iota(jnp.int32, arrs[0].shape, 1) // SSM_GROUP
    cur = list(arrs)
    for dist in (4, 2, 1):
        keep = (gran & dist) == 0
        nxt = list(cur)
        for i in range(len(cur)):
            if i & dist:
                continue
            a, b = cur[i], cur[i + dist]
            nxt[i] = jnp.where(keep, a, pltpu.roll(b, dist * SSM_GROUP, 1))
            nxt[i + dist] = jnp.where(keep, pltpu.roll(a, LANES - dist * SSM_GROUP, 1), b)
        cur = nxt
    return cur


def _pair_lanes(a, b):
    low = lax.broadcasted_iota(jnp.int32, a.shape, 1) < LANES // 2
    return (jnp.where(low, a, pltpu.roll(b, LANES // 2, 1)),
            jnp.where(low, pltpu.roll(a, LANES // 2, 1), b))


def _s5_chunk_kernel(u_ref, wa_ref, yi_ref, sfr_ref, sfi_ref, sbr_ref, sbi_ref, *, nchunk):
    halves = []
    for jh in range(S5_CHUNK // TOKENS_PER_HALF):
        toks = [u_ref[pl.ds(jh * TOKENS_PER_HALF + jj, nchunk, stride=S5_CHUNK), :]
                for jj in range(TOKENS_PER_HALF)]
        halves.append([o.astype(BF16) for o in _granule_transpose(toks)])
    kd = S5_CHUNK * SSM_GROUP
    for qq in range(PAIRS_PER_VREG):
        res = []
        for gl in range(S5_PAIR):
            g = S5_PAIR * qq + gl
            z = jnp.concatenate([halves[jh][g] for jh in range(len(halves))], axis=-1)
            r = jnp.dot(z, wa_ref[0, g], preferred_element_type=F32)
            yi_ref[0, g] = r[:, :kd]
            res.append(r)
        lo = qq * LANES
        sfr_ref[:, lo:lo + LANES], sfi_ref[:, lo:lo + LANES] = _pair_lanes(
            res[0][:, kd:kd + LANES], res[1][:, kd:kd + LANES])
        sbr_ref[:, lo:lo + LANES], sbi_ref[:, lo:lo + LANES] = _pair_lanes(
            res[0][:, kd + LANES:kd + 2 * LANES], res[1][:, kd + LANES:kd + 2 * LANES])


def _s5_chunk_call(u, wa, *, layer, batch, seq):
    t, d_ssm = u.shape
    nchunk = seq // S5_CHUNK
    nv = d_ssm // LANES
    _, ng, kd, ncol = wa.shape
    sw = PAIRS_PER_VREG * LANES
    st = jax.ShapeDtypeStruct((nchunk, batch * nv * sw), F32)
    sspec = pl.BlockSpec((nchunk, sw), lambda b, v: (0, b * nv + v))
    return pl.pallas_call(
        functools.partial(_s5_chunk_kernel, nchunk=nchunk),
        grid=(batch, nv),
        in_specs=[pl.BlockSpec((seq, LANES), lambda b, v: (b, v)),
                  pl.BlockSpec((1, GROUPS_PER_VREG, kd, ncol), lambda b, v: (layer, v, 0, 0))],
        out_specs=[pl.BlockSpec((1, GROUPS_PER_VREG, nchunk, kd), lambda b, v: (b, v, 0, 0)),
                   sspec, sspec, sspec, sspec],
        out_shape=[jax.ShapeDtypeStruct((batch, ng, nchunk, kd), F32), st, st, st, st],
        name="s5_chunk",
        compiler_params=_cparams(
            ("arbitrary", "arbitrary"),
            blocks=[((seq, LANES), F32), ((GROUPS_PER_VREG, kd, ncol), BF16),
                    ((1, GROUPS_PER_VREG, nchunk, kd), F32)] + [((nchunk, sw), F32)] * 4,
            temps=[((seq, LANES), F32)] * 3 + [((nchunk, ncol), F32)] * 4),
    )(u, wa)


def _s5scan_kernel(sfr_ref, sfi_ref, sbr_ref, sbi_ref, afr_ref, afi_ref, abr_ref, abi_ref,
                   xfr_ref, xfi_ref, xbr_ref, xbi_ref, *, nchunk):
    afr, afi = afr_ref[...], afi_ref[...]
    abr, abi = abr_ref[...], abi_ref[...]
    zero = jnp.zeros_like(afr)

    def body(k, carry):
        fr, fi, br, bi = carry
        kb = nchunk - 1 - k
        rowf = pl.ds(k, 1)
        rowb = pl.ds(kb, 1)
        xfr_ref[rowf, :] = fr
        xfi_ref[rowf, :] = fi
        xbr_ref[rowb, :] = br
        xbi_ref[rowb, :] = bi
        nfr = fr * afr - fi * afi + sfr_ref[rowf, :]
        nfi = fr * afi + fi * afr + sfi_ref[rowf, :]
        nbr = br * abr - bi * abi + sbr_ref[rowb, :]
        nbi = br * abi + bi * abr + sbi_ref[rowb, :]
        return nfr, nfi, nbr, nbi

    lax.fori_loop(0, nchunk, body, (zero, zero, zero, zero))


def _s5scan_call(s4, a4, *, nchunk, tl):
    w = s4[0].shape[1]
    sspec = pl.BlockSpec((nchunk, tl), lambda j: (0, j))
    aspec = pl.BlockSpec((1, tl), lambda j: (0, j))
    st = jax.ShapeDtypeStruct((nchunk, w), F32)
    return pl.pallas_call(
        functools.partial(_s5scan_kernel, nchunk=nchunk),
        grid=(w // tl,),
        in_specs=[sspec] * 4 + [aspec] * 4,
        out_specs=[sspec] * 4,
        out_shape=[st] * 4,
        name="s5_scan",
        compiler_params=_cparams(("arbitrary",),
                                 blocks=[((nchunk, tl), F32)] * 8 + [((1, tl), F32)] * 4,
                                 temps=[((SUBLANES_32BIT, tl), F32)] * 16),
    )(*s4, *a4)


def _s5_readout_kernel(yi_ref, xfr_ref, xfi_ref, xbr_ref, xbi_ref, wb_ref, y_ref, *, nchunk):
    nhalf = S5_CHUNK // TOKENS_PER_HALF
    pieces = [[None] * GROUPS_PER_VREG for _ in range(nhalf)]
    for qq in range(PAIRS_PER_VREG):
        lo = qq * LANES
        xf = _pair_lanes(xfr_ref[:, lo:lo + LANES], xfi_ref[:, lo:lo + LANES])
        xb = _pair_lanes(xbr_ref[:, lo:lo + LANES], xbi_ref[:, lo:lo + LANES])
        for gl in range(S5_PAIR):
            g = S5_PAIR * qq + gl
            xs = jnp.concatenate([xf[gl], xb[gl]], axis=-1)
            y = yi_ref[0, g] + jnp.dot(xs.astype(BF16), wb_ref[0, g], preferred_element_type=F32)
            for jh in range(nhalf):
                pieces[jh][g] = y[:, jh * LANES:(jh + 1) * LANES]
    for jh in range(nhalf):
        toks = _granule_transpose(pieces[jh])
        for jj in range(TOKENS_PER_HALF):
            y_ref[pl.ds(jh * TOKENS_PER_HALF + jj, nchunk, stride=S5_CHUNK), :] = toks[jj]


def _s5_readout_call(yi, x4, wb, *, layer, batch, seq):
    _, ng, nchunk, kd = yi.shape
    nrow = wb.shape[2]
    nv = ng // GROUPS_PER_VREG
    sw = PAIRS_PER_VREG * LANES
    xspec = pl.BlockSpec((nchunk, sw), lambda b, v: (0, b * nv + v))
    return pl.pallas_call(
        functools.partial(_s5_readout_kernel, nchunk=nchunk),
        grid=(batch, nv),
        in_specs=[pl.BlockSpec((1, GROUPS_PER_VREG, nchunk, kd), lambda b, v: (b, v, 0, 0)),
                  xspec, xspec, xspec, xspec,
                  pl.BlockSpec((1, GROUPS_PER_VREG, nrow, kd), lambda b, v: (layer, v, 0, 0))],
        out_specs=pl.BlockSpec((seq, LANES), lambda b, v: (b, v)),
        out_shape=jax.ShapeDtypeStruct((batch * seq, nv * LANES), F32),
        name="s5_readout",
        compiler_params=_cparams(
            ("arbitrary", "arbitrary"),
            blocks=[((1, GROUPS_PER_VREG, nchunk, kd), F32), ((GROUPS_PER_VREG, nrow, kd), BF16),
                    ((seq, LANES), F32)] + [((nchunk, sw), F32)] * 4,
            temps=[((seq, LANES), F32)] * 4),
    )(yi, *x4, wb)


def _s5_mixer(u, wa, wb, a_re, a_im, *, layer, batch, seq):
    nchunk = seq // S5_CHUNK
    yi, sfr, sfi, sbr, sbi = _s5_chunk_call(u, wa, layer=layer, batch=batch, seq=seq)
    w = a_re.shape[1]
    a4 = [jnp.tile(a.reshape(1, w), (1, batch)) for a in (a_re[0], a_im[0], a_re[1], a_im[1])]
    tl = min(2048, batch * w)
    x4 = _s5scan_call([sfr, sfi, sbr, sbi], a4, nchunk=nchunk, tl=tl)
    return _s5_readout_call(yi, x4, wb, layer=layer, batch=batch, seq=seq)


ROW_TILE = 8


def _post_kernel(y_ref, at_ref, gs_ref, ga_ref, x_ref, mod_ref, ln_ref,
                 wg_ref, bg_ref, wsu_ref, wau_ref, wo_ref, wr_ref,
                 x1_ref, h2_ref, aff_ref, afft_ref, *, d_model):
    mod = mod_ref[0]
    g1 = mod[:, 2 * d_model:3 * d_model]
    sh2 = mod[:, 3 * d_model:4 * d_model]
    sc2 = mod[:, 4 * d_model:5 * d_model]
    tm = y_ref.shape[0]

    au = jnp.dot(at_ref[...], wau_ref[...], preferred_element_type=F32)
    z = jax.nn.gelu(y_ref[...], approximate=True)
    glu = jnp.dot(z.astype(BF16), wg_ref[...], preferred_element_type=F32) + bg_ref[...]
    ssm = z * jax.nn.sigmoid(glu)
    su = jnp.dot(ssm.astype(BF16), wsu_ref[...], preferred_element_type=F32)
    merged = gs_ref[...].astype(F32) * su + ga_ref[...].astype(F32) * au
    mix = jnp.dot(merged.astype(BF16), wo_ref[...], preferred_element_type=F32)
    x1 = x_ref[...] + g1 * mix
    x1_ref[...] = x1
    ms = jnp.mean(x1 * x1, axis=-1, keepdims=True)
    h2 = (x1 * lax.rsqrt(ms + EPS) * ln_ref[...]) * (1.0 + sc2) + sh2
    for i in range(d_model // LANES):
        h2_ref[pl.ds(i, tm, stride=ROW_TILE), :] = h2[:, i * LANES:(i + 1) * LANES]
    lg = lax.dot_general(wr_ref[...], h2, (((1,), (1,)), ((), ())), preferred_element_type=F32,
                         precision=lax.Precision.HIGHEST)
    lg = lg - jnp.max(lg, axis=0, keepdims=True)
    ex = jnp.exp(lg)
    aff = ex / jnp.sum(ex, axis=0, keepdims=True)
    aff_ref[0] = aff
    pad = jnp.zeros((LANES - aff.shape[0], aff.shape[1]), F32)
    afft_ref[...] = jnp.concatenate([aff, pad], axis=0).T


def _post_call(y, attn, gs, ga, x2, mod3, ln_w, wg, bg, wsu, wau, wo, wr_t, *, seq, tm):
    t, d = x2.shape
    d_ssm = y.shape[1]
    d_attn = attn.shape[1]
    ne = wr_t.shape[0]
    tiles_per_seq = seq // tm
    row = lambda i: (i, 0)
    const = lambda i: (0, 0)
    return pl.pallas_call(
        functools.partial(_post_kernel, d_model=d),
        grid=(t // tm,),
        in_specs=[pl.BlockSpec((tm, d_ssm), row),
                  pl.BlockSpec((tm, d_attn), row),
                  pl.BlockSpec((tm, d), row),
                  pl.BlockSpec((tm, d), row),
                  pl.BlockSpec((tm, d), row),
                  pl.BlockSpec((1, 1, mod3.shape[2]), lambda i: (i // tiles_per_seq, 0, 0)),
                  pl.BlockSpec((1, d), const),
                  pl.BlockSpec(wg.shape, const),
                  pl.BlockSpec((1, d_ssm), const),
                  pl.BlockSpec(wsu.shape, const),
                  pl.BlockSpec(wau.shape, const),
                  pl.BlockSpec(wo.shape, const),
                  pl.BlockSpec(wr_t.shape, const)],
        out_specs=[pl.BlockSpec((tm, d), row),
                   pl.BlockSpec((tm * ROW_TILE, LANES), row),
                   pl.BlockSpec((1, ne, tm), lambda i: (i // tiles_per_seq, 0, i % tiles_per_seq)),
                   pl.BlockSpec((tm, LANES), row)],
        out_shape=[jax.ShapeDtypeStruct((t, d), F32),
                   jax.ShapeDtypeStruct((t * ROW_TILE, LANES), F32),
                   jax.ShapeDtypeStruct((t // seq, ne, seq), F32),
                   jax.ShapeDtypeStruct((t, LANES), F32)],
        name="post_mix",
        compiler_params=_cparams(
            ("arbitrary",),
            blocks=[((tm, d_ssm), F32), ((tm, d_attn), BF16), ((tm, d), BF16), ((tm, d), BF16),
                    ((tm, d), F32), ((1, 1, mod3.shape[2]), F32), ((1, d), F32), (wg.shape, BF16),
                    ((1, d_ssm), F32), (wsu.shape, BF16), (wau.shape, BF16), (wo.shape, BF16),
                    (wr_t.shape, F32), ((tm, d), F32), ((tm * ROW_TILE, LANES), F32),
                    ((1, ne, tm), F32), ((tm, LANES), F32)],
            temps=[((tm, d), F32)] * 6),
    )(y, attn, gs, ga, x2, mod3, ln_w, wg, bg, wsu, wau, wo, wr_t)


def _lane_cumsum(mask_f, tri):
    ne, seq = mask_f.shape
    run = jnp.zeros((ne, 1), F32)
    parts = []
    for j in range(seq // LANES):
        blk = mask_f[:, j * LANES:(j + 1) * LANES]
        cs = jnp.dot(blk.astype(BF16), tri, preferred_element_type=F32) + run
        parts.append(cs)
        run = run + jnp.sum(blk, axis=1, keepdims=True)
    return jnp.concatenate(parts, axis=-1)


def _select_kernel(aff_ref, tri_ref, idx_ref, half_ref, *, cap, seq, ts):
    aff = aff_ref[0]
    ne = aff.shape[0]
    tri = tri_ref[...]

    def body(i, thr):
        cand = thr | (jnp.int32(1) << (30 - i))
        cnt = jnp.sum(jnp.where(aff >= pltpu.bitcast(cand, F32), 1.0, 0.0), axis=1, keepdims=True)
        return jnp.where(cnt >= cap, cand, thr)

    thr = lax.fori_loop(0, 31, body, jnp.zeros((ne, 1), jnp.int32))
    gt = aff >= pltpu.bitcast(thr + 1, F32)
    eq = (aff >= pltpu.bitcast(thr, F32)) & jnp.logical_not(gt)
    need = cap - jnp.sum(jnp.where(gt, 1.0, 0.0), axis=1, keepdims=True)
    eq_f = jnp.where(eq, 1.0, 0.0)
    eq_rank = _lane_cumsum(eq_f, tri) - eq_f
    sel = gt | (eq & (eq_rank < need))
    sel_f = jnp.where(sel, 1.0, 0.0)
    csum = _lane_cumsum(sel_f, tri)
    rank = jnp.where(sel, csum - sel_f, -1.0)
    half_ref[0] = csum[:, seq // 2 - 1:seq // 2].astype(jnp.int32)

    slot = lax.broadcasted_iota(jnp.int32, (cap, ts), 0).astype(F32)
    lane_e = lax.broadcasted_iota(jnp.int32, (cap, ne), 1)
    idx = jnp.zeros((cap, ne), F32)
    for e in range(ne):
        col = jnp.zeros((cap, 1), F32)
        for j in range(seq // ts):
            tok = (lax.broadcasted_iota(jnp.int32, (1, ts), 1) + j * ts).astype(F32)
            hit = rank[e:e + 1, j * ts:(j + 1) * ts] == slot
            col = col + jnp.sum(jnp.where(hit, tok, 0.0), axis=1, keepdims=True)
        idx = jnp.where(lane_e == e, col, idx)
    idx_ref[0] = idx.astype(jnp.int32)


def _select_call(aff_t, tri, *, cap):
    b, ne, seq = aff_t.shape
    ts = min(1024, seq)
    return pl.pallas_call(
        functools.partial(_select_kernel, cap=cap, seq=seq, ts=ts),
        grid=(b,),
        in_specs=[pl.BlockSpec((1, ne, seq), lambda i: (i, 0, 0)),
                  pl.BlockSpec((LANES, LANES), lambda i: (0, 0))],
        out_specs=[pl.BlockSpec((1, cap, ne), lambda i: (i, 0, 0)),
                   pl.BlockSpec((1, ne, 1), lambda i: (i, 0, 0))],
        out_shape=[jax.ShapeDtypeStruct((b, cap, ne), jnp.int32),
                   jax.ShapeDtypeStruct((b, ne, 1), jnp.int32)],
        name="expert_select",
        compiler_params=_cparams(
            ("arbitrary",),
            blocks=[((1, ne, seq), F32), ((LANES, LANES), BF16), ((1, cap, ne), jnp.int32),
                    ((1, ne, 1), jnp.int32)],
            temps=[((ne, seq), F32)] * 8 + [((cap, ts), F32)] * 4),
    )(aff_t, tri)


GATHER_UNROLL = 8


def _gather_kernel(idx_ref, h_ref, aff_ref, xg_ref, gc_ref, rows_ref, arow_ref, *, cap, ne):
    b = pl.program_id(0)
    e = pl.program_id(1)
    base = (b * ne + e) * cap

    def body(g, carry):
        for k in range(GATHER_UNROLL):
            c = g * GATHER_UNROLL + k
            r = idx_ref[base + c]
            src = pl.ds(pl.multiple_of(r * ROW_TILE, ROW_TILE), ROW_TILE)
            dst = pl.ds(pl.multiple_of(c * ROW_TILE, ROW_TILE), ROW_TILE)
            rows_ref[dst, :] = h_ref[src, :]
            arow_ref[pl.ds(c, 1), :] = aff_ref[pl.ds(r, 1), :]
        return carry

    lax.fori_loop(0, cap // GATHER_UNROLL, body, 0)
    xg_ref[0] = jnp.concatenate([rows_ref[pl.ds(i, cap, stride=ROW_TILE), :] for i in range(ROW_TILE)],
                                axis=1).astype(xg_ref.dtype)
    lane = lax.broadcasted_iota(jnp.int32, arow_ref.shape, 1)
    gc_ref[0] = jnp.sum(jnp.where(lane == e, arow_ref[...], 0.0), axis=1, keepdims=True)


def _gather_call(idx_flat, h2t, aff_tm, *, batch, seq, cap, ne):
    d = ROW_TILE * LANES
    assert h2t.shape == (batch * seq * ROW_TILE, LANES)
    grid_spec = pltpu.PrefetchScalarGridSpec(
        num_scalar_prefetch=1,
        grid=(batch, ne),
        in_specs=[pl.BlockSpec((seq * ROW_TILE, LANES), lambda b, e, idx: (b, 0)),
                  pl.BlockSpec((seq, LANES), lambda b, e, idx: (b, 0))],
        out_specs=[pl.BlockSpec((1, cap, d), lambda b, e, idx: (e, b, 0)),
                   pl.BlockSpec((1, cap, 1), lambda b, e, idx: (e, b, 0))],
        scratch_shapes=[pltpu.VMEM((cap * ROW_TILE, LANES), F32), pltpu.VMEM((cap, LANES), F32)],
    )
    return pl.pallas_call(
        functools.partial(_gather_kernel, cap=cap, ne=ne),
        grid_spec=grid_spec,
        out_shape=[jax.ShapeDtypeStruct((ne, batch * cap, d), BF16),
                   jax.ShapeDtypeStruct((ne, batch * cap, 1), F32)],
        name="expert_gather",
        compiler_params=_cparams(
            ("arbitrary", "arbitrary"),
            blocks=[((seq * ROW_TILE, LANES), F32), ((seq, LANES), F32), ((1, cap, d), BF16),
                    ((1, cap, 1), F32)],
            scratch=[((cap * ROW_TILE, LANES), F32), ((cap, LANES), F32)],
            temps=[((cap, d), F32)] * 2),
    )(idx_flat, h2t, aff_tm)


def _ffn_kernel(x_ref, gc_ref, wg_ref, wu_ref, wd_ref, ye_ref, *, rows):
    f = pl.program_id(1)

    @pl.when(f == 0)
    def _init():
        ye_ref[...] = jnp.zeros_like(ye_ref)

    wg = wg_ref[0, 0].astype(BF16)
    wu = wu_ref[0, 0].astype(BF16)
    wd = wd_ref[0, 0].astype(BF16)
    nblk = x_ref.shape[1] // rows

    def up(r):
        x = x_ref[0, r * rows:(r + 1) * rows, :]
        return (jnp.dot(x, wg, preferred_element_type=F32), jnp.dot(x, wu, preferred_element_type=F32))

    nxt = up(0)
    for r in range(nblk):
        hg, hu = nxt
        if r + 1 < nblk:
            nxt = up(r + 1)
        hid = (hg * jax.nn.sigmoid(hg) * hu).astype(BF16)
        ye_ref[0, r * rows:(r + 1) * rows, :] += jnp.dot(hid, wd, preferred_element_type=F32)

    @pl.when(f == pl.num_programs(1) - 1)
    def _gate():
        ye_ref[0] = ye_ref[0] * gc_ref[0]


def _ffn_call(xg, gc, wg, wu, wd, *, layer, tf, rows):
    ne, n, d = xg.shape
    dff = wg.shape[3]
    return pl.pallas_call(
        functools.partial(_ffn_kernel, rows=rows),
        grid=(ne, dff // tf),
        in_specs=[pl.BlockSpec((1, n, d), lambda e, f: (e, 0, 0)),
                  pl.BlockSpec((1, n, 1), lambda e, f: (e, 0, 0)),
                  pl.BlockSpec((1, 1, d, tf), lambda e, f: (layer, e, 0, f)),
                  pl.BlockSpec((1, 1, d, tf), lambda e, f: (layer, e, 0, f)),
                  pl.BlockSpec((1, 1, tf, d), lambda e, f: (layer, e, f, 0))],
        out_specs=pl.BlockSpec((1, n, d), lambda e, f: (e, 0, 0)),
        out_shape=jax.ShapeDtypeStruct((ne, n, d), F32),
        name="expert_ffn",
        compiler_params=_cparams(
            ("arbitrary", "arbitrary"),
            blocks=[((1, n, d), BF16), ((1, n, 1), F32), ((1, 1, d, tf), F32), ((1, 1, d, tf), F32),
                    ((1, 1, tf, d), F32), ((1, n, d), F32)],
            temps=[((d, tf), BF16)] * 3 + [((rows, tf), F32)] * 5 + [((rows, d), F32)]),
    )(xg, gc, wg, wu, wd)


SCATTER_UNROLL = 8
SEQ_HALVES = 2


def _combine_kernel(idx_ref, half_ref, ye_ref, x1_ref, mod_ref, o_ref, acc_ref, *, cap, ne, rows,
                    d_model):
    i = pl.program_id(0)
    e = pl.program_id(1)
    b = i // SEQ_HALVES
    hh = i % SEQ_HALVES

    @pl.when(e == 0)
    def _init():
        o_ref[...] = jnp.zeros_like(o_ref)
        acc_ref[...] = jnp.zeros_like(acc_ref)

    mid = half_ref[b * ne + e]
    lo = jnp.where(hh == 0, 0, mid)
    hi = jnp.where(hh == 0, mid, cap)
    base = (b * ne + e) * cap
    off = hh * rows

    def add_row(c, dst_ref):
        r = idx_ref[base + c] - off
        dst_ref[pl.ds(r, 1), :] += ye_ref[0, pl.ds(c, 1), :]

    def group(g, carry):
        for k in range(SCATTER_UNROLL):
            add_row(lo + g * SCATTER_UNROLL + k, o_ref if k % 2 == 0 else acc_ref)
        return carry

    ngroup = (hi - lo) // SCATTER_UNROLL
    lax.fori_loop(0, ngroup, group, 0)

    def tail(c, carry):
        add_row(c, o_ref)
        return carry

    lax.fori_loop(lo + ngroup * SCATTER_UNROLL, hi, tail, 0)

    @pl.when(e == pl.num_programs(1) - 1)
    def _fin():
        g2 = mod_ref[0][:, 5 * d_model:6 * d_model]
        o_ref[...] = x1_ref[...] + g2 * (o_ref[...] + acc_ref[...])


def _combine_call(idx_flat, half_flat, ye, x1, mod3, *, batch, seq, cap):
    t, d = x1.shape
    ne = ye.shape[0]
    rows = seq // SEQ_HALVES
    grid_spec = pltpu.PrefetchScalarGridSpec(
        num_scalar_prefetch=2,
        grid=(batch * SEQ_HALVES, ne),
        in_specs=[pl.BlockSpec((1, cap, d), lambda i, e, idx, half: (e, i // SEQ_HALVES, 0)),
                  pl.BlockSpec((rows, d), lambda i, e, idx, half: (i, 0)),
                  pl.BlockSpec((1, 1, mod3.shape[2]), lambda i, e, idx, half: (i // SEQ_HALVES, 0, 0))],
        out_specs=pl.BlockSpec((rows, d), lambda i, e, idx, half: (i, 0)),
        scratch_shapes=[pltpu.VMEM((rows, d), F32)],
    )
    return pl.pallas_call(
        functools.partial(_combine_kernel, cap=cap, ne=ne, rows=rows, d_model=d),
        grid_spec=grid_spec,
        out_shape=jax.ShapeDtypeStruct((t, d), F32),
        name="moe_combine",
        compiler_params=_cparams(
            ("arbitrary", "arbitrary"),
            blocks=[((1, cap, d), F32), ((rows, d), F32), ((1, 1, mod3.shape[2]), F32), ((rows, d), F32)],
            scratch=[((rows, d), F32)],
            temps=[((SUBLANES_32BIT * SCATTER_UNROLL, d), F32)] * 4),
    )(idx_flat, half_flat, ye, x1, mod3)


def kernel(x, c, w_mod, b_mod, ln1_w, ln2_w, w_in, ssm_lam_re, ssm_lam_im, ssm_log_dt, ssm_b_re,
           ssm_b_im, ssm_c_re, ssm_c_im, ssm_d, w_glu, b_glu, q_norm_w, k_norm_w, w_ssm_up,
           w_attn_up, w_out, w_router, w_exp_gate, w_exp_up, w_exp_down):
    batch, seq, d = x.shape
    depth = w_mod.shape[0]
    t = batch * seq
    d_attn = N_Q_HEADS * HEAD_DIM
    d_kv = N_KV_HEADS * HEAD_DIM
    cap = CAPACITY_FACTOR * seq // N_EXPERTS
    tm = min(512, seq)
    tq = min(256, seq)

    tabs = _rope_tables(seq)
    head_id = jnp.arange(d_attn) // HEAD_DIM
    hm = (head_id[:, None] == head_id[None, :]).astype(BF16)
    tri = (jnp.arange(LANES)[:, None] <= jnp.arange(LANES)[None, :]).astype(BF16)
    c_pad = jnp.zeros((8, d), F32).at[:batch].set(c)
    mod_all = _mod_call(c_pad, w_mod, b_mod)[:, :batch]
    wa_all, wb_all, a_re_all, a_im_all = _s5_weights(ssm_lam_re, ssm_lam_im, ssm_log_dt, ssm_b_re,
                                                     ssm_b_im, ssm_c_re, ssm_c_im, ssm_d)

    x2 = x.reshape(t, d)
    for l in range(depth):
        mod3 = mod_all[l].reshape(batch, 1, 6 * d)
        qw = jnp.tile(q_norm_w[l], N_Q_HEADS).reshape(1, d_attn)
        kw = jnp.tile(k_norm_w[l], N_KV_HEADS).reshape(1, d_kv)
        u, q, k, v, gs, ga = _in_call(x2, mod3, ln1_w[l].reshape(1, d), w_in[l].astype(BF16), tabs,
                                      qw, kw, hm, seq=seq, tm=tm)
        y = _s5_mixer(u, wa_all, wb_all, a_re_all[l], a_im_all[l], layer=l, batch=batch, seq=seq)
        attn = _attn_call(q, k, v, seq=seq, tq=tq)
        x1, h2, aff_t, aff_tm = _post_call(
            y, attn, gs, ga, x2, mod3, ln2_w[l].reshape(1, d), w_glu[l].astype(BF16),
            b_glu[l].reshape(1, -1), w_ssm_up[l].astype(BF16), w_attn_up[l].astype(BF16),
            w_out[l].astype(BF16), w_router[l].T, seq=seq, tm=tm)
        idx_t, half = _select_call(aff_t, tri, cap=cap)
        idx_flat = jnp.swapaxes(idx_t, 1, 2).reshape(-1)
        half_flat = half.reshape(-1)
        xg, gc = _gather_call(idx_flat, h2, aff_tm, batch=batch, seq=seq, cap=cap, ne=N_EXPERTS)
        ye = _ffn_call(xg, gc, w_exp_gate, w_exp_up, w_exp_down, layer=l, tf=512,
                       rows=min(1024, batch * cap))
        x2 = _combine_call(idx_flat, half_flat, ye, x1, mod3, batch=batch, seq=seq, cap=cap)
    return x2.reshape(batch, seq, d)
```

```python
import functools
import math

import jax
import jax.numpy as jnp
from jax import lax
from jax.experimental import pallas as pl
from jax.experimental.pallas import tpu as pltpu

F32 = jnp.float32
BF16 = jnp.bfloat16

HEAD_DIM = 64
N_Q_HEADS = 8
N_KV_HEADS = 2
Q_PER_KV = N_Q_HEADS // N_KV_HEADS
ROT_HALF = HEAD_DIM // 2
ROPE_THETA = 10000.0
GRID_W = 64
SSM_GROUP = 16
SSM_STATE = 64
N_EXPERTS = 16
CAPACITY_FACTOR = 2
EPS = 1e-6

S5_CHUNK = 16
S5_PAIR = 2
LANES = 128
V7X_VMEM_BYTES = 64 * 1024 * 1024
LOG2E = 1.4426950408889634


PIPELINE_BUFFERS = 2
SUBLANES_32BIT = 8


def _vmem_bytes(shape, dtype):
    itemsize = jnp.dtype(dtype).itemsize
    sub = SUBLANES_32BIT * 4 // itemsize
    lead = math.prod(shape[:-2]) if len(shape) > 2 else 1
    rows = shape[-2] if len(shape) > 1 else 1
    return lead * (-(-rows // sub) * sub) * (-(-shape[-1] // LANES) * LANES) * itemsize


def _cparams(semantics, blocks, scratch=(), temps=()):
    need = (PIPELINE_BUFFERS * sum(_vmem_bytes(s, d) for s, d in blocks)
            + sum(_vmem_bytes(s, d) for s, d in scratch) + sum(_vmem_bytes(s, d) for s, d in temps))
    return pltpu.CompilerParams(dimension_semantics=semantics,
                                vmem_limit_bytes=min(need, V7X_VMEM_BYTES))


def _mod_kernel(c_ref, w_ref, b_ref, o_ref):
    c = c_ref[...]
    ca = c * jax.nn.sigmoid(c)
    o_ref[0] = jnp.dot(ca, w_ref[0], preferred_element_type=F32,
                       precision=lax.Precision.HIGHEST) + b_ref[0]


def _mod_call(c_pad, w_mod, b_mod):
    depth, d, n = w_mod.shape
    rows = c_pad.shape[0]
    tn = 1536
    return pl.pallas_call(
        _mod_kernel,
        grid=(depth, n // tn),
        in_specs=[pl.BlockSpec((rows, d), lambda l, j: (0, 0)),
                  pl.BlockSpec((1, d, tn), lambda l, j: (l, 0, j)),
                  pl.BlockSpec((1, 1, tn), lambda l, j: (l, 0, j))],
        out_specs=pl.BlockSpec((1, rows, tn), lambda l, j: (l, 0, j)),
        out_shape=jax.ShapeDtypeStruct((depth, rows, n), F32),
        name="adaln_mod",
        compiler_params=_cparams(
            ("arbitrary", "arbitrary"),
            blocks=[((rows, d), F32), ((1, d, tn), F32), ((1, 1, tn), F32), ((1, rows, tn), F32)],
            temps=[((d, tn), F32)] * 2),
    )(c_pad, w_mod, b_mod.reshape(depth, 1, n))


def _head_mean_square(xsq, hm):
    hi = xsq.astype(BF16)
    lo = (xsq - hi.astype(F32)).astype(BF16)
    s = jnp.dot(hi, hm, preferred_element_type=F32) + jnp.dot(lo, hm, preferred_element_type=F32)
    return s * (1.0 / HEAD_DIM)


def _rope_block(blk, cos, sa, sb):
    return (blk * cos + pltpu.roll(blk, LANES - ROT_HALF // 2, 1) * sa
            + pltpu.roll(blk, ROT_HALF // 2, 1) * sb)


def _in_kernel(x_ref, mod_ref, ln_ref, w_ref, cos_ref, sa_ref, sb_ref, qw_ref, kw_ref, hm_ref,
               u_ref, q_ref, k_ref, v_ref, gs_ref, ga_ref, *, d_model, d_ssm, d_attn, d_kv):
    x = x_ref[...]
    mod = mod_ref[0]
    sh = mod[:, 0:d_model]
    sc = mod[:, d_model:2 * d_model]
    ms = jnp.mean(x * x, axis=-1, keepdims=True)
    h = (x * lax.rsqrt(ms + EPS) * ln_ref[...]) * (1.0 + sc) + sh
    hb = h.astype(BF16)
    hm = hm_ref[...]
    cos = cos_ref[...]
    sa = sa_ref[...]
    sb = sb_ref[...]
    qscale = (HEAD_DIM ** -0.5) * LOG2E

    def finish_u(p):
        u_ref[...] = p.astype(u_ref.dtype)

    def finish_q(q):
        qn = q * lax.rsqrt(_head_mean_square(q * q, hm) + EPS) * qw_ref[...]
        qr = [_rope_block(qn[:, j * LANES:(j + 1) * LANES], cos, sa, sb) for j in range(d_attn // LANES)]
        q_ref[...] = (jnp.concatenate(qr, axis=-1) * qscale).astype(q_ref.dtype)

    def finish_kv(p):
        k = p[:, :d_kv]
        v_ref[0] = p[:, d_kv:].T.astype(v_ref.dtype)
        kn = k * lax.rsqrt(_head_mean_square(k * k, hm[:d_kv, :d_kv]) + EPS) * kw_ref[...]
        kr = [_rope_block(kn[:, j * LANES:(j + 1) * LANES], cos, sa, sb) for j in range(d_kv // LANES)]
        k_ref[...] = jnp.concatenate(kr, axis=-1).astype(k_ref.dtype)

    def finish_gs(p):
        gs_ref[...] = jax.nn.sigmoid(p).astype(gs_ref.dtype)

    def finish_ga(p):
        ga_ref[...] = jax.nn.sigmoid(p).astype(ga_ref.dtype)

    groups = ((d_ssm, finish_u), (d_attn, finish_q), (2 * d_kv, finish_kv),
              (d_model, finish_gs), (d_model, finish_ga))
    starts = [sum(g[0] for g in groups[:i]) for i in range(len(groups))]

    def project(i):
        return jnp.dot(hb, w_ref[:, starts[i]:starts[i] + groups[i][0]], preferred_element_type=F32)

    nxt = project(0)
    for i, (_, finish) in enumerate(groups):
        cur = nxt
        if i + 1 < len(groups):
            nxt = project(i + 1)
        finish(cur)


def _in_call(x2, mod3, ln_w, w_in, tabs, qw, kw, hm, *, seq, tm):
    t, d = x2.shape
    n_in = w_in.shape[1]
    d_ssm = d // 2
    d_attn = N_Q_HEADS * HEAD_DIM
    d_kv = N_KV_HEADS * HEAD_DIM
    tiles_per_seq = seq // tm
    cos, sa, sb = tabs
    kern = functools.partial(_in_kernel, d_model=d, d_ssm=d_ssm, d_attn=d_attn, d_kv=d_kv)
    row = lambda i: (i, 0)
    const = lambda i: (0, 0)
    tab = lambda i: (i % tiles_per_seq, 0)
    return pl.pallas_call(
        kern,
        grid=(t // tm,),
        in_specs=[pl.BlockSpec((tm, d), row),
                  pl.BlockSpec((1, 1, mod3.shape[2]), lambda i: (i // tiles_per_seq, 0, 0)),
                  pl.BlockSpec((1, d), const),
                  pl.BlockSpec((d, n_in), const),
                  pl.BlockSpec((tm, LANES), tab),
                  pl.BlockSpec((tm, LANES), tab),
                  pl.BlockSpec((tm, LANES), tab),
                  pl.BlockSpec((1, d_attn), const),
                  pl.BlockSpec((1, d_kv), const),
                  pl.BlockSpec((d_attn, d_attn), const)],
        out_specs=[pl.BlockSpec((tm, d_ssm), row),
                   pl.BlockSpec((tm, d_attn), row),
                   pl.BlockSpec((tm, d_kv), row),
                   pl.BlockSpec((1, d_kv, tm), lambda i: (i // tiles_per_seq, 0, i % tiles_per_seq)),
                   pl.BlockSpec((tm, d), row),
                   pl.BlockSpec((tm, d), row)],
        out_shape=[jax.ShapeDtypeStruct((t, d_ssm), F32),
                   jax.ShapeDtypeStruct((t, d_attn), BF16),
                   jax.ShapeDtypeStruct((t, d_kv), BF16),
                   jax.ShapeDtypeStruct((t // seq, d_kv, seq), BF16),
                   jax.ShapeDtypeStruct((t, d), BF16),
                   jax.ShapeDtypeStruct((t, d), BF16)],
        name="in_proj",
        compiler_params=_cparams(
            ("arbitrary",),
            blocks=[((tm, d), F32), ((1, 1, mod3.shape[2]), F32), ((1, d), F32), ((d, n_in), BF16),
                    ((tm, LANES), F32), ((tm, LANES), F32), ((tm, LANES), F32),
                    ((1, d_attn), F32), ((1, d_kv), F32), ((d_attn, d_attn), BF16),
                    ((tm, d_ssm), F32), ((tm, d_attn), BF16), ((tm, d_kv), BF16), ((1, d_kv, tm), BF16),
                    ((tm, d), BF16), ((tm, d), BF16)],
            temps=[((tm, d), F32)] * 6),
    )(x2, mod3, ln_w, w_in, cos, sa, sb, qw, kw, hm)


def _rope_tables(seq):
    pos = jnp.arange(seq)
    row = (pos // GRID_W).astype(F32)
    col = (pos % GRID_W).astype(F32)
    inv = 1.0 / (ROPE_THETA ** (jnp.arange(0, ROT_HALF, 2, dtype=F32) / ROT_HALF))
    ang_r = row[:, None] * inv[None, :]
    ang_c = col[:, None] * inv[None, :]
    zeros = jnp.zeros_like(ang_r)
    cos_h = jnp.concatenate([jnp.cos(ang_r), jnp.cos(ang_r), jnp.cos(ang_c), jnp.cos(ang_c)], axis=-1)
    sa_h = jnp.concatenate([-jnp.sin(ang_r), zeros, -jnp.sin(ang_c), zeros], axis=-1)
    sb_h = jnp.concatenate([zeros, jnp.sin(ang_r), zeros, jnp.sin(ang_c)], axis=-1)
    rep = LANES // HEAD_DIM
    return tuple(jnp.tile(a, (1, rep)) for a in (cos_h, sa_h, sb_h))


def _attn_heads_out(o_t, tq):
    return [o_t[:, g * tq:(g + 1) * tq].T for g in range(Q_PER_KV)]


def _attn_kernel(q_ref, k_ref, vt_ref, o_ref, *, kb):
    tq = q_ref.shape[0]
    seq = k_ref.shape[0]
    nblk = seq // kb
    width = Q_PER_KV * HEAD_DIM
    q_ts = []
    for kvh in range(N_KV_HEADS):
        qt = q_ref[:, kvh * width:(kvh + 1) * width].astype(F32).T.astype(BF16)
        q_ts.append(jnp.concatenate([qt[g * HEAD_DIM:(g + 1) * HEAD_DIM] for g in range(Q_PER_KV)],
                                    axis=1))

    def scores(kvh, j):
        lo = kvh * HEAD_DIM
        return jnp.dot(k_ref[j * kb:(j + 1) * kb, lo:lo + HEAD_DIM], q_ts[kvh],
                       preferred_element_type=F32)

    def weighted_v(kvh, j, p):
        lo = kvh * HEAD_DIM
        return jnp.dot(vt_ref[0, lo:lo + HEAD_DIM, j * kb:(j + 1) * kb], p.astype(BF16),
                       preferred_element_type=F32)

    outs = []
    worst = jnp.zeros((1, 1), F32)
    for kvh in range(N_KV_HEADS):
        s0 = scores(kvh, 0)
        m0 = jnp.max(s0, axis=0, keepdims=True)
        l = jnp.zeros_like(m0)
        acc = jnp.zeros((HEAD_DIM, m0.shape[1]), F32)
        s_cur = s0
        for j in range(nblk):
            s_next = scores(kvh, j + 1) if j + 1 < nblk else None
            p = jnp.exp2(s_cur - m0)
            l = l + jnp.sum(p, axis=0, keepdims=True)
            acc = acc + weighted_v(kvh, j, p)
            s_cur = s_next
        worst = jnp.maximum(worst, jnp.max(l, axis=1, keepdims=True))
        outs.extend(_attn_heads_out(acc / l, tq))
    o_ref[...] = jnp.concatenate(outs, axis=-1).astype(o_ref.dtype)

    @pl.when(worst[0, 0] > jnp.finfo(F32).max)
    def _exact():
        outs = []
        for kvh in range(N_KV_HEADS):
            m = jnp.full((1, Q_PER_KV * tq), -jnp.inf, F32)
            l = jnp.zeros_like(m)
            acc = jnp.zeros((HEAD_DIM, Q_PER_KV * tq), F32)
            for j in range(nblk):
                s = scores(kvh, j)
                m_new = jnp.maximum(m, jnp.max(s, axis=0, keepdims=True))
                alpha = jnp.exp2(m - m_new)
                p = jnp.exp2(s - m_new)
                l = l * alpha + jnp.sum(p, axis=0, keepdims=True)
                acc = acc * alpha + weighted_v(kvh, j, p)
                m = m_new
            outs.extend(_attn_heads_out(acc / l, tq))
        o_ref[...] = jnp.concatenate(outs, axis=-1).astype(o_ref.dtype)


def _attn_call(q, k, vt, *, seq, tq):
    t, d_attn = q.shape
    d_kv = k.shape[1]
    nq = seq // tq
    kb = min(256, seq)
    return pl.pallas_call(
        functools.partial(_attn_kernel, kb=kb),
        grid=(t // seq, nq),
        in_specs=[pl.BlockSpec((tq, d_attn), lambda b, i: (b * nq + i, 0)),
                  pl.BlockSpec((seq, d_kv), lambda b, i: (b, 0)),
                  pl.BlockSpec((1, d_kv, seq), lambda b, i: (b, 0, 0))],
        out_specs=pl.BlockSpec((tq, d_attn), lambda b, i: (b * nq + i, 0)),
        out_shape=jax.ShapeDtypeStruct((t, d_attn), BF16),
        name="attention",
        compiler_params=_cparams(
            ("arbitrary", "arbitrary"),
            blocks=[((tq, d_attn), BF16), ((seq, d_kv), BF16), ((1, d_kv, seq), BF16), ((tq, d_attn), BF16)],
            temps=[((kb, Q_PER_KV * tq), F32)] * 2 * (seq // kb + 4)),
    )(q, k, vt)


def _s5_weights(lam_re, lam_im, log_dt, b_re, b_im, c_re, c_im, d_skip):
    hp = lax.Precision.HIGHEST
    nl, _, ng, p = lam_re.shape
    cg = SSM_GROUP
    L = S5_CHUNK
    kd = L * cg
    dt = jnp.exp(log_dt)[..., None]
    zr = lam_re * dt
    zi = lam_im * dt

    def cpow(e):
        mag = jnp.exp(zr[..., None] * e)
        return mag * jnp.cos(zi[..., None] * e), mag * jnp.sin(zi[..., None] * e)

    lb_re, lb_im = jnp.exp(zr) * jnp.cos(zi), jnp.exp(zr) * jnp.sin(zi)
    nr, ni = lb_re - 1.0, lb_im
    den = lam_re * lam_re + lam_im * lam_im
    f_re = ((nr * lam_re + ni * lam_im) / den)[..., None]
    f_im = ((ni * lam_re - nr * lam_im) / den)[..., None]
    bb_re = f_re * b_re - f_im * b_im
    bb_im = f_re * b_im + f_im * b_re

    m_idx = jnp.arange(L + 1, dtype=F32)
    expo = jnp.stack([m_idx, L - m_idx])[None, :, None, None, :]
    pw_re, pw_im = cpow(expo)
    rep = jnp.repeat(jnp.eye(L + 1, dtype=F32), cg, axis=1)
    til = jnp.tile(jnp.eye(cg, dtype=F32), (1, L + 1))
    pwx_re = jnp.einsum('dxgpm,mn->dxgpn', pw_re, rep, precision=hp)
    pwx_im = jnp.einsum('dxgpm,mn->dxgpn', pw_im, rep, precision=hp)
    ct_re = jnp.einsum('dxgcp,cn->dxgpn', c_re, til, precision=hp)
    ct_im = jnp.einsum('dxgcp,cn->dxgpn', c_im, til, precision=hp)
    ca_re = ct_re * pwx_re - ct_im * pwx_im
    ca_im = ct_re * pwx_im + ct_im * pwx_re
    ef_re, ef_im = ca_re[:, 0, ..., :kd], ca_im[:, 0, ..., :kd]
    w2f_re, w2f_im = ca_re[:, 0, ..., cg:], ca_im[:, 0, ..., cg:]
    w2b_re, w2b_im = ca_re[:, 1, ..., :kd], ca_im[:, 1, ..., :kd]
    eb_re, eb_im = ca_re[:, 1, ..., cg:], ca_im[:, 1, ..., cg:]

    def lag_rows(x, e_re, e_im):
        return (jnp.einsum('dgpe,dgpn->dgen', bb_re[:, x], e_re, precision=hp)
                - jnp.einsum('dgpe,dgpn->dgen', bb_im[:, x], e_im, precision=hp))

    kf = lag_rows(0, ef_re, ef_im)
    kb = lag_rows(1, eb_re, eb_im)
    zeros = jnp.zeros_like(kf)
    pf = jnp.concatenate([zeros, kf], axis=-1)
    pb = jnp.concatenate([kb, zeros], axis=-1)
    tf = jnp.stack([pf[..., kd - cg * s:2 * kd - cg * s] for s in range(L)], axis=2)
    tb = jnp.stack([pb[..., cg * (L - 1 - s):cg * (L - 1 - s) + kd] for s in range(L)], axis=2)
    skip = jnp.eye(kd, dtype=F32) * jnp.tile(d_skip.reshape(nl, ng, 1, cg), (1, 1, 1, L))
    tmat = (tf + tb).reshape(nl, ng, kd, kd) + skip

    s_idx = jnp.arange(L, dtype=F32)
    sexp = jnp.stack([L - 1 - s_idx, s_idx])[None, :, None, None, :]
    ps_re, ps_im = cpow(sexp)
    w1_re = (jnp.einsum('dxgps,dxgpe->dxgsep', ps_re, bb_re)
             - jnp.einsum('dxgps,dxgpe->dxgsep', ps_im, bb_im)).reshape(nl, 2, ng, kd, p)
    w1_im = (jnp.einsum('dxgps,dxgpe->dxgsep', ps_re, bb_im)
             + jnp.einsum('dxgps,dxgpe->dxgsep', ps_im, bb_re)).reshape(nl, 2, ng, kd, p)

    wa = jnp.concatenate([tmat, w1_re[:, 0], w1_im[:, 0], w1_re[:, 1], w1_im[:, 1]], axis=-1)
    wb = jnp.concatenate([w2f_re, -w2f_im, w2b_re, -w2b_im], axis=2)
    al_re, al_im = cpow(jnp.float32(L))
    a_re = al_re.reshape(nl, 2, ng * p)
    a_im = al_im.reshape(nl, 2, ng * p)
    return wa.astype(BF16), wb.astype(BF16), a_re, a_im


GROUPS_PER_VREG = LANES // SSM_GROUP
PAIRS_PER_VREG = GROUPS_PER_VREG // S5_PAIR
TOKENS_PER_HALF = LANES // SSM_GROUP


def _granule_transpose(arrs):
    gran = lax.broadcasted_iota(jnp.int32, arrs[0].shape, 1) // SSM_GROUP
    cur = list(arrs)
    for dist in (4, 2, 1):
        keep = (gran & dist) == 0
        nxt = list(cur)
        for i in range(len(cur)):
            if i & dist:
                continue
            a, b = cur[i], cur[i + dist]
            nxt[i] = jnp.where(keep, a, pltpu.roll(b, dist * SSM_GROUP, 1))
            nxt[i + dist] = jnp.where(keep, pltpu.roll(a, LANES - dist * SSM_GROUP, 1), b)
        cur = nxt
    return cur


def _pair_lanes(a, b):
    low = lax.broadcasted_iota(jnp.int32, a.shape, 1) < LANES // 2
    return (jnp.where(low, a, pltpu.roll(b, LANES // 2, 1)),
            jnp.where(low, pltpu.roll(a, LANES // 2, 1), b))


def _s5_chunk_kernel(u_ref, wa_ref, yi_ref, sfr_ref, sfi_ref, sbr_ref, sbi_ref, *, nchunk):
    halves = []
    for jh in range(S5_CHUNK // TOKENS_PER_HALF):
        toks = [u_ref[pl.ds(jh * TOKENS_PER_HALF + jj, nchunk, stride=S5_CHUNK), :]
                for jj in range(TOKENS_PER_HALF)]
        halves.append([o.astype(BF16) for o in _granule_transpose(toks)])
    kd = S5_CHUNK * SSM_GROUP
    for qq in range(PAIRS_PER_VREG):
        res = []
        for gl in range(S5_PAIR):
            g = S5_PAIR * qq + gl
            z = jnp.concatenate([halves[jh][g] for jh in range(len(halves))], axis=-1)
            r = jnp.dot(z, wa_ref[0, g], preferred_element_type=F32)
            yi_ref[0, g] = r[:, :kd]
            res.append(r)
        lo = qq * LANES
        sfr_ref[:, lo:lo + LANES], sfi_ref[:, lo:lo + LANES] = _pair_lanes(
            res[0][:, kd:kd + LANES], res[1][:, kd:kd + LANES])
        sbr_ref[:, lo:lo + LANES], sbi_ref[:, lo:lo + LANES] = _pair_lanes(
            res[0][:, kd + LANES:kd + 2 * LANES], res[1][:, kd + LANES:kd + 2 * LANES])


def _s5_chunk_call(u, wa, *, layer, batch, seq):
    t, d_ssm = u.shape
    nchunk = seq // S5_CHUNK
    nv = d_ssm // LANES
    _, ng, kd, ncol = wa.shape
    sw = PAIRS_PER_VREG * LANES
    st = jax.ShapeDtypeStruct((nchunk, batch * nv * sw), F32)
    sspec = pl.BlockSpec((nchunk, sw), lambda b, v: (0, b * nv + v))
    return pl.pallas_call(
        functools.partial(_s5_chunk_kernel, nchunk=nchunk),
        grid=(batch, nv),
        in_specs=[pl.BlockSpec((seq, LANES), lambda b, v: (b, v)),
                  pl.BlockSpec((1, GROUPS_PER_VREG, kd, ncol), lambda b, v: (layer, v, 0, 0))],
        out_specs=[pl.BlockSpec((1, GROUPS_PER_VREG, nchunk, kd), lambda b, v: (b, v, 0, 0)),
                   sspec, sspec, sspec, sspec],
        out_shape=[jax.ShapeDtypeStruct((batch, ng, nchunk, kd), F32), st, st, st, st],
        name="s5_chunk",
        compiler_params=_cparams(
            ("arbitrary", "arbitrary"),
            blocks=[((seq, LANES), F32), ((GROUPS_PER_VREG, kd, ncol), BF16),
                    ((1, GROUPS_PER_VREG, nchunk, kd), F32)] + [((nchunk, sw), F32)] * 4,
            temps=[((seq, LANES), F32)] * 3 + [((nchunk, ncol), F32)] * 4),
    )(u, wa)


def _s5scan_kernel(sfr_ref, sfi_ref, sbr_ref, sbi_ref, afr_ref, afi_ref, abr_ref, abi_ref,
                   xfr_ref, xfi_ref, xbr_ref, xbi_ref, *, nchunk):
    afr, afi = afr_ref[...], afi_ref[...]
    abr, abi = abr_ref[...], abi_ref[...]
    zero = jnp.zeros_like(afr)

    def body(k, carry):
        fr, fi, br, bi = carry
        kb = nchunk - 1 - k
        rowf = pl.ds(k, 1)
        rowb = pl.ds(kb, 1)
        xfr_ref[rowf, :] = fr
        xfi_ref[rowf, :] = fi
        xbr_ref[rowb, :] = br
        xbi_ref[rowb, :] = bi
        nfr = fr * afr - fi * afi + sfr_ref[rowf, :]
        nfi = fr * afi + fi * afr + sfi_ref[rowf, :]
        nbr = br * abr - bi * abi + sbr_ref[rowb, :]
        nbi = br * abi + bi * abr + sbi_ref[rowb, :]
        return nfr, nfi, nbr, nbi

    lax.fori_loop(0, nchunk, body, (zero, zero, zero, zero))


def _s5scan_call(s4, a4, *, nchunk, tl):
    w = s4[0].shape[1]
    sspec = pl.BlockSpec((nchunk, tl), lambda j: (0, j))
    aspec = pl.BlockSpec((1, tl), lambda j: (0, j))
    st = jax.ShapeDtypeStruct((nchunk, w), F32)
    return pl.pallas_call(
        functools.partial(_s5scan_kernel, nchunk=nchunk),
        grid=(w // tl,),
        in_specs=[sspec] * 4 + [aspec] * 4,
        out_specs=[sspec] * 4,
        out_shape=[st] * 4,
        name="s5_scan",
        compiler_params=_cparams(("arbitrary",),
                                 blocks=[((nchunk, tl), F32)] * 8 + [((1, tl), F32)] * 4,
                                 temps=[((SUBLANES_32BIT, tl), F32)] * 16),
    )(*s4, *a4)


def _s5_readout_kernel(yi_ref, xfr_ref, xfi_ref, xbr_ref, xbi_ref, wb_ref, y_ref, *, nchunk):
    nhalf = S5_CHUNK // TOKENS_PER_HALF
    pieces = [[None] * GROUPS_PER_VREG for _ in range(nhalf)]
    for qq in range(PAIRS_PER_VREG):
        lo = qq * LANES
        xf = _pair_lanes(xfr_ref[:, lo:lo + LANES], xfi_ref[:, lo:lo + LANES])
        xb = _pair_lanes(xbr_ref[:, lo:lo + LANES], xbi_ref[:, lo:lo + LANES])
        for gl in range(S5_PAIR):
            g = S5_PAIR * qq + gl
            xs = jnp.concatenate([xf[gl], xb[gl]], axis=-1)
            y = yi_ref[0, g] + jnp.dot(xs.astype(BF16), wb_ref[0, g], preferred_element_type=F32)
            for jh in range(nhalf):
                pieces[jh][g] = y[:, jh * LANES:(jh + 1) * LANES]
    for jh in range(nhalf):
        toks = _granule_transpose(pieces[jh])
        for jj in range(TOKENS_PER_HALF):
            y_ref[pl.ds(jh * TOKENS_PER_HALF + jj, nchunk, stride=S5_CHUNK), :] = toks[jj]


def _s5_readout_call(yi, x4, wb, *, layer, batch, seq):
    _, ng, nchunk, kd = yi.shape
    nrow = wb.shape[2]
    nv = ng // GROUPS_PER_VREG
    sw = PAIRS_PER_VREG * LANES
    xspec = pl.BlockSpec((nchunk, sw), lambda b, v: (0, b * nv + v))
    return pl.pallas_call(
        functools.partial(_s5_readout_kernel, nchunk=nchunk),
        grid=(batch, nv),
        in_specs=[pl.BlockSpec((1, GROUPS_PER_VREG, nchunk, kd), lambda b, v: (b, v, 0, 0)),
                  xspec, xspec, xspec, xspec,
                  pl.BlockSpec((1, GROUPS_PER_VREG, nrow, kd), lambda b, v: (layer, v, 0, 0))],
        out_specs=pl.BlockSpec((seq, LANES), lambda b, v: (b, v)),
        out_shape=jax.ShapeDtypeStruct((batch * seq, nv * LANES), F32),
        name="s5_readout",
        compiler_params=_cparams(
            ("arbitrary", "arbitrary"),
            blocks=[((1, GROUPS_PER_VREG, nchunk, kd), F32), ((GROUPS_PER_VREG, nrow, kd), BF16),
                    ((seq, LANES), F32)] + [((nchunk, sw), F32)] * 4,
            temps=[((seq, LANES), F32)] * 4),
    )(yi, *x4, wb)


def _s5_mixer(u, wa, wb, a_re, a_im, *, layer, batch, seq):
    nchunk = seq // S5_CHUNK
    yi, sfr, sfi, sbr, sbi = _s5_chunk_call(u, wa, layer=layer, batch=batch, seq=seq)
    w = a_re.shape[1]
    a4 = [jnp.tile(a.reshape(1, w), (1, batch)) for a in (a_re[0], a_im[0], a_re[1], a_im[1])]
    tl = min(2048, batch * w)
    x4 = _s5scan_call([sfr, sfi, sbr, sbi], a4, nchunk=nchunk, tl=tl)
    return _s5_readout_call(yi, x4, wb, layer=layer, batch=batch, seq=seq)


ROW_TILE = 8


def _post_kernel(y_ref, at_ref, gs_ref, ga_ref, x_ref, mod_ref, ln_ref,
                 wg_ref, bg_ref, wsu_ref, wau_ref, wo_ref, wr_ref,
                 x1_ref, h2_ref, aff_ref, afft_ref, *, d_model):
    mod = mod_ref[0]
    g1 = mod[:, 2 * d_model:3 * d_model]
    sh2 = mod[:, 3 * d_model:4 * d_model]
    sc2 = mod[:, 4 * d_model:5 * d_model]
    tm = y_ref.shape[0]

    au = jnp.dot(at_ref[...], wau_ref[...], preferred_element_type=F32)
    z = jax.nn.gelu(y_ref[...], approximate=True)
    glu = jnp.dot(z.astype(BF16), wg_ref[...], preferred_element_type=F32) + bg_ref[...]
    ssm = z * jax.nn.sigmoid(glu)
    su = jnp.dot(ssm.astype(BF16), wsu_ref[...], preferred_element_type=F32)
    merged = gs_ref[...].astype(F32) * su + ga_ref[...].astype(F32) * au
    mix = jnp.dot(merged.astype(BF16), wo_ref[...], preferred_element_type=F32)
    x1 = x_ref[...] + g1 * mix
    x1_ref[...] = x1
    ms = jnp.mean(x1 * x1, axis=-1, keepdims=True)
    h2 = (x1 * lax.rsqrt(ms + EPS) * ln_ref[...]) * (1.0 + sc2) + sh2
    for i in range(d_model // LANES):
        h2_ref[pl.ds(i, tm, stride=ROW_TILE), :] = h2[:, i * LANES:(i + 1) * LANES]
    lg = lax.dot_general(wr_ref[...], h2, (((1,), (1,)), ((), ())), preferred_element_type=F32,
                         precision=lax.Precision.HIGHEST)
    lg = lg - jnp.max(lg, axis=0, keepdims=True)
    ex = jnp.exp(lg)
    aff = ex / jnp.sum(ex, axis=0, keepdims=True)
    aff_ref[0] = aff
    pad = jnp.zeros((LANES - aff.shape[0], aff.shape[1]), F32)
    afft_ref[...] = jnp.concatenate([aff, pad], axis=0).T


def _post_call(y, attn, gs, ga, x2, mod3, ln_w, wg, bg, wsu, wau, wo, wr_t, *, seq, tm):
    t, d = x2.shape
    d_ssm = y.shape[1]
    d_attn = attn.shape[1]
    ne = wr_t.shape[0]
    tiles_per_seq = seq // tm
    row = lambda i: (i, 0)
    const = lambda i: (0, 0)
    return pl.pallas_call(
        functools.partial(_post_kernel, d_model=d),
        grid=(t // tm,),
        in_specs=[pl.BlockSpec((tm, d_ssm), row),
                  pl.BlockSpec((tm, d_attn), row),
                  pl.BlockSpec((tm, d), row),
                  pl.BlockSpec((tm, d), row),
                  pl.BlockSpec((tm, d), row),
                  pl.BlockSpec((1, 1, mod3.shape[2]), lambda i: (i // tiles_per_seq, 0, 0)),
                  pl.BlockSpec((1, d), const),
                  pl.BlockSpec(wg.shape, const),
                  pl.BlockSpec((1, d_ssm), const),
                  pl.BlockSpec(wsu.shape, const),
                  pl.BlockSpec(wau.shape, const),
                  pl.BlockSpec(wo.shape, const),
                  pl.BlockSpec(wr_t.shape, const)],
        out_specs=[pl.BlockSpec((tm, d), row),
                   pl.BlockSpec((tm * ROW_TILE, LANES), row),
                   pl.BlockSpec((1, ne, tm), lambda i: (i // tiles_per_seq, 0, i % tiles_per_seq)),
                   pl.BlockSpec((tm, LANES), row)],
        out_shape=[jax.ShapeDtypeStruct((t, d), F32),
                   jax.ShapeDtypeStruct((t * ROW_TILE, LANES), F32),
                   jax.ShapeDtypeStruct((t // seq, ne, seq), F32),
                   jax.ShapeDtypeStruct((t, LANES), F32)],
        name="post_mix",
        compiler_params=_cparams(
            ("arbitrary",),
            blocks=[((tm, d_ssm), F32), ((tm, d_attn), BF16), ((tm, d), BF16), ((tm, d), BF16),
                    ((tm, d), F32), ((1, 1, mod3.shape[2]), F32), ((1, d), F32), (wg.shape, BF16),
                    ((1, d_ssm), F32), (wsu.shape, BF16), (wau.shape, BF16), (wo.shape, BF16),
                    (wr_t.shape, F32), ((tm, d), F32), ((tm * ROW_TILE, LANES), F32),
                    ((1, ne, tm), F32), ((tm, LANES), F32)],
            temps=[((tm, d), F32)] * 6),
    )(y, attn, gs, ga, x2, mod3, ln_w, wg, bg, wsu, wau, wo, wr_t)


def _lane_cumsum(mask_f, tri):
    ne, seq = mask_f.shape
    run = jnp.zeros((ne, 1), F32)
    parts = []
    for j in range(seq // LANES):
        blk = mask_f[:, j * LANES:(j + 1) * LANES]
        cs = jnp.dot(blk.astype(BF16), tri, preferred_element_type=F32) + run
        parts.append(cs)
        run = run + jnp.sum(blk, axis=1, keepdims=True)
    return jnp.concatenate(parts, axis=-1)


def _select_kernel(aff_ref, tri_ref, idx_ref, half_ref, *, cap, seq, ts):
    aff = aff_ref[0]
    ne = aff.shape[0]
    tri = tri_ref[...]

    def body(i, thr):
        cand = thr | (jnp.int32(1) << (30 - i))
        cnt = jnp.sum(jnp.where(aff >= pltpu.bitcast(cand, F32), 1.0, 0.0), axis=1, keepdims=True)
        return jnp.where(cnt >= cap, cand, thr)

    thr = lax.fori_loop(0, 31, body, jnp.zeros((ne, 1), jnp.int32))
    gt = aff >= pltpu.bitcast(thr + 1, F32)
    eq = (aff >= pltpu.bitcast(thr, F32)) & jnp.logical_not(gt)
    need = cap - jnp.sum(jnp.where(gt, 1.0, 0.0), axis=1, keepdims=True)
    eq_f = jnp.where(eq, 1.0, 0.0)
    eq_rank = _lane_cumsum(eq_f, tri) - eq_f
    sel = gt | (eq & (eq_rank < need))
    sel_f = jnp.where(sel, 1.0, 0.0)
    csum = _lane_cumsum(sel_f, tri)
    rank = jnp.where(sel, csum - sel_f, -1.0)
    half_ref[0] = csum[:, seq // 2 - 1:seq // 2].astype(jnp.int32)

    slot = lax.broadcasted_iota(jnp.int32, (cap, ts), 0).astype(F32)
    lane_e = lax.broadcasted_iota(jnp.int32, (cap, ne), 1)
    idx = jnp.zeros((cap, ne), F32)
    for e in range(ne):
        col = jnp.zeros((cap, 1), F32)
        for j in range(seq // ts):
            tok = (lax.broadcasted_iota(jnp.int32, (1, ts), 1) + j * ts).astype(F32)
            hit = rank[e:e + 1, j * ts:(j + 1) * ts] == slot
            col = col + jnp.sum(jnp.where(hit, tok, 0.0), axis=1, keepdims=True)
        idx = jnp.where(lane_e == e, col, idx)
    idx_ref[0] = idx.astype(jnp.int32)


def _select_call(aff_t, tri, *, cap):
    b, ne, seq = aff_t.shape
    ts = min(1024, seq)
    return pl.pallas_call(
        functools.partial(_select_kernel, cap=cap, seq=seq, ts=ts),
        grid=(b,),
        in_specs=[pl.BlockSpec((1, ne, seq), lambda i: (i, 0, 0)),
                  pl.BlockSpec((LANES, LANES), lambda i: (0, 0))],
        out_specs=[pl.BlockSpec((1, cap, ne), lambda i: (i, 0, 0)),
                   pl.BlockSpec((1, ne, 1), lambda i: (i, 0, 0))],
        out_shape=[jax.ShapeDtypeStruct((b, cap, ne), jnp.int32),
                   jax.ShapeDtypeStruct((b, ne, 1), jnp.int32)],
        name="expert_select",
        compiler_params=_cparams(
            ("arbitrary",),
            blocks=[((1, ne, seq), F32), ((LANES, LANES), BF16), ((1, cap, ne), jnp.int32),
                    ((1, ne, 1), jnp.int32)],
            temps=[((ne, seq), F32)] * 8 + [((cap, ts), F32)] * 4),
    )(aff_t, tri)


GATHER_UNROLL = 8


def _gather_kernel(idx_ref, h_ref, aff_ref, xg_ref, gc_ref, rows_ref, arow_ref, *, cap, ne):
    b = pl.program_id(0)
    e = pl.program_id(1)
    base = (b * ne + e) * cap

    def body(g, carry):
        for k in range(GATHER_UNROLL):
            c = g * GATHER_UNROLL + k
            r = idx_ref[base + c]
            src = pl.ds(pl.multiple_of(r * ROW_TILE, ROW_TILE), ROW_TILE)
            dst = pl.ds(pl.multiple_of(c * ROW_TILE, ROW_TILE), ROW_TILE)
            rows_ref[dst, :] = h_ref[src, :]
            arow_ref[pl.ds(c, 1), :] = aff_ref[pl.ds(r, 1), :]
        return carry

    lax.fori_loop(0, cap // GATHER_UNROLL, body, 0)
    xg_ref[0] = jnp.concatenate([rows_ref[pl.ds(i, cap, stride=ROW_TILE), :] for i in range(ROW_TILE)],
                                axis=1).astype(xg_ref.dtype)
    lane = lax.broadcasted_iota(jnp.int32, arow_ref.shape, 1)
    gc_ref[0] = jnp.sum(jnp.where(lane == e, arow_ref[...], 0.0), axis=1, keepdims=True)


def _gather_call(idx_flat, h2t, aff_tm, *, batch, seq, cap, ne):
    d = ROW_TILE * LANES
    assert h2t.shape == (batch * seq * ROW_TILE, LANES)
    grid_spec = pltpu.PrefetchScalarGridSpec(
        num_scalar_prefetch=1,
        grid=(batch, ne),
        in_specs=[pl.BlockSpec((seq * ROW_TILE, LANES), lambda b, e, idx: (b, 0)),
                  pl.BlockSpec((seq, LANES), lambda b, e, idx: (b, 0))],
        out_specs=[pl.BlockSpec((1, cap, d), lambda b, e, idx: (e, b, 0)),
                   pl.BlockSpec((1, cap, 1), lambda b, e, idx: (e, b, 0))],
        scratch_shapes=[pltpu.VMEM((cap * ROW_TILE, LANES), F32), pltpu.VMEM((cap, LANES), F32)],
    )
    return pl.pallas_call(
        functools.partial(_gather_kernel, cap=cap, ne=ne),
        grid_spec=grid_spec,
        out_shape=[jax.ShapeDtypeStruct((ne, batch * cap, d), BF16),
                   jax.ShapeDtypeStruct((ne, batch * cap, 1), F32)],
        name="expert_gather",
        compiler_params=_cparams(
            ("arbitrary", "arbitrary"),
            blocks=[((seq * ROW_TILE, LANES), F32), ((seq, LANES), F32), ((1, cap, d), BF16),
                    ((1, cap, 1), F32)],
            scratch=[((cap * ROW_TILE, LANES), F32), ((cap, LANES), F32)],
            temps=[((cap, d), F32)] * 2),
    )(idx_flat, h2t, aff_tm)


def _ffn_kernel(x_ref, gc_ref, wg_ref, wu_ref, wd_ref, ye_ref, *, rows):
    f = pl.program_id(1)

    @pl.when(f == 0)
    def _init():
        ye_ref[...] = jnp.zeros_like(ye_ref)

    wg = wg_ref[0, 0].astype(BF16)
    wu = wu_ref[0, 0].astype(BF16)
    wd = wd_ref[0, 0].astype(BF16)
    nblk = x_ref.shape[1] // rows

    def up(r):
        x = x_ref[0, r * rows:(r + 1) * rows, :]
        return (jnp.dot(x, wg, preferred_element_type=F32), jnp.dot(x, wu, preferred_element_type=F32))

    nxt = up(0)
    for r in range(nblk):
        hg, hu = nxt
        if r + 1 < nblk:
            nxt = up(r + 1)
        hid = (hg * jax.nn.sigmoid(hg) * hu).astype(BF16)
        ye_ref[0, r * rows:(r + 1) * rows, :] += jnp.dot(hid, wd, preferred_element_type=F32)

    @pl.when(f == pl.num_programs(1) - 1)
    def _gate():
        ye_ref[0] = ye_ref[0] * gc_ref[0]


def _ffn_call(xg, gc, wg, wu, wd, *, layer, tf, rows):
    ne, n, d = xg.shape
    dff = wg.shape[3]
    return pl.pallas_call(
        functools.partial(_ffn_kernel, rows=rows),
        grid=(ne, dff // tf),
        in_specs=[pl.BlockSpec((1, n, d), lambda e, f: (e, 0, 0)),
                  pl.BlockSpec((1, n, 1), lambda e, f: (e, 0, 0)),
                  pl.BlockSpec((1, 1, d, tf), lambda e, f: (layer, e, 0, f)),
                  pl.BlockSpec((1, 1, d, tf), lambda e, f: (layer, e, 0, f)),
                  pl.BlockSpec((1, 1, tf, d), lambda e, f: (layer, e, f, 0))],
        out_specs=pl.BlockSpec((1, n, d), lambda e, f: (e, 0, 0)),
        out_shape=jax.ShapeDtypeStruct((ne, n, d), F32),
        name="expert_ffn",
        compiler_params=_cparams(
            ("arbitrary", "arbitrary"),
            blocks=[((1, n, d), BF16), ((1, n, 1), F32), ((1, 1, d, tf), F32), ((1, 1, d, tf), F32),
                    ((1, 1, tf, d), F32), ((1, n, d), F32)],
            temps=[((d, tf), BF16)] * 3 + [((rows, tf), F32)] * 5 + [((rows, d), F32)]),
    )(xg, gc, wg, wu, wd)


SCATTER_UNROLL = 8
SEQ_HALVES = 2


def _combine_kernel(idx_ref, half_ref, ye_ref, x1_ref, mod_ref, o_ref, acc_ref, *, cap, ne, rows,
                    d_model):
    i = pl.program_id(0)
    e = pl.program_id(1)
    b = i // SEQ_HALVES
    hh = i % SEQ_HALVES

    @pl.when(e == 0)
    def _init():
        o_ref[...] = jnp.zeros_like(o_ref)
        acc_ref[...] = jnp.zeros_like(acc_ref)

    mid = half_ref[b * ne + e]
    lo = jnp.where(hh == 0, 0, mid)
    hi = jnp.where(hh == 0, mid, cap)
    base = (b * ne + e) * cap
    off = hh * rows

    def add_row(c, dst_ref):
        r = idx_ref[base + c] - off
        dst_ref[pl.ds(r, 1), :] += ye_ref[0, pl.ds(c, 1), :]

    def group(g, carry):
        for k in range(SCATTER_UNROLL):
            add_row(lo + g * SCATTER_UNROLL + k, o_ref if k % 2 == 0 else acc_ref)
        return carry

    ngroup = (hi - lo) // SCATTER_UNROLL
    lax.fori_loop(0, ngroup, group, 0)

    def tail(c, carry):
        add_row(c, o_ref)
        return carry

    lax.fori_loop(lo + ngroup * SCATTER_UNROLL, hi, tail, 0)

    @pl.when(e == pl.num_programs(1) - 1)
    def _fin():
        g2 = mod_ref[0][:, 5 * d_model:6 * d_model]
        o_ref[...] = x1_ref[...] + g2 * (o_ref[...] + acc_ref[...])


def _combine_call(idx_flat, half_flat, ye, x1, mod3, *, batch, seq, cap):
    t, d = x1.shape
    ne = ye.shape[0]
    rows = seq // SEQ_HALVES
    grid_spec = pltpu.PrefetchScalarGridSpec(
        num_scalar_prefetch=2,
        grid=(batch * SEQ_HALVES, ne),
        in_specs=[pl.BlockSpec((1, cap, d), lambda i, e, idx, half: (e, i // SEQ_HALVES, 0)),
                  pl.BlockSpec((rows, d), lambda i, e, idx, half: (i, 0)),
                  pl.BlockSpec((1, 1, mod3.shape[2]), lambda i, e, idx, half: (i // SEQ_HALVES, 0, 0))],
        out_specs=pl.BlockSpec((rows, d), lambda i, e, idx, half: (i, 0)),
        scratch_shapes=[pltpu.VMEM((rows, d), F32)],
    )
    return pl.pallas_call(
        functools.partial(_combine_kernel, cap=cap, ne=ne, rows=rows, d_model=d),
        grid_spec=grid_spec,
        out_shape=jax.ShapeDtypeStruct((t, d), F32),
        name="moe_combine",
        compiler_params=_cparams(
            ("arbitrary", "arbitrary"),
            blocks=[((1, cap, d), F32), ((rows, d), F32), ((1, 1, mod3.shape[2]), F32), ((rows, d), F32)],
            scratch=[((rows, d), F32)],
            temps=[((SUBLANES_32BIT * SCATTER_UNROLL, d), F32)] * 4),
    )(idx_flat, half_flat, ye, x1, mod3)


def kernel(x, c, w_mod, b_mod, ln1_w, ln2_w, w_in, ssm_lam_re, ssm_lam_im, ssm_log_dt, ssm_b_re,
           ssm_b_im, ssm_c_re, ssm_c_im, ssm_d, w_glu, b_glu, q_norm_w, k_norm_w, w_ssm_up,
           w_attn_up, w_out, w_router, w_exp_gate, w_exp_up, w_exp_down):
    batch, seq, d = x.shape
    depth = w_mod.shape[0]
    t = batch * seq
    d_attn = N_Q_HEADS * HEAD_DIM
    d_kv = N_KV_HEADS * HEAD_DIM
    cap = CAPACITY_FACTOR * seq // N_EXPERTS
    tm = min(512, seq)
    tq = min(256, seq)

    tabs = _rope_tables(seq)
    head_id = jnp.arange(d_attn) // HEAD_DIM
    hm = (head_id[:, None] == head_id[None, :]).astype(BF16)
    tri = (jnp.arange(LANES)[:, None] <= jnp.arange(LANES)[None, :]).astype(BF16)
    c_pad = jnp.zeros((8, d), F32).at[:batch].set(c)
    mod_all = _mod_call(c_pad, w_mod, b_mod)[:, :batch]
    wa_all, wb_all, a_re_all, a_im_all = _s5_weights(ssm_lam_re, ssm_lam_im, ssm_log_dt, ssm_b_re,
                                                     ssm_b_im, ssm_c_re, ssm_c_im, ssm_d)

    x2 = x.reshape(t, d)
    for l in range(depth):
        mod3 = mod_all[l].reshape(batch, 1, 6 * d)
        qw = jnp.tile(q_norm_w[l], N_Q_HEADS).reshape(1, d_attn)
        kw = jnp.tile(k_norm_w[l], N_KV_HEADS).reshape(1, d_kv)
        u, q, k, v, gs, ga = _in_call(x2, mod3, ln1_w[l].reshape(1, d), w_in[l].astype(BF16), tabs,
                                      qw, kw, hm, seq=seq, tm=tm)
        y = _s5_mixer(u, wa_all, wb_all, a_re_all[l], a_im_all[l], layer=l, batch=batch, seq=seq)
        attn = _attn_call(q, k, v, seq=seq, tq=tq)
        x1, h2, aff_t, aff_tm = _post_call(
            y, attn, gs, ga, x2, mod3, ln2_w[l].reshape(1, d), w_glu[l].astype(BF16),
            b_glu[l].reshape(1, -1), w_ssm_up[l].astype(BF16), w_attn_up[l].astype(BF16),
            w_out[l].astype(BF16), w_router[l].T, seq=seq, tm=tm)
        idx_t, half = _select_call(aff_t, tri, cap=cap)
        idx_flat = jnp.swapaxes(idx_t, 1, 2).reshape(-1)
        half_flat = half.reshape(-1)
        xg, gc = _gather_call(idx_flat, h2, aff_tm, batch=batch, seq=seq, cap=cap, ne=N_EXPERTS)
        ye = _ffn_call(xg, gc, w_exp_gate, w_exp_up, w_exp_down, layer=l, tf=512,
                       rows=min(1024, batch * cap))
        x2 = _combine_call(idx_flat, half_flat, ye, x1, mod3, batch=batch, seq=seq, cap=cap)
    return x2.reshape(batch, seq, d)
```

```python
import functools
import math

import jax
import jax.numpy as jnp
from jax import lax
from jax.experimental import pallas as pl
from jax.experimental.pallas import tpu as pltpu

F32 = jnp.float32
BF16 = jnp.bfloat16

HEAD_DIM = 64
N_Q_HEADS = 8
N_KV_HEADS = 2
Q_PER_KV = N_Q_HEADS // N_KV_HEADS
ROT_HALF = HEAD_DIM // 2
ROPE_THETA = 10000.0
GRID_W = 64
SSM_GROUP = 16
SSM_STATE = 64
N_EXPERTS = 16
CAPACITY_FACTOR = 2
EPS = 1e-6

S5_CHUNK = 16
S5_PAIR = 2
LANES = 128
V7X_VMEM_BYTES = 64 * 1024 * 1024
LOG2E = 1.4426950408889634


PIPELINE_BUFFERS = 2
SUBLANES_32BIT = 8


def _vmem_bytes(shape, dtype):
    itemsize = jnp.dtype(dtype).itemsize
    sub = SUBLANES_32BIT * 4 // itemsize
    lead = math.prod(shape[:-2]) if len(shape) > 2 else 1
    rows = shape[-2] if len(shape) > 1 else 1
    return lead * (-(-rows // sub) * sub) * (-(-shape[-1] // LANES) * LANES) * itemsize


def _cparams(semantics, blocks, scratch=(), temps=()):
    need = (PIPELINE_BUFFERS * sum(_vmem_bytes(s, d) for s, d in blocks)
            + sum(_vmem_bytes(s, d) for s, d in scratch) + sum(_vmem_bytes(s, d) for s, d in temps))
    return pltpu.CompilerParams(dimension_semantics=semantics,
                                vmem_limit_bytes=min(need, V7X_VMEM_BYTES))


def _mod_kernel(c_ref, w_ref, b_ref, o_ref):
    c = c_ref[...]
    ca = c * jax.nn.sigmoid(c)
    o_ref[0] = jnp.dot(ca, w_ref[0], preferred_element_type=F32,
                       precision=lax.Precision.HIGHEST) + b_ref[0]


def _mod_call(c_pad, w_mod, b_mod):
    depth, d, n = w_mod.shape
    rows = c_pad.shape[0]
    tn = 1536
    return pl.pallas_call(
        _mod_kernel,
        grid=(depth, n // tn),
        in_specs=[pl.BlockSpec((rows, d), lambda l, j: (0, 0)),
                  pl.BlockSpec((1, d, tn), lambda l, j: (l, 0, j)),
                  pl.BlockSpec((1, 1, tn), lambda l, j: (l, 0, j))],
        out_specs=pl.BlockSpec((1, rows, tn), lambda l, j: (l, 0, j)),
        out_shape=jax.ShapeDtypeStruct((depth, rows, n), F32),
        name="adaln_mod",
        compiler_params=_cparams(
            ("arbitrary", "arbitrary"),
            blocks=[((rows, d), F32), ((1, d, tn), F32), ((1, 1, tn), F32), ((1, rows, tn), F32)],
            temps=[((d, tn), F32)] * 2),
    )(c_pad, w_mod, b_mod.reshape(depth, 1, n))


def _head_mean_square(xsq, hm):
    hi = xsq.astype(BF16)
    lo = (xsq - hi.astype(F32)).astype(BF16)
    s = jnp.dot(hi, hm, preferred_element_type=F32) + jnp.dot(lo, hm, preferred_element_type=F32)
    return s * (1.0 / HEAD_DIM)


def _rope_block(blk, cos, sa, sb):
    return (blk * cos + pltpu.roll(blk, LANES - ROT_HALF // 2, 1) * sa
            + pltpu.roll(blk, ROT_HALF // 2, 1) * sb)


def _in_kernel(x_ref, mod_ref, ln_ref, w_ref, cos_ref, sa_ref, sb_ref, qw_ref, kw_ref, hm_ref,
               u_ref, q_ref, k_ref, v_ref, gs_ref, ga_ref, *, d_model, d_ssm, d_attn, d_kv):
    x = x_ref[...]
    mod = mod_ref[0]
    sh = mod[:, 0:d_model]
    sc = mod[:, d_model:2 * d_model]
    ms = jnp.mean(x * x, axis=-1, keepdims=True)
    h = (x * lax.rsqrt(ms + EPS) * ln_ref[...]) * (1.0 + sc) + sh
    hb = h.astype(BF16)
    hm = hm_ref[...]
    cos = cos_ref[...]
    sa = sa_ref[...]
    sb = sb_ref[...]
    qscale = (HEAD_DIM ** -0.5) * LOG2E

    def finish_u(p):
        u_ref[...] = p.astype(u_ref.dtype)

    def finish_q(q):
        qn = q * lax.rsqrt(_head_mean_square(q * q, hm) + EPS) * qw_ref[...]
        qr = [_rope_block(qn[:, j * LANES:(j + 1) * LANES], cos, sa, sb) for j in range(d_attn // LANES)]
        q_ref[...] = (jnp.concatenate(qr, axis=-1) * qscale).astype(q_ref.dtype)

    def finish_kv(p):
        k = p[:, :d_kv]
        v_ref[0] = p[:, d_kv:].T.astype(v_ref.dtype)
        kn = k * lax.rsqrt(_head_mean_square(k * k, hm[:d_kv, :d_kv]) + EPS) * kw_ref[...]
        kr = [_rope_block(kn[:, j * LANES:(j + 1) * LANES], cos, sa, sb) for j in range(d_kv // LANES)]
        k_ref[...] = jnp.concatenate(kr, axis=-1).astype(k_ref.dtype)

    def finish_gs(p):
        gs_ref[...] = jax.nn.sigmoid(p).astype(gs_ref.dtype)

    def finish_ga(p):
        ga_ref[...] = jax.nn.sigmoid(p).astype(ga_ref.dtype)

    groups = ((d_ssm, finish_u), (d_attn, finish_q), (2 * d_kv, finish_kv),
              (d_model, finish_gs), (d_model, finish_ga))
    starts = [sum(g[0] for g in groups[:i]) for i in range(len(groups))]

    def project(i):
        return jnp.dot(hb, w_ref[:, starts[i]:starts[i] + groups[i][0]], preferred_element_type=F32)

    nxt = project(0)
    for i, (_, finish) in enumerate(groups):
        cur = nxt
        if i + 1 < len(groups):
            nxt = project(i + 1)
        finish(cur)


def _in_call(x2, mod3, ln_w, w_in, tabs, qw, kw, hm, *, seq, tm):
    t, d = x2.shape
    n_in = w_in.shape[1]
    d_ssm = d // 2
    d_attn = N_Q_HEADS * HEAD_DIM
    d_kv = N_KV_HEADS * HEAD_DIM
    tiles_per_seq = seq // tm
    cos, sa, sb = tabs
    kern = functools.partial(_in_kernel, d_model=d, d_ssm=d_ssm, d_attn=d_attn, d_kv=d_kv)
    row = lambda i: (i, 0)
    const = lambda i: (0, 0)
    tab = lambda i: (i % tiles_per_seq, 0)
    return pl.pallas_call(
        kern,
        grid=(t // tm,),
        in_specs=[pl.BlockSpec((tm, d), row),
                  pl.BlockSpec((1, 1, mod3.shape[2]), lambda i: (i // tiles_per_seq, 0, 0)),
                  pl.BlockSpec((1, d), const),
                  pl.BlockSpec((d, n_in), const),
                  pl.BlockSpec((tm, LANES), tab),
                  pl.BlockSpec((tm, LANES), tab),
                  pl.BlockSpec((tm, LANES), tab),
                  pl.BlockSpec((1, d_attn), const),
                  pl.BlockSpec((1, d_kv), const),
                  pl.BlockSpec((d_attn, d_attn), const)],
        out_specs=[pl.BlockSpec((tm, d_ssm), row),
                   pl.BlockSpec((tm, d_attn), row),
                   pl.BlockSpec((tm, d_kv), row),
                   pl.BlockSpec((1, d_kv, tm), lambda i: (i // tiles_per_seq, 0, i % tiles_per_seq)),
                   pl.BlockSpec((tm, d), row),
                   pl.BlockSpec((tm, d), row)],
        out_shape=[jax.ShapeDtypeStruct((t, d_ssm), F32),
                   jax.ShapeDtypeStruct((t, d_attn), BF16),
                   jax.ShapeDtypeStruct((t, d_kv), BF16),
                   jax.ShapeDtypeStruct((t // seq, d_kv, seq), BF16),
                   jax.ShapeDtypeStruct((t, d), BF16),
                   jax.ShapeDtypeStruct((t, d), BF16)],
        name="in_proj",
        compiler_params=_cparams(
            ("arbitrary",),
            blocks=[((tm, d), F32), ((1, 1, mod3.shape[2]), F32), ((1, d), F32), ((d, n_in), BF16),
                    ((tm, LANES), F32), ((tm, LANES), F32), ((tm, LANES), F32),
                    ((1, d_attn), F32), ((1, d_kv), F32), ((d_attn, d_attn), BF16),
                    ((tm, d_ssm), F32), ((tm, d_attn), BF16), ((tm, d_kv), BF16), ((1, d_kv, tm), BF16),
                    ((tm, d), BF16), ((tm, d), BF16)],
            temps=[((tm, d), F32)] * 6),
    )(x2, mod3, ln_w, w_in, cos, sa, sb, qw, kw, hm)


def _rope_tables(seq):
    pos = jnp.arange(seq)
    row = (pos // GRID_W).astype(F32)
    col = (pos % GRID_W).astype(F32)
    inv = 1.0 / (ROPE_THETA ** (jnp.arange(0, ROT_HALF, 2, dtype=F32) / ROT_HALF))
    ang_r = row[:, None] * inv[None, :]
    ang_c = col[:, None] * inv[None, :]
    zeros = jnp.zeros_like(ang_r)
    cos_h = jnp.concatenate([jnp.cos(ang_r), jnp.cos(ang_r), jnp.cos(ang_c), jnp.cos(ang_c)], axis=-1)
    sa_h = jnp.concatenate([-jnp.sin(ang_r), zeros, -jnp.sin(ang_c), zeros], axis=-1)
    sb_h = jnp.concatenate([zeros, jnp.sin(ang_r), zeros, jnp.sin(ang_c)], axis=-1)
    rep = LANES // HEAD_DIM
    return tuple(jnp.tile(a, (1, rep)) for a in (cos_h, sa_h, sb_h))


def _attn_heads_out(o_t, tq):
    return [o_t[:, g * tq:(g + 1) * tq].T for g in range(Q_PER_KV)]


def _attn_kernel(q_ref, k_ref, vt_ref, o_ref, *, kb):
    tq = q_ref.shape[0]
    seq = k_ref.shape[0]
    nblk = seq // kb
    width = Q_PER_KV * HEAD_DIM
    q_ts = []
    for kvh in range(N_KV_HEADS):
        qt = q_ref[:, kvh * width:(kvh + 1) * width].astype(F32).T.astype(BF16)
        q_ts.append(jnp.concatenate([qt[g * HEAD_DIM:(g + 1) * HEAD_DIM] for g in range(Q_PER_KV)],
                                    axis=1))

    def scores(kvh, j):
        lo = kvh * HEAD_DIM
        return jnp.dot(k_ref[j * kb:(j + 1) * kb, lo:lo + HEAD_DIM], q_ts[kvh],
                       preferred_element_type=F32)

    def weighted_v(kvh, j, p):
        lo = kvh * HEAD_DIM
        return jnp.dot(vt_ref[0, lo:lo + HEAD_DIM, j * kb:(j + 1) * kb], p.astype(BF16),
                       preferred_element_type=F32)

    outs = []
    worst = jnp.zeros((1, 1), F32)
    for kvh in range(N_KV_HEADS):
        s0 = scores(kvh, 0)
        m0 = jnp.max(s0, axis=0, keepdims=True)
        l = jnp.zeros_like(m0)
        acc = jnp.zeros((HEAD_DIM, m0.shape[1]), F32)
        s_cur = s0
        for j in range(nblk):
            s_next = scores(kvh, j + 1) if j + 1 < nblk else None
            p = jnp.exp2(s_cur - m0)
            l = l + jnp.sum(p, axis=0, keepdims=True)
            acc = acc + weighted_v(kvh, j, p)
            s_cur = s_next
        worst = jnp.maximum(worst, jnp.max(l, axis=1, keepdims=True))
        outs.extend(_attn_heads_out(acc / l, tq))
    o_ref[...] = jnp.concatenate(outs, axis=-1).astype(o_ref.dtype)

    @pl.when(worst[0, 0] > jnp.finfo(F32).max)
    def _exact():
        outs = []
        for kvh in range(N_KV_HEADS):
            m = jnp.full((1, Q_PER_KV * tq), -jnp.inf, F32)
            l = jnp.zeros_like(m)
            acc = jnp.zeros((HEAD_DIM, Q_PER_KV * tq), F32)
            for j in range(nblk):
                s = scores(kvh, j)
                m_new = jnp.maximum(m, jnp.max(s, axis=0, keepdims=True))
                alpha = jnp.exp2(m - m_new)
                p = jnp.exp2(s - m_new)
                l = l * alpha + jnp.sum(p, axis=0, keepdims=True)
                acc = acc * alpha + weighted_v(kvh, j, p)
                m = m_new
            outs.extend(_attn_heads_out(acc / l, tq))
        o_ref[...] = jnp.concatenate(outs, axis=-1).astype(o_ref.dtype)


def _attn_call(q, k, vt, *, seq, tq):
    t, d_attn = q.shape
    d_kv = k.shape[1]
    nq = seq // tq
    kb = min(256, seq)
    return pl.pallas_call(
        functools.partial(_attn_kernel, kb=kb),
        grid=(t // seq, nq),
        in_specs=[pl.BlockSpec((tq, d_attn), lambda b, i: (b * nq + i, 0)),
                  pl.BlockSpec((seq, d_kv), lambda b, i: (b, 0)),
                  pl.BlockSpec((1, d_kv, seq), lambda b, i: (b, 0, 0))],
        out_specs=pl.BlockSpec((tq, d_attn), lambda b, i: (b * nq + i, 0)),
        out_shape=jax.ShapeDtypeStruct((t, d_attn), BF16),
        name="attention",
        compiler_params=_cparams(
            ("arbitrary", "arbitrary"),
            blocks=[((tq, d_attn), BF16), ((seq, d_kv), BF16), ((1, d_kv, seq), BF16), ((tq, d_attn), BF16)],
            temps=[((kb, Q_PER_KV * tq), F32)] * 2 * (seq // kb + 4)),
    )(q, k, vt)


def _s5_weights(lam_re, lam_im, log_dt, b_re, b_im, c_re, c_im, d_skip):
    hp = lax.Precision.HIGHEST
    nl, _, ng, p = lam_re.shape
    cg = SSM_GROUP
    L = S5_CHUNK
    kd = L * cg
    dt = jnp.exp(log_dt)[..., None]
    zr = lam_re * dt
    zi = lam_im * dt

    def cpow(e):
        mag = jnp.exp(zr[..., None] * e)
        return mag * jnp.cos(zi[..., None] * e), mag * jnp.sin(zi[..., None] * e)

    lb_re, lb_im = jnp.exp(zr) * jnp.cos(zi), jnp.exp(zr) * jnp.sin(zi)
    nr, ni = lb_re - 1.0, lb_im
    den = lam_re * lam_re + lam_im * lam_im
    f_re = ((nr * lam_re + ni * lam_im) / den)[..., None]
    f_im = ((ni * lam_re - nr * lam_im) / den)[..., None]
    bb_re = f_re * b_re - f_im * b_im
    bb_im = f_re * b_im + f_im * b_re

    m_idx = jnp.arange(L + 1, dtype=F32)
    expo = jnp.stack([m_idx, L - m_idx])[None, :, None, None, :]
    pw_re, pw_im = cpow(expo)
    rep = jnp.repeat(jnp.eye(L + 1, dtype=F32), cg, axis=1)
    til = jnp.tile(jnp.eye(cg, dtype=F32), (1, L + 1))
    pwx_re = jnp.einsum('dxgpm,mn->dxgpn', pw_re, rep, precision=hp)
    pwx_im = jnp.einsum('dxgpm,mn->dxgpn', pw_im, rep, precision=hp)
    ct_re = jnp.einsum('dxgcp,cn->dxgpn', c_re, til, precision=hp)
    ct_im = jnp.einsum('dxgcp,cn->dxgpn', c_im, til, precision=hp)
    ca_re = ct_re * pwx_re - ct_im * pwx_im
    ca_im = ct_re * pwx_im + ct_im * pwx_re
    ef_re, ef_im = ca_re[:, 0, ..., :kd], ca_im[:, 0, ..., :kd]
    w2f_re, w2f_im = ca_re[:, 0, ..., cg:], ca_im[:, 0, ..., cg:]
    w2b_re, w2b_im = ca_re[:, 1, ..., :kd], ca_im[:, 1, ..., :kd]
    eb_re, eb_im = ca_re[:, 1, ..., cg:], ca_im[:, 1, ..., cg:]

    def lag_rows(x, e_re, e_im):
        return (jnp.einsum('dgpe,dgpn->dgen', bb_re[:, x], e_re, precision=hp)
                - jnp.einsum('dgpe,dgpn->dgen', bb_im[:, x], e_im, precision=hp))

    kf = lag_rows(0, ef_re, ef_im)
    kb = lag_rows(1, eb_re, eb_im)
    zeros = jnp.zeros_like(kf)
    pf = jnp.concatenate([zeros, kf], axis=-1)
    pb = jnp.concatenate([kb, zeros], axis=-1)
    tf = jnp.stack([pf[..., kd - cg * s:2 * kd - cg * s] for s in range(L)], axis=2)
    tb = jnp.stack([pb[..., cg * (L - 1 - s):cg * (L - 1 - s) + kd] for s in range(L)], axis=2)
    skip = jnp.eye(kd, dtype=F32) * jnp.tile(d_skip.reshape(nl, ng, 1, cg), (1, 1, 1, L))
    tmat = (tf + tb).reshape(nl, ng, kd, kd) + skip

    s_idx = jnp.arange(L, dtype=F32)
    sexp = jnp.stack([L - 1 - s_idx, s_idx])[None, :, None, None, :]
    ps_re, ps_im = cpow(sexp)
    w1_re = (jnp.einsum('dxgps,dxgpe->dxgsep', ps_re, bb_re)
             - jnp.einsum('dxgps,dxgpe->dxgsep', ps_im, bb_im)).reshape(nl, 2, ng, kd, p)
    w1_im = (jnp.einsum('dxgps,dxgpe->dxgsep', ps_re, bb_im)
             + jnp.einsum('dxgps,dxgpe->dxgsep', ps_im, bb_re)).reshape(nl, 2, ng, kd, p)

    wa = jnp.concatenate([tmat, w1_re[:, 0], w1_im[:, 0], w1_re[:, 1], w1_im[:, 1]], axis=-1)
    wb = jnp.concatenate([w2f_re, -w2f_im, w2b_re, -w2b_im], axis=2)
    al_re, al_im = cpow(jnp.float32(L))
    a_re = al_re.reshape(nl, 2, ng * p)
    a_im = al_im.reshape(nl, 2, ng * p)
    return wa.astype(BF16), wb.astype(BF16), a_re, a_im


GROUPS_PER_VREG = LANES // SSM_GROUP
PAIRS_PER_VREG = GROUPS_PER_VREG // S5_PAIR
TOKENS_PER_HALF = LANES // SSM_GROUP


def _granule_transpose(arrs):
    gran = lax.broadcasted_iota(jnp.int32, arrs[0].shape, 1) // SSM_GROUP
    cur = list(arrs)
    for dist in (4, 2, 1):
        keep = (gran & dist) == 0
        nxt = list(cur)
        for i in range(len(cur)):
            if i & dist:
                continue
            a, b = cur[i], cur[i + dist]
            nxt[i] = jnp.where(keep, a, pltpu.roll(b, dist * SSM_GROUP, 1))
            nxt[i + dist] = jnp.where(keep, pltpu.roll(a, LANES - dist * SSM_GROUP, 1), b)
        cur = nxt
    return cur


def _pair_lanes(a, b):
    low = lax.broadcasted_iota(jnp.int32, a.shape, 1) < LANES // 2
    return (jnp.where(low, a, pltpu.roll(b, LANES // 2, 1)),
            jnp.where(low, pltpu.roll(a, LANES // 2, 1), b))


def _s5_chunk_kernel(u_ref, wa_ref, yi_ref, sfr_ref, sfi_ref, sbr_ref, sbi_ref, *, nchunk):
    halves = []
    for jh in range(S5_CHUNK // TOKENS_PER_HALF):
        toks = [u_ref[pl.ds(jh * TOKENS_PER_HALF + jj, nchunk, stride=S5_CHUNK), :]
                for jj in range(TOKENS_PER_HALF)]
        halves.append([o.astype(BF16) for o in _granule_transpose(toks)])
    kd = S5_CHUNK * SSM_GROUP
    for qq in range(PAIRS_PER_VREG):
        res = []
        for gl in range(S5_PAIR):
            g = S5_PAIR * qq + gl
            z = jnp.concatenate([halves[jh][g] for jh in range(len(halves))], axis=-1)
            r = jnp.dot(z, wa_ref[0, g], preferred_element_type=F32)
            yi_ref[0, g] = r[:, :kd].astype(yi_ref.dtype)
            res.append(r)
        lo = qq * LANES
        sfr_ref[:, lo:lo + LANES], sfi_ref[:, lo:lo + LANES] = _pair_lanes(
            res[0][:, kd:kd + LANES], res[1][:, kd:kd + LANES])
        sbr_ref[:, lo:lo + LANES], sbi_ref[:, lo:lo + LANES] = _pair_lanes(
            res[0][:, kd + LANES:kd + 2 * LANES], res[1][:, kd + LANES:kd + 2 * LANES])


def _s5_chunk_call(u, wa, *, layer, batch, seq):
    t, d_ssm = u.shape
    nchunk = seq // S5_CHUNK
    nv = d_ssm // LANES
    _, ng, kd, ncol = wa.shape
    sw = PAIRS_PER_VREG * LANES
    st = jax.ShapeDtypeStruct((nchunk, batch * nv * sw), F32)
    sspec = pl.BlockSpec((nchunk, sw), lambda b, v: (0, b * nv + v))
    return pl.pallas_call(
        functools.partial(_s5_chunk_kernel, nchunk=nchunk),
        grid=(batch, nv),
        in_specs=[pl.BlockSpec((seq, LANES), lambda b, v: (b, v)),
                  pl.BlockSpec((1, GROUPS_PER_VREG, kd, ncol), lambda b, v: (layer, v, 0, 0))],
        out_specs=[pl.BlockSpec((1, GROUPS_PER_VREG, nchunk, kd), lambda b, v: (b, v, 0, 0)),
                   sspec, sspec, sspec, sspec],
        out_shape=[jax.ShapeDtypeStruct((batch, ng, nchunk, kd), BF16), st, st, st, st],
        name="s5_chunk",
        compiler_params=_cparams(
            ("arbitrary", "arbitrary"),
            blocks=[((seq, LANES), F32), ((GROUPS_PER_VREG, kd, ncol), BF16),
                    ((1, GROUPS_PER_VREG, nchunk, kd), BF16)] + [((nchunk, sw), F32)] * 4,
            temps=[((seq, LANES), F32)] * 3 + [((nchunk, ncol), F32)] * 4),
    )(u, wa)


def _s5scan_kernel(sfr_ref, sfi_ref, sbr_ref, sbi_ref, afr_ref, afi_ref, abr_ref, abi_ref,
                   xfr_ref, xfi_ref, xbr_ref, xbi_ref, *, nchunk):
    afr, afi = afr_ref[...], afi_ref[...]
    abr, abi = abr_ref[...], abi_ref[...]
    zero = jnp.zeros_like(afr)

    def body(k, carry):
        fr, fi, br, bi = carry
        kb = nchunk - 1 - k
        rowf = pl.ds(k, 1)
        rowb = pl.ds(kb, 1)
        xfr_ref[rowf, :] = fr
        xfi_ref[rowf, :] = fi
        xbr_ref[rowb, :] = br
        xbi_ref[rowb, :] = bi
        nfr = fr * afr - fi * afi + sfr_ref[rowf, :]
        nfi = fr * afi + fi * afr + sfi_ref[rowf, :]
        nbr = br * abr - bi * abi + sbr_ref[rowb, :]
        nbi = br * abi + bi * abr + sbi_ref[rowb, :]
        return nfr, nfi, nbr, nbi

    lax.fori_loop(0, nchunk, body, (zero, zero, zero, zero))


def _s5scan_call(s4, a4, *, nchunk, tl):
    w = s4[0].shape[1]
    sspec = pl.BlockSpec((nchunk, tl), lambda j: (0, j))
    aspec = pl.BlockSpec((1, tl), lambda j: (0, j))
    st = jax.ShapeDtypeStruct((nchunk, w), F32)
    return pl.pallas_call(
        functools.partial(_s5scan_kernel, nchunk=nchunk),
        grid=(w // tl,),
        in_specs=[sspec] * 4 + [aspec] * 4,
        out_specs=[sspec] * 4,
        out_shape=[st] * 4,
        name="s5_scan",
        compiler_params=_cparams(("arbitrary",),
                                 blocks=[((nchunk, tl), F32)] * 8 + [((1, tl), F32)] * 4,
                                 temps=[((SUBLANES_32BIT, tl), F32)] * 16),
    )(*s4, *a4)


def _s5_readout_kernel(yi_ref, xfr_ref, xfi_ref, xbr_ref, xbi_ref, wb_ref, y_ref, *, nchunk):
    nhalf = S5_CHUNK // TOKENS_PER_HALF
    pieces = [[None] * GROUPS_PER_VREG for _ in range(nhalf)]
    for qq in range(PAIRS_PER_VREG):
        lo = qq * LANES
        xf = _pair_lanes(xfr_ref[:, lo:lo + LANES], xfi_ref[:, lo:lo + LANES])
        xb = _pair_lanes(xbr_ref[:, lo:lo + LANES], xbi_ref[:, lo:lo + LANES])
        for gl in range(S5_PAIR):
            g = S5_PAIR * qq + gl
            xs = jnp.concatenate([xf[gl], xb[gl]], axis=-1)
            y = yi_ref[0, g].astype(F32) + jnp.dot(xs.astype(BF16), wb_ref[0, g],
                                                   preferred_element_type=F32)
            for jh in range(nhalf):
                pieces[jh][g] = y[:, jh * LANES:(jh + 1) * LANES]
    for jh in range(nhalf):
        toks = _granule_transpose(pieces[jh])
        for jj in range(TOKENS_PER_HALF):
            y_ref[pl.ds(jh * TOKENS_PER_HALF + jj, nchunk, stride=S5_CHUNK), :] = toks[jj]


def _s5_readout_call(yi, x4, wb, *, layer, batch, seq):
    _, ng, nchunk, kd = yi.shape
    nrow = wb.shape[2]
    nv = ng // GROUPS_PER_VREG
    sw = PAIRS_PER_VREG * LANES
    xspec = pl.BlockSpec((nchunk, sw), lambda b, v: (0, b * nv + v))
    return pl.pallas_call(
        functools.partial(_s5_readout_kernel, nchunk=nchunk),
        grid=(batch, nv),
        in_specs=[pl.BlockSpec((1, GROUPS_PER_VREG, nchunk, kd), lambda b, v: (b, v, 0, 0)),
                  xspec, xspec, xspec, xspec,
                  pl.BlockSpec((1, GROUPS_PER_VREG, nrow, kd), lambda b, v: (layer, v, 0, 0))],
        out_specs=pl.BlockSpec((seq, LANES), lambda b, v: (b, v)),
        out_shape=jax.ShapeDtypeStruct((batch * seq, nv * LANES), F32),
        name="s5_readout",
        compiler_params=_cparams(
            ("arbitrary", "arbitrary"),
            blocks=[((1, GROUPS_PER_VREG, nchunk, kd), BF16), ((GROUPS_PER_VREG, nrow, kd), BF16),
                    ((seq, LANES), F32)] + [((nchunk, sw), F32)] * 4,
            temps=[((seq, LANES), F32)] * 4),
    )(yi, *x4, wb)


def _s5_mixer(u, wa, wb, a_re, a_im, *, layer, batch, seq):
    nchunk = seq // S5_CHUNK
    yi, sfr, sfi, sbr, sbi = _s5_chunk_call(u, wa, layer=layer, batch=batch, seq=seq)
    w = a_re.shape[1]
    a4 = [jnp.tile(a.reshape(1, w), (1, batch)) for a in (a_re[0], a_im[0], a_re[1], a_im[1])]
    tl = min(2048, batch * w)
    x4 = _s5scan_call([sfr, sfi, sbr, sbi], a4, nchunk=nchunk, tl=tl)
    return _s5_readout_call(yi, x4, wb, layer=layer, batch=batch, seq=seq)


ROW_TILE = 8


def _post_kernel(y_ref, at_ref, gs_ref, ga_ref, x_ref, mod_ref, ln_ref,
                 wg_ref, bg_ref, wsu_ref, wau_ref, wo_ref, wr_ref,
                 x1_ref, h2_ref, aff_ref, afft_ref, *, d_model):
    mod = mod_ref[0]
    g1 = mod[:, 2 * d_model:3 * d_model]
    sh2 = mod[:, 3 * d_model:4 * d_model]
    sc2 = mod[:, 4 * d_model:5 * d_model]
    tm = y_ref.shape[0]

    au = jnp.dot(at_ref[...], wau_ref[...], preferred_element_type=F32)
    z = jax.nn.gelu(y_ref[...], approximate=True)
    glu = jnp.dot(z.astype(BF16), wg_ref[...], preferred_element_type=F32) + bg_ref[...]
    ssm = z * jax.nn.sigmoid(glu)
    su = jnp.dot(ssm.astype(BF16), wsu_ref[...], preferred_element_type=F32)
    merged = gs_ref[...].astype(F32) * su + ga_ref[...].astype(F32) * au
    mix = jnp.dot(merged.astype(BF16), wo_ref[...], preferred_element_type=F32)
    x1 = x_ref[...] + g1 * mix
    x1_ref[...] = x1
    ms = jnp.mean(x1 * x1, axis=-1, keepdims=True)
    h2 = (x1 * lax.rsqrt(ms + EPS) * ln_ref[...]) * (1.0 + sc2) + sh2
    for i in range(d_model // LANES):
        h2_ref[pl.ds(i, tm, stride=ROW_TILE), :] = h2[:, i * LANES:(i + 1) * LANES]
    lg = lax.dot_general(wr_ref[...], h2, (((1,), (1,)), ((), ())), preferred_element_type=F32,
                         precision=lax.Precision.HIGHEST)
    lg = lg - jnp.max(lg, axis=0, keepdims=True)
    ex = jnp.exp(lg)
    aff = ex / jnp.sum(ex, axis=0, keepdims=True)
    aff_ref[0] = aff
    pad = jnp.zeros((LANES - aff.shape[0], aff.shape[1]), F32)
    afft_ref[...] = jnp.concatenate([aff, pad], axis=0).T


def _post_call(y, attn, gs, ga, x2, mod3, ln_w, wg, bg, wsu, wau, wo, wr_t, *, seq, tm):
    t, d = x2.shape
    d_ssm = y.shape[1]
    d_attn = attn.shape[1]
    ne = wr_t.shape[0]
    tiles_per_seq = seq // tm
    row = lambda i: (i, 0)
    const = lambda i: (0, 0)
    return pl.pallas_call(
        functools.partial(_post_kernel, d_model=d),
        grid=(t // tm,),
        in_specs=[pl.BlockSpec((tm, d_ssm), row),
                  pl.BlockSpec((tm, d_attn), row),
                  pl.BlockSpec((tm, d), row),
                  pl.BlockSpec((tm, d), row),
                  pl.BlockSpec((tm, d), row),
                  pl.BlockSpec((1, 1, mod3.shape[2]), lambda i: (i // tiles_per_seq, 0, 0)),
                  pl.BlockSpec((1, d), const),
                  pl.BlockSpec(wg.shape, const),
                  pl.BlockSpec((1, d_ssm), const),
                  pl.BlockSpec(wsu.shape, const),
                  pl.BlockSpec(wau.shape, const),
                  pl.BlockSpec(wo.shape, const),
                  pl.BlockSpec(wr_t.shape, const)],
        out_specs=[pl.BlockSpec((tm, d), row),
                   pl.BlockSpec((tm * ROW_TILE, LANES), row),
                   pl.BlockSpec((1, ne, tm), lambda i: (i // tiles_per_seq, 0, i % tiles_per_seq)),
                   pl.BlockSpec((tm, LANES), row)],
        out_shape=[jax.ShapeDtypeStruct((t, d), F32),
                   jax.ShapeDtypeStruct((t * ROW_TILE, LANES), F32),
                   jax.ShapeDtypeStruct((t // seq, ne, seq), F32),
                   jax.ShapeDtypeStruct((t, LANES), F32)],
        name="post_mix",
        compiler_params=_cparams(
            ("arbitrary",),
            blocks=[((tm, d_ssm), F32), ((tm, d_attn), BF16), ((tm, d), BF16), ((tm, d), BF16),
                    ((tm, d), F32), ((1, 1, mod3.shape[2]), F32), ((1, d), F32), (wg.shape, BF16),
                    ((1, d_ssm), F32), (wsu.shape, BF16), (wau.shape, BF16), (wo.shape, BF16),
                    (wr_t.shape, F32), ((tm, d), F32), ((tm * ROW_TILE, LANES), F32),
                    ((1, ne, tm), F32), ((tm, LANES), F32)],
            temps=[((tm, d), F32)] * 6),
    )(y, attn, gs, ga, x2, mod3, ln_w, wg, bg, wsu, wau, wo, wr_t)


def _lane_cumsum(mask_f, tri):
    ne, seq = mask_f.shape
    run = jnp.zeros((ne, 1), F32)
    parts = []
    for j in range(seq // LANES):
        blk = mask_f[:, j * LANES:(j + 1) * LANES]
        cs = jnp.dot(blk.astype(BF16), tri, preferred_element_type=F32) + run
        parts.append(cs)
        run = run + jnp.sum(blk, axis=1, keepdims=True)
    return jnp.concatenate(parts, axis=-1)


def _select_kernel(aff_ref, tri_ref, idx_ref, half_ref, *, cap, seq, ts):
    aff = aff_ref[0]
    ne = aff.shape[0]
    tri = tri_ref[...]

    def body(i, thr):
        cand = thr | (jnp.int32(1) << (30 - i))
        cnt = jnp.sum(jnp.where(aff >= pltpu.bitcast(cand, F32), 1.0, 0.0), axis=1, keepdims=True)
        return jnp.where(cnt >= cap, cand, thr)

    thr = lax.fori_loop(0, 31, body, jnp.zeros((ne, 1), jnp.int32))
    gt = aff >= pltpu.bitcast(thr + 1, F32)
    eq = (aff >= pltpu.bitcast(thr, F32)) & jnp.logical_not(gt)
    need = cap - jnp.sum(jnp.where(gt, 1.0, 0.0), axis=1, keepdims=True)
    eq_f = jnp.where(eq, 1.0, 0.0)
    eq_rank = _lane_cumsum(eq_f, tri) - eq_f
    sel = gt | (eq & (eq_rank < need))
    sel_f = jnp.where(sel, 1.0, 0.0)
    csum = _lane_cumsum(sel_f, tri)
    rank = jnp.where(sel, csum - sel_f, -1.0)
    half_ref[0] = csum[:, seq // 2 - 1:seq // 2].astype(jnp.int32)

    slot = lax.broadcasted_iota(jnp.int32, (cap, ts), 0).astype(F32)
    lane_e = lax.broadcasted_iota(jnp.int32, (cap, ne), 1)
    idx = jnp.zeros((cap, ne), F32)
    for e in range(ne):
        col = jnp.zeros((cap, 1), F32)
        for j in range(seq // ts):
            tok = (lax.broadcasted_iota(jnp.int32, (1, ts), 1) + j * ts).astype(F32)
            hit = rank[e:e + 1, j * ts:(j + 1) * ts] == slot
            col = col + jnp.sum(jnp.where(hit, tok, 0.0), axis=1, keepdims=True)
        idx = jnp.where(lane_e == e, col, idx)
    idx_ref[0] = idx.astype(jnp.int32)


def _select_call(aff_t, tri, *, cap):
    b, ne, seq = aff_t.shape
    ts = min(1024, seq)
    return pl.pallas_call(
        functools.partial(_select_kernel, cap=cap, seq=seq, ts=ts),
        grid=(b,),
        in_specs=[pl.BlockSpec((1, ne, seq), lambda i: (i, 0, 0)),
                  pl.BlockSpec((LANES, LANES), lambda i: (0, 0))],
        out_specs=[pl.BlockSpec((1, cap, ne), lambda i: (i, 0, 0)),
                   pl.BlockSpec((1, ne, 1), lambda i: (i, 0, 0))],
        out_shape=[jax.ShapeDtypeStruct((b, cap, ne), jnp.int32),
                   jax.ShapeDtypeStruct((b, ne, 1), jnp.int32)],
        name="expert_select",
        compiler_params=_cparams(
            ("arbitrary",),
            blocks=[((1, ne, seq), F32), ((LANES, LANES), BF16), ((1, cap, ne), jnp.int32),
                    ((1, ne, 1), jnp.int32)],
            temps=[((ne, seq), F32)] * 8 + [((cap, ts), F32)] * 4),
    )(aff_t, tri)


GATHER_UNROLL = 8


def _gather_kernel(idx_ref, h_ref, aff_ref, xg_ref, gc_ref, rows_ref, arow_ref, *, cap, ne):
    b = pl.program_id(0)
    e = pl.program_id(1)
    base = (b * ne + e) * cap

    def body(g, carry):
        for k in range(GATHER_UNROLL):
            c = g * GATHER_UNROLL + k
            r = idx_ref[base + c]
            src = pl.ds(pl.multiple_of(r * ROW_TILE, ROW_TILE), ROW_TILE)
            dst = pl.ds(pl.multiple_of(c * ROW_TILE, ROW_TILE), ROW_TILE)
            rows_ref[dst, :] = h_ref[src, :]
            arow_ref[pl.ds(c, 1), :] = aff_ref[pl.ds(r, 1), :]
        return carry

    lax.fori_loop(0, cap // GATHER_UNROLL, body, 0)
    xg_ref[0] = jnp.concatenate([rows_ref[pl.ds(i, cap, stride=ROW_TILE), :] for i in range(ROW_TILE)],
                                axis=1).astype(xg_ref.dtype)
    lane = lax.broadcasted_iota(jnp.int32, arow_ref.shape, 1)
    gc_ref[0] = jnp.sum(jnp.where(lane == e, arow_ref[...], 0.0), axis=1, keepdims=True)


def _gather_call(idx_flat, h2t, aff_tm, *, batch, seq, cap, ne):
    d = ROW_TILE * LANES
    assert h2t.shape == (batch * seq * ROW_TILE, LANES)
    grid_spec = pltpu.PrefetchScalarGridSpec(
        num_scalar_prefetch=1,
        grid=(batch, ne),
        in_specs=[pl.BlockSpec((seq * ROW_TILE, LANES), lambda b, e, idx: (b, 0)),
                  pl.BlockSpec((seq, LANES), lambda b, e, idx: (b, 0))],
        out_specs=[pl.BlockSpec((1, cap, d), lambda b, e, idx: (e, b, 0)),
                   pl.BlockSpec((1, cap, 1), lambda b, e, idx: (e, b, 0))],
        scratch_shapes=[pltpu.VMEM((cap * ROW_TILE, LANES), F32), pltpu.VMEM((cap, LANES), F32)],
    )
    return pl.pallas_call(
        functools.partial(_gather_kernel, cap=cap, ne=ne),
        grid_spec=grid_spec,
        out_shape=[jax.ShapeDtypeStruct((ne, batch * cap, d), BF16),
                   jax.ShapeDtypeStruct((ne, batch * cap, 1), F32)],
        name="expert_gather",
        compiler_params=_cparams(
            ("arbitrary", "arbitrary"),
            blocks=[((seq * ROW_TILE, LANES), F32), ((seq, LANES), F32), ((1, cap, d), BF16),
                    ((1, cap, 1), F32)],
            scratch=[((cap * ROW_TILE, LANES), F32), ((cap, LANES), F32)],
            temps=[((cap, d), F32)] * 2),
    )(idx_flat, h2t, aff_tm)


def _ffn_kernel(x_ref, gc_ref, wg_ref, wu_ref, wd_ref, ye_ref, *, rows):
    f = pl.program_id(1)

    @pl.when(f == 0)
    def _init():
        ye_ref[...] = jnp.zeros_like(ye_ref)

    wg = wg_ref[0, 0].astype(BF16)
    wu = wu_ref[0, 0].astype(BF16)
    wd = wd_ref[0, 0].astype(BF16)
    nblk = x_ref.shape[1] // rows

    def up(r):
        x = x_ref[0, r * rows:(r + 1) * rows, :]
        return (jnp.dot(x, wg, preferred_element_type=F32), jnp.dot(x, wu, preferred_element_type=F32))

    nxt = up(0)
    for r in range(nblk):
        hg, hu = nxt
        if r + 1 < nblk:
            nxt = up(r + 1)
        hid = (hg * jax.nn.sigmoid(hg) * hu).astype(BF16)
        ye_ref[0, r * rows:(r + 1) * rows, :] += jnp.dot(hid, wd, preferred_element_type=F32)

    @pl.when(f == pl.num_programs(1) - 1)
    def _gate():
        ye_ref[0] = ye_ref[0] * gc_ref[0]


def _ffn_call(xg, gc, wg, wu, wd, *, layer, tf, rows):
    ne, n, d = xg.shape
    dff = wg.shape[3]
    return pl.pallas_call(
        functools.partial(_ffn_kernel, rows=rows),
        grid=(ne, dff // tf),
        in_specs=[pl.BlockSpec((1, n, d), lambda e, f: (e, 0, 0)),
                  pl.BlockSpec((1, n, 1), lambda e, f: (e, 0, 0)),
                  pl.BlockSpec((1, 1, d, tf), lambda e, f: (layer, e, 0, f)),
                  pl.BlockSpec((1, 1, d, tf), lambda e, f: (layer, e, 0, f)),
                  pl.BlockSpec((1, 1, tf, d), lambda e, f: (layer, e, f, 0))],
        out_specs=pl.BlockSpec((1, n, d), lambda e, f: (e, 0, 0)),
        out_shape=jax.ShapeDtypeStruct((ne, n, d), F32),
        name="expert_ffn",
        compiler_params=_cparams(
            ("arbitrary", "arbitrary"),
            blocks=[((1, n, d), BF16), ((1, n, 1), F32), ((1, 1, d, tf), F32), ((1, 1, d, tf), F32),
                    ((1, 1, tf, d), F32), ((1, n, d), F32)],
            temps=[((d, tf), BF16)] * 3 + [((rows, tf), F32)] * 5 + [((rows, d), F32)]),
    )(xg, gc, wg, wu, wd)


SCATTER_UNROLL = 8
SEQ_HALVES = 2


def _combine_kernel(idx_ref, half_ref, ye_ref, x1_ref, mod_ref, o_ref, acc_ref, *, cap, ne, rows,
                    d_model):
    i = pl.program_id(0)
    e = pl.program_id(1)
    b = i // SEQ_HALVES
    hh = i % SEQ_HALVES

    @pl.when(e == 0)
    def _init():
        o_ref[...] = jnp.zeros_like(o_ref)
        acc_ref[...] = jnp.zeros_like(acc_ref)

    mid = half_ref[b * ne + e]
    lo = jnp.where(hh == 0, 0, mid)
    hi = jnp.where(hh == 0, mid, cap)
    base = (b * ne + e) * cap
    off = hh * rows

    def add_row(c, dst_ref):
        r = idx_ref[base + c] - off
        dst_ref[pl.ds(r, 1), :] += ye_ref[0, pl.ds(c, 1), :]

    def group(g, carry):
        for k in range(SCATTER_UNROLL):
            add_row(lo + g * SCATTER_UNROLL + k, o_ref if k % 2 == 0 else acc_ref)
        return carry

    ngroup = (hi - lo) // SCATTER_UNROLL
    lax.fori_loop(0, ngroup, group, 0)

    def tail(c, carry):
        add_row(c, o_ref)
        return carry

    lax.fori_loop(lo + ngroup * SCATTER_UNROLL, hi, tail, 0)

    @pl.when(e == pl.num_programs(1) - 1)
    def _fin():
        g2 = mod_ref[0][:, 5 * d_model:6 * d_model]
        o_ref[...] = x1_ref[...] + g2 * (o_ref[...] + acc_ref[...])


def _combine_call(idx_flat, half_flat, ye, x1, mod3, *, batch, seq, cap):
    t, d = x1.shape
    ne = ye.shape[0]
    rows = seq // SEQ_HALVES
    grid_spec = pltpu.PrefetchScalarGridSpec(
        num_scalar_prefetch=2,
        grid=(batch * SEQ_HALVES, ne),
        in_specs=[pl.BlockSpec((1, cap, d), lambda i, e, idx, half: (e, i // SEQ_HALVES, 0)),
                  pl.BlockSpec((rows, d), lambda i, e, idx, half: (i, 0)),
                  pl.BlockSpec((1, 1, mod3.shape[2]), lambda i, e, idx, half: (i // SEQ_HALVES, 0, 0))],
        out_specs=pl.BlockSpec((rows, d), lambda i, e, idx, half: (i, 0)),
        scratch_shapes=[pltpu.VMEM((rows, d), F32)],
    )
    return pl.pallas_call(
        functools.partial(_combine_kernel, cap=cap, ne=ne, rows=rows, d_model=d),
        grid_spec=grid_spec,
        out_shape=jax.ShapeDtypeStruct((t, d), F32),
        name="moe_combine",
        compiler_params=_cparams(
            ("arbitrary", "arbitrary"),
            blocks=[((1, cap, d), F32), ((rows, d), F32), ((1, 1, mod3.shape[2]), F32), ((rows, d), F32)],
            scratch=[((rows, d), F32)],
            temps=[((SUBLANES_32BIT * SCATTER_UNROLL, d), F32)] * 4),
    )(idx_flat, half_flat, ye, x1, mod3)


def kernel(x, c, w_mod, b_mod, ln1_w, ln2_w, w_in, ssm_lam_re, ssm_lam_im, ssm_log_dt, ssm_b_re,
           ssm_b_im, ssm_c_re, ssm_c_im, ssm_d, w_glu, b_glu, q_norm_w, k_norm_w, w_ssm_up,
           w_attn_up, w_out, w_router, w_exp_gate, w_exp_up, w_exp_down):
    batch, seq, d = x.shape
    depth = w_mod.shape[0]
    t = batch * seq
    d_attn = N_Q_HEADS * HEAD_DIM
    d_kv = N_KV_HEADS * HEAD_DIM
    cap = CAPACITY_FACTOR * seq // N_EXPERTS
    tm = min(512, seq)
    tq = min(256, seq)

    tabs = _rope_tables(seq)
    head_id = jnp.arange(d_attn) // HEAD_DIM
    hm = (head_id[:, None] == head_id[None, :]).astype(BF16)
    tri = (jnp.arange(LANES)[:, None] <= jnp.arange(LANES)[None, :]).astype(BF16)
    c_pad = jnp.zeros((8, d), F32).at[:batch].set(c)
    mod_all = _mod_call(c_pad, w_mod, b_mod)[:, :batch]
    wa_all, wb_all, a_re_all, a_im_all = _s5_weights(ssm_lam_re, ssm_lam_im, ssm_log_dt, ssm_b_re,
                                                     ssm_b_im, ssm_c_re, ssm_c_im, ssm_d)

    x2 = x.reshape(t, d)
    for l in range(depth):
        mod3 = mod_all[l].reshape(batch, 1, 6 * d)
        qw = jnp.tile(q_norm_w[l], N_Q_HEADS).reshape(1, d_attn)
        kw = jnp.tile(k_norm_w[l], N_KV_HEADS).reshape(1, d_kv)
        u, q, k, v, gs, ga = _in_call(x2, mod3, ln1_w[l].reshape(1, d), w_in[l].astype(BF16), tabs,
                                      qw, kw, hm, seq=seq, tm=tm)
        y = _s5_mixer(u, wa_all, wb_all, a_re_all[l], a_im_all[l], layer=l, batch=batch, seq=seq)
        attn = _attn_call(q, k, v, seq=seq, tq=tq)
        x1, h2, aff_t, aff_tm = _post_call(
            y, attn, gs, ga, x2, mod3, ln2_w[l].reshape(1, d), w_glu[l].astype(BF16),
            b_glu[l].reshape(1, -1), w_ssm_up[l].astype(BF16), w_attn_up[l].astype(BF16),
            w_out[l].astype(BF16), w_router[l].T, seq=seq, tm=tm)
        idx_t, half = _select_call(aff_t, tri, cap=cap)
        idx_flat = jnp.swapaxes(idx_t, 1, 2).reshape(-1)
        half_flat = half.reshape(-1)
        xg, gc = _gather_call(idx_flat, h2, aff_tm, batch=batch, seq=seq, cap=cap, ne=N_EXPERTS)
        ye = _ffn_call(xg, gc, w_exp_gate, w_exp_up, w_exp_down, layer=l, tf=512,
                       rows=min(1024, batch * cap))
        x2 = _combine_call(idx_flat, half_flat, ye, x1, mod3, batch=batch, seq=seq, cap=cap)
    return x2.reshape(batch, seq, d)
```
